```python
import jax, jax.numpy as jnp
from jax import lax
import numpy as np

D_MODEL = 1024
BATCH = 16
SEQ = 2048
DEPTH = 1

HEAD_DIM = 64
HEADS_PER_GROUP = 4
DILATED_GROUPS = ((128, 1), (512, 4), (2048, 16))
N_GROUPS = 3
N_ATTN_HEADS = N_GROUPS * HEADS_PER_GROUP
ATTN_WIDTH = N_ATTN_HEADS * HEAD_DIM
ATTN_OUT_WIDTH = HEADS_PER_GROUP * HEAD_DIM
CONV_WIDTH = D_MODEL
CONV_K = 3
SUB_BLOCK = 128
ALIBI_MAX_EXP = 8.0
DEEPNORM_ALPHA = (2.0 * DEPTH) ** 0.25
DEEPNORM_BETA = (8.0 * DEPTH) ** -0.25
LN_EPS = 1e-5
IN_WIDTHS = (ATTN_WIDTH, ATTN_WIDTH, ATTN_WIDTH, ATTN_OUT_WIDTH,
             CONV_WIDTH, CONV_WIDTH, CONV_WIDTH, CONV_WIDTH, D_MODEL, D_MODEL)
D_IN = 3 * ATTN_WIDTH + ATTN_OUT_WIDTH + 4 * CONV_WIDTH + 2 * D_MODEL

kernel_name = "hybrid_dilated_attn_shortconv_deepnorm_adaln"


def _split_points():
    pts, acc = [], 0
    for w in IN_WIDTHS[:-1]:
        acc += w
        pts.append(acc)
    return pts


def _layer_norm(x, g, b):
    xf = x.astype(jnp.float32)
    mu = xf.mean(-1, keepdims=True)
    var = jnp.square(xf - mu).mean(-1, keepdims=True)
    y = (xf - mu) * lax.rsqrt(var + LN_EPS) * g.astype(jnp.float32) + b.astype(jnp.float32)
    return y.astype(x.dtype)


def _dilated_window_attention(q, k, v, window, dilation, slopes):
    bsz, s, h, dh = q.shape
    span = window // dilation
    n = s // dilation
    L = SUB_BLOCK
    nb = -(-n // L)
    pad = nb * L - n

    def to_sub(t):
        t = t.reshape(bsz, n, dilation, h, dh).transpose(0, 2, 1, 3, 4)
        t = jnp.pad(t, ((0, 0), (0, 0), (0, pad), (0, 0), (0, 0)))
        return t.reshape(bsz, dilation, nb, L, h, dh)

    def with_prev(t):
        prev = jnp.pad(t, ((0, 0), (0, 0), (1, 0), (0, 0), (0, 0), (0, 0)))[:, :, :-1]
        return jnp.concatenate([prev, t], axis=3)

    qs = to_sub(q)
    kb = with_prev(to_sub(k))
    vb = with_prev(to_sub(v))

    scores = jnp.einsum('brnqhd,brnkhd->brnhqk', qs, kb).astype(jnp.float32) * (dh ** -0.5)
    qi = jnp.arange(L)[:, None]
    kj = jnp.arange(2 * L)[None, :]
    delta = qi + L - kj
    key_sub = jnp.arange(nb)[:, None, None] * L + kj[None] - L
    valid = (delta >= 0) & (delta <= span) & (key_sub >= 0)
    alibi = -slopes[:, None, None] * (delta * dilation).astype(jnp.float32)[None]
    scores = scores + alibi[None, None, None]
    scores = jnp.where(valid[None, None, :, None], scores, -jnp.inf)
    m = scores.max(-1, keepdims=True)
    p = jnp.exp(scores - m)
    den = p.sum(-1, keepdims=True)
    out = jnp.einsum('brnhqk,brnkhd->brnqhd', (p / den).astype(v.dtype), vb)
    lse = jnp.swapaxes((m + jnp.log(den))[..., 0], 3, 4)

    def from_sub(t):
        t = t.reshape((bsz, dilation, nb * L) + t.shape[4:])[:, :, :n]
        return jnp.swapaxes(t, 1, 2).reshape((bsz, s) + t.shape[3:])

    return from_sub(out), from_sub(lse)


def setup_inputs(seed: int = 0) -> dict:
    key = jax.random.key(seed)
    ks = jax.random.split(key, 16)
    f32 = jnp.float32
    x = jax.random.normal(ks[0], (BATCH, SEQ, D_MODEL), f32)
    c = jax.random.normal(ks[1], (BATCH, D_MODEL), f32)
    w_ada = jax.random.normal(ks[2], (DEPTH, D_MODEL, 3 * D_MODEL), f32) * (D_MODEL ** -0.5) * 0.5
    b_ada = jax.random.normal(ks[3], (DEPTH, 3 * D_MODEL), f32) * 0.01
    w_in = jax.random.normal(ks[4], (DEPTH, D_MODEL, D_IN), f32) * (D_MODEL ** -0.5)
    w_in = w_in.at[:, :, 2 * ATTN_WIDTH:3 * ATTN_WIDTH].multiply(DEEPNORM_BETA)
    b_in = jax.random.normal(ks[5], (DEPTH, D_IN), f32) * 0.01
    conv_w = jax.random.normal(ks[6], (DEPTH, CONV_K, CONV_WIDTH), f32) * (CONV_K ** -0.5)
    w_proj_attn = jax.random.normal(ks[7], (DEPTH, ATTN_OUT_WIDTH, D_MODEL), f32) * (ATTN_OUT_WIDTH ** -0.5) * DEEPNORM_BETA
    w_proj_conv = jax.random.normal(ks[8], (DEPTH, CONV_WIDTH, D_MODEL), f32) * (CONV_WIDTH ** -0.5) * DEEPNORM_BETA
    w_out = jax.random.normal(ks[9], (DEPTH, D_MODEL, D_MODEL), f32) * (D_MODEL ** -0.5) * DEEPNORM_BETA
    b_out = jax.random.normal(ks[10], (DEPTH, D_MODEL), f32) * 0.01
    ln_g = 1.0 + 0.02 * jax.random.normal(ks[11], (DEPTH, D_MODEL), f32)
    ln_b = 0.02 * jax.random.normal(ks[12], (DEPTH, D_MODEL), f32)
    return {"x": x, "c": c, "w_ada": w_ada, "b_ada": b_ada, "w_in": w_in, "b_in": b_in,
            "conv_w": conv_w, "w_proj_attn": w_proj_attn, "w_proj_conv": w_proj_conv,
            "w_out": w_out, "b_out": b_out, "ln_g": ln_g, "ln_b": ln_b}


def reference(x, c, w_ada, b_ada, w_in, b_in, conv_w, w_proj_attn, w_proj_conv, w_out, b_out, ln_g, ln_b):
    bsz, s, _ = x.shape
    split_pts = _split_points()
    slopes = 2.0 ** (-ALIBI_MAX_EXP * (jnp.arange(N_ATTN_HEADS, dtype=jnp.float32) + 1.0) / N_ATTN_HEADS)
    c_act = jax.nn.silu(c)
    for layer in range(DEPTH):
        ada = (c_act @ w_ada[layer] + b_ada[layer])[:, None, :]
        shift, scale, gate = jnp.split(ada, 3, axis=-1)
        h = x * (1.0 + scale) + shift

        proj = h @ w_in[layer] + b_in[layer]
        q, k, v, z_attn, u_x, g_b, g_c, z_conv, g_mix_a, g_mix_b = jnp.split(proj, split_pts, axis=-1)
        q = q.reshape(bsz, s, N_ATTN_HEADS, HEAD_DIM)
        k = k.reshape(bsz, s, N_ATTN_HEADS, HEAD_DIM)
        v = v.reshape(bsz, s, N_ATTN_HEADS, HEAD_DIM)

        outs, lses = [], []
        for g, (window, dilation) in enumerate(DILATED_GROUPS):
            hs = slice(g * HEADS_PER_GROUP, (g + 1) * HEADS_PER_GROUP)
            o_g, lse_g = _dilated_window_attention(q[:, :, hs], k[:, :, hs], v[:, :, hs],
                                                   window, dilation, slopes[hs])
            outs.append(o_g.astype(jnp.float32))
            lses.append(lse_g)
        mix_w = jax.nn.softmax(jnp.stack(lses), axis=0)
        o_attn = jnp.einsum('gbsh,gbshd->bshd', mix_w, jnp.stack(outs)).astype(x.dtype)
        o_attn = o_attn.reshape(bsz, s, ATTN_OUT_WIDTH)
        y_attn = (o_attn * jax.nn.silu(z_attn)) @ w_proj_attn[layer]

        u = g_c * u_x
        u_pad = jnp.pad(u, ((0, 0), (CONV_K - 1, 0), (0, 0)))
        cw = conv_w[layer]
        conv = cw[0] * u_pad[:, 0:s]
        for j in range(1, CONV_K):
            conv = conv + cw[j] * u_pad[:, j:j + s]
        y_conv = (g_b * conv * jax.nn.silu(z_conv)) @ w_proj_conv[layer]

        merged = jax.nn.sigmoid(g_mix_a) * y_attn + jax.nn.sigmoid(g_mix_b) * y_conv
        sub = gate * (merged @ w_out[layer] + b_out[layer])
        x = _layer_norm(DEEPNORM_ALPHA * x + sub, ln_g[layer], ln_b[layer])
    return x
```

```python
import functools

import numpy as np
import jax
import jax.numpy as jnp
from jax import lax
from jax.experimental import pallas as pl
from jax.experimental.pallas import tpu as pltpu

D_MODEL = 1024
HEAD_DIM = 64
HEADS_PER_GROUP = 4
DILATED_GROUPS = ((128, 1), (512, 4), (2048, 16))
N_GROUPS = len(DILATED_GROUPS)
N_ATTN_HEADS = N_GROUPS * HEADS_PER_GROUP
ATTN_WIDTH = N_ATTN_HEADS * HEAD_DIM
GROUP_WIDTH = HEADS_PER_GROUP * HEAD_DIM
QKV_WIDTH = 3 * ATTN_WIDTH
CONV_WIDTH = D_MODEL
CONV_K = 3
SUB_BLOCK = 128
ALIBI_MAX_EXP = 8.0
LN_EPS = 1e-5

OFF_Z_ATTN = QKV_WIDTH
OFF_UX = OFF_Z_ATTN + GROUP_WIDTH
OFF_GB = OFF_UX + CONV_WIDTH
OFF_GC = OFF_GB + CONV_WIDTH
OFF_ZC = OFF_GC + CONV_WIDTH
OFF_GA = OFF_ZC + CONV_WIDTH
OFF_GMB = OFF_GA + D_MODEL
D_IN = OFF_GMB + D_MODEL

ROW_TILE = 512
CONV_CHUNK = 256
HALO = 8
VMEM_LIMIT = 56 * 1024 * 1024

F32 = jnp.float32
BF16 = jnp.bfloat16


def _sigmoid(v):
    return 1.0 / (1.0 + jnp.exp(-v))


def _silu(v):
    return v / (1.0 + jnp.exp(-v))


def _ada_kernel(c_ref, w_ref, b_ref, o_ref):
    o_ref[...] = jnp.dot(_silu(c_ref[...]), w_ref[...], preferred_element_type=F32,
                         precision=lax.Precision.HIGHEST) + b_ref[...]


def _ada(c, w_ada, b_ada):
    bsz, d = c.shape
    n_out = w_ada.shape[1]
    return pl.pallas_call(
        _ada_kernel,
        grid=(n_out // d,),
        in_specs=[pl.BlockSpec((bsz, d), lambda j: (0, 0)),
                  pl.BlockSpec((d, d), lambda j: (0, j)),
                  pl.BlockSpec((1, d), lambda j: (0, j))],
        out_specs=pl.BlockSpec((bsz, d), lambda j: (0, j)),
        out_shape=jax.ShapeDtypeStruct((bsz, n_out), F32),
        name="ada",
    )(c, w_ada, b_ada.reshape(1, n_out))


def _inproj_kernel(x_ref, ada_ref, w_ref, b_ref, cw_ref, wpc_ref,
                   qkv_ref, za_ref, ga_ref, mb_ref, u_scr, t_scr):
    tm = x_ref.shape[0]
    first_tile = pl.program_id(1) == 0
    shift = ada_ref[0:1, :]
    scale = ada_ref[1:2, :]
    h = (x_ref[...] * (1.0 + scale) + shift).astype(BF16)

    def proj(lo, width):
        return (jnp.dot(h, w_ref[:, lo:lo + width], preferred_element_type=F32)
                + b_ref[:, lo:lo + width])

    qkv_ref[...] = proj(0, QKV_WIDTH).astype(BF16)
    za_ref[...] = _silu(proj(OFF_Z_ATTN, GROUP_WIDTH)).astype(BF16)
    ga_ref[...] = _sigmoid(proj(OFF_GA, D_MODEL)).astype(BF16)

    @pl.when(first_tile)
    def _():
        u_scr[0:HALO, :] = jnp.zeros((HALO, CONV_WIDTH), F32)

    @pl.when(jnp.logical_not(first_tile))
    def _():
        u_scr[0:HALO, :] = u_scr[tm:tm + HALO, :]

    for c0 in range(0, CONV_WIDTH, CONV_CHUNK):
        cs = slice(c0, c0 + CONV_CHUNK)
        u = proj(OFF_GC + c0, CONV_CHUNK) * proj(OFF_UX + c0, CONV_CHUNK)
        u_scr[HALO:HALO + tm, cs] = u
        conv = cw_ref[0:1, cs] * u_scr[HALO - 2:HALO - 2 + tm, cs]
        conv = conv + cw_ref[1:2, cs] * u_scr[HALO - 1:HALO - 1 + tm, cs]
        conv = conv + cw_ref[2:3, cs] * u
        t = proj(OFF_GB + c0, CONV_CHUNK) * conv * _silu(proj(OFF_ZC + c0, CONV_CHUNK))
        t_scr[:, cs] = t.astype(BF16)
    y_conv = jnp.dot(t_scr[...], wpc_ref[...], preferred_element_type=F32)
    mb_ref[...] = (_sigmoid(proj(OFF_GMB, D_MODEL)) * y_conv).astype(BF16)


def _inproj(x, ada3, w_in, b_in, conv_w, w_pc):
    bsz, s, d = x.shape
    tm = ROW_TILE
    const = dict(pipeline_mode=pl.Buffered(1))
    row = lambda width: pl.BlockSpec((None, tm, width), lambda b, i: (b, i, 0))
    return pl.pallas_call(
        _inproj_kernel,
        grid=(bsz, s // tm),
        in_specs=[row(d),
                  pl.BlockSpec((None, 3, d), lambda b, i: (b, 0, 0)),
                  pl.BlockSpec((d, D_IN), lambda b, i: (0, 0), **const),
                  pl.BlockSpec((1, D_IN), lambda b, i: (0, 0), **const),
                  pl.BlockSpec((CONV_K, CONV_WIDTH), lambda b, i: (0, 0), **const),
                  pl.BlockSpec((CONV_WIDTH, d), lambda b, i: (0, 0), **const)],
        out_specs=[row(QKV_WIDTH), row(GROUP_WIDTH), row(d), row(d)],
        out_shape=[jax.ShapeDtypeStruct((bsz, s, QKV_WIDTH), BF16),
                   jax.ShapeDtypeStruct((bsz, s, GROUP_WIDTH), BF16),
                   jax.ShapeDtypeStruct((bsz, s, d), BF16),
                   jax.ShapeDtypeStruct((bsz, s, d), BF16)],
        scratch_shapes=[pltpu.VMEM((tm + HALO, CONV_WIDTH), F32),
                        pltpu.VMEM((tm, CONV_WIDTH), BF16)],
        compiler_params=pltpu.CompilerParams(
            dimension_semantics=("arbitrary", "arbitrary"), vmem_limit_bytes=VMEM_LIMIT),
        name="inproj",
    )(x, ada3, w_in, b_in, conv_w, w_pc)


def _attn_bias(group):
    _, dilation = DILATED_GROUPS[group]
    window = DILATED_GROUPS[group][0]
    span = window // dilation
    L = SUB_BLOCK
    heads = np.arange(group * HEADS_PER_GROUP, (group + 1) * HEADS_PER_GROUP, dtype=np.float64)
    slopes = 2.0 ** (-ALIBI_MAX_EXP * (heads + 1.0) / N_ATTN_HEADS)
    delta = (np.arange(L)[:, None] + L - np.arange(2 * L)[None, :]).astype(np.float64)
    valid = (delta >= 0) & (delta <= span)
    bias = -slopes[:, None, None] * (delta * dilation)[None]
    return np.where(valid[None], bias, -np.inf).astype(np.float32)


def _attn_kernel(q_ref, k_ref, v_ref, bias_ref, o_ref, l_ref, *, n_blocks):
    L = SUB_BLOCK
    pair_w = 2 * HEAD_DIM
    low_half = lax.broadcasted_iota(jnp.int32, (L, pair_w), 1) < HEAD_DIM

    def one_block(r0, k0, n_keys):
        for pair in range(HEADS_PER_GROUP // 2):
            cs = slice(pair * pair_w, (pair + 1) * pair_w)
            q = q_ref[pl.ds(r0, L), cs] * 0.125
            k = k_ref[pl.ds(k0, n_keys), cs]
            v = v_ref[pl.ds(k0, n_keys), cs]
            outs, lses = [], []
            for hh in range(2):
                qm = jnp.where(low_half if hh == 0 else ~low_half, q, jnp.zeros_like(q))
                sc = lax.dot_general(qm, k, (((1,), (1,)), ((), ())), preferred_element_type=F32)
                sc = sc + bias_ref[2 * pair + hh, :, 2 * L - n_keys:]
                m = jnp.max(sc, axis=-1, keepdims=True)
                p = jnp.exp(sc - m)
                den = jnp.sum(p, axis=-1, keepdims=True)
                pv = jnp.dot(p.astype(BF16), v, preferred_element_type=F32)
                outs.append(pv / den)
                lses.append(jnp.broadcast_to(m + jnp.log(den), (L, pair_w)))
            o_ref[pl.ds(r0, L), cs] = jnp.where(low_half, outs[0], outs[1])
            l_ref[pl.ds(r0, L), cs] = jnp.where(low_half, lses[0], lses[1])

    one_block(0, 0, L)
    if n_blocks > 1:
        def body(jb, carry):
            r0 = pl.multiple_of(jb * L, L)
            one_block(r0, pl.multiple_of(r0 - L, L), 2 * L)
            return carry
        lax.fori_loop(1, n_blocks, body, 0)


def _attention_group(qkv, group):
    bsz, s, _ = qkv.shape
    _, d = DILATED_GROUPS[group]
    n = s // d
    blocks_per_row = QKV_WIDTH // GROUP_WIDTH
    view = qkv.reshape(bsz, n, d * QKV_WIDTH)
    bias = jnp.asarray(_attn_bias(group))

    def qkv_spec(which):
        return pl.BlockSpec((None, n, GROUP_WIDTH),
                            lambda b, r: (b, 0, r * blocks_per_row + which * N_GROUPS + group))

    out_spec = pl.BlockSpec((None, n, GROUP_WIDTH), lambda b, r: (b, 0, r))
    out_shape = jax.ShapeDtypeStruct((bsz, n, d * GROUP_WIDTH), F32)
    o, l = pl.pallas_call(
        functools.partial(_attn_kernel, n_blocks=n // SUB_BLOCK),
        grid=(bsz, d),
        in_specs=[qkv_spec(0), qkv_spec(1), qkv_spec(2),
                  pl.BlockSpec(bias.shape, lambda b, r: (0, 0, 0))],
        out_specs=[out_spec, out_spec],
        out_shape=[out_shape, out_shape],
        compiler_params=pltpu.CompilerParams(
            dimension_semantics=("arbitrary", "arbitrary"), vmem_limit_bytes=VMEM_LIMIT),
        name=f"attn_g{group}",
    )(view, view, view, bias)
    return o.reshape(bsz, s, GROUP_WIDTH), l.reshape(bsz, s, GROUP_WIDTH)


def _out_kernel(x_ref, ada_ref, o0_ref, o1_ref, o2_ref, l0_ref, l1_ref, l2_ref,
                za_ref, ga_ref, mb_ref, wpa_ref, wout_ref, bout_ref, lng_ref, lnb_ref,
                out_ref, *, alpha):
    l0, l1, l2 = l0_ref[...], l1_ref[...], l2_ref[...]
    lmax = jnp.maximum(jnp.maximum(l0, l1), l2)
    e0, e1, e2 = jnp.exp(l0 - lmax), jnp.exp(l1 - lmax), jnp.exp(l2 - lmax)
    o_attn = (e0 * o0_ref[...] + e1 * o1_ref[...] + e2 * o2_ref[...]) / (e0 + e1 + e2)
    ta = (o_attn * za_ref[...].astype(F32)).astype(BF16)
    y_attn = jnp.dot(ta, wpa_ref[...], preferred_element_type=F32)
    merged = (ga_ref[...].astype(F32) * y_attn + mb_ref[...].astype(F32)).astype(BF16)
    gate = ada_ref[2:3, :]
    sub = gate * (jnp.dot(merged, wout_ref[...], preferred_element_type=F32) + bout_ref[...])
    r = alpha * x_ref[...] + sub
    mu = jnp.mean(r, axis=-1, keepdims=True)
    cen = r - mu
    var = jnp.mean(cen * cen, axis=-1, keepdims=True)
    out_ref[...] = cen * lax.rsqrt(var + LN_EPS) * lng_ref[...] + lnb_ref[...]


def _out(x, ada3, os, ls, za, ga, mb, w_pa, w_out, b_out, ln_g, ln_b, alpha):
    bsz, s, d = x.shape
    tm = ROW_TILE
    const = dict(pipeline_mode=pl.Buffered(1))
    row = lambda width: pl.BlockSpec((None, tm, width), lambda b, i: (b, i, 0))
    vec = pl.BlockSpec((1, d), lambda b, i: (0, 0), **const)
    return pl.pallas_call(
        functools.partial(_out_kernel, alpha=alpha),
        grid=(bsz, s // tm),
        in_specs=[row(d), pl.BlockSpec((None, 3, d), lambda b, i: (b, 0, 0))]
                 + [row(GROUP_WIDTH)] * 6
                 + [row(GROUP_WIDTH), row(d), row(d),
                    pl.BlockSpec((GROUP_WIDTH, d), lambda b, i: (0, 0), **const),
                    pl.BlockSpec((d, d), lambda b, i: (0, 0), **const),
                    vec, vec, vec],
        out_specs=row(d),
        out_shape=jax.ShapeDtypeStruct((bsz, s, d), F32),
        compiler_params=pltpu.CompilerParams(
            dimension_semantics=("arbitrary", "arbitrary"), vmem_limit_bytes=VMEM_LIMIT),
        name="out",
    )(x, ada3, *os, *ls, za, ga, mb, w_pa, w_out, b_out, ln_g, ln_b)


def kernel(x, c, w_ada, b_ada, w_in, b_in, conv_w, w_proj_attn, w_proj_conv, w_out, b_out, ln_g, ln_b):
    bsz, s, d = x.shape
    depth = w_in.shape[0]
    alpha = (2.0 * depth) ** 0.25
    for layer in range(depth):
        ada3 = _ada(c, w_ada[layer], b_ada[layer]).reshape(bsz, 3, d)
        qkv, za, ga, mb = _inproj(x, ada3, w_in[layer].astype(BF16), b_in[layer].reshape(1, D_IN),
                                  conv_w[layer], w_proj_conv[layer].astype(BF16))
        outs = [_attention_group(qkv, g) for g in range(N_GROUPS)]
        x = _out(x, ada3, [o for o, _ in outs], [l for _, l in outs], za, ga, mb,
                 w_proj_attn[layer].astype(BF16), w_out[layer].astype(BF16),
                 b_out[layer].reshape(1, d), ln_g[layer].reshape(1, d), ln_b[layer].reshape(1, d), alpha)
    return x
```

```python
import functools

import numpy as np
import jax
import jax.numpy as jnp
from jax import lax
from jax.experimental import pallas as pl
from jax.experimental.pallas import tpu as pltpu

D_MODEL = 1024
HEAD_DIM = 64
HEADS_PER_GROUP = 4
DILATED_GROUPS = ((128, 1), (512, 4), (2048, 16))
N_GROUPS = len(DILATED_GROUPS)
N_ATTN_HEADS = N_GROUPS * HEADS_PER_GROUP
ATTN_WIDTH = N_ATTN_HEADS * HEAD_DIM
GROUP_WIDTH = HEADS_PER_GROUP * HEAD_DIM
QKV_WIDTH = 3 * ATTN_WIDTH
CONV_WIDTH = D_MODEL
CONV_K = 3
SUB_BLOCK = 128
ALIBI_MAX_EXP = 8.0
LN_EPS = 1e-5

OFF_Z_ATTN = QKV_WIDTH
OFF_UX = OFF_Z_ATTN + GROUP_WIDTH
OFF_GB = OFF_UX + CONV_WIDTH
OFF_GC = OFF_GB + CONV_WIDTH
OFF_ZC = OFF_GC + CONV_WIDTH
OFF_GA = OFF_ZC + CONV_WIDTH
OFF_GMB = OFF_GA + D_MODEL
D_IN = OFF_GMB + D_MODEL

ROW_TILE = 512
CONV_CHUNK = 256
HALO = 8
VMEM_LIMIT = 56 * 1024 * 1024

F32 = jnp.float32
BF16 = jnp.bfloat16


def _sigmoid(v):
    return 1.0 / (1.0 + jnp.exp(-v))


def _silu(v):
    return v / (1.0 + jnp.exp(-v))


def _ada_kernel(c_ref, w_ref, b_ref, o_ref):
    o_ref[...] = jnp.dot(_silu(c_ref[...]), w_ref[...], preferred_element_type=F32,
                         precision=lax.Precision.HIGHEST) + b_ref[...]


def _ada(c, w_ada, b_ada):
    bsz, d = c.shape
    n_out = w_ada.shape[1]
    return pl.pallas_call(
        _ada_kernel,
        grid=(n_out // d,),
        in_specs=[pl.BlockSpec((bsz, d), lambda j: (0, 0)),
                  pl.BlockSpec((d, d), lambda j: (0, j)),
                  pl.BlockSpec((1, d), lambda j: (0, j))],
        out_specs=pl.BlockSpec((bsz, d), lambda j: (0, j)),
        out_shape=jax.ShapeDtypeStruct((bsz, n_out), F32),
        name="ada",
    )(c, w_ada, b_ada.reshape(1, n_out))


def _inproj_kernel(x_ref, ada_ref, w_ref, b_ref, cw_ref, wpc_ref,
                   qkv_ref, za_ref, ga_ref, mb_ref, u_scr, t_scr):
    tm = x_ref.shape[0]
    first_tile = pl.program_id(1) == 0
    shift = ada_ref[0:1, :]
    scale = ada_ref[1:2, :]
    h = (x_ref[...] * (1.0 + scale) + shift).astype(BF16)

    def proj(lo, width):
        return (jnp.dot(h, w_ref[:, lo:lo + width], preferred_element_type=F32)
                + b_ref[:, lo:lo + width])

    qkv_ref[...] = proj(0, QKV_WIDTH).astype(BF16)
    za_ref[...] = _silu(proj(OFF_Z_ATTN, GROUP_WIDTH)).astype(BF16)
    ga_ref[...] = _sigmoid(proj(OFF_GA, D_MODEL)).astype(BF16)

    @pl.when(first_tile)
    def _():
        u_scr[0:HALO, :] = jnp.zeros((HALO, CONV_WIDTH), F32)

    @pl.when(jnp.logical_not(first_tile))
    def _():
        u_scr[0:HALO, :] = u_scr[tm:tm + HALO, :]

    for c0 in range(0, CONV_WIDTH, CONV_CHUNK):
        cs = slice(c0, c0 + CONV_CHUNK)
        u = proj(OFF_GC + c0, CONV_CHUNK) * proj(OFF_UX + c0, CONV_CHUNK)
        u_scr[HALO:HALO + tm, cs] = u
        conv = cw_ref[0:1, cs] * u_scr[HALO - 2:HALO - 2 + tm, cs]
        conv = conv + cw_ref[1:2, cs] * u_scr[HALO - 1:HALO - 1 + tm, cs]
        conv = conv + cw_ref[2:3, cs] * u
        t = proj(OFF_GB + c0, CONV_CHUNK) * conv * _silu(proj(OFF_ZC + c0, CONV_CHUNK))
        t_scr[:, cs] = t.astype(BF16)
    y_conv = jnp.dot(t_scr[...], wpc_ref[...], preferred_element_type=F32)
    mb_ref[...] = (_sigmoid(proj(OFF_GMB, D_MODEL)) * y_conv).astype(BF16)


def _inproj(x, ada3, w_in, b_in, conv_w, w_pc):
    bsz, s, d = x.shape
    tm = ROW_TILE
    const = dict(pipeline_mode=pl.Buffered(1))
    row = lambda width: pl.BlockSpec((None, tm, width), lambda b, i: (b, i, 0))
    return pl.pallas_call(
        _inproj_kernel,
        grid=(bsz, s // tm),
        in_specs=[row(d),
                  pl.BlockSpec((None, 3, d), lambda b, i: (b, 0, 0)),
                  pl.BlockSpec((d, D_IN), lambda b, i: (0, 0), **const),
                  pl.BlockSpec((1, D_IN), lambda b, i: (0, 0), **const),
                  pl.BlockSpec((CONV_K, CONV_WIDTH), lambda b, i: (0, 0), **const),
                  pl.BlockSpec((CONV_WIDTH, d), lambda b, i: (0, 0), **const)],
        out_specs=[row(QKV_WIDTH), row(GROUP_WIDTH), row(d), row(d)],
        out_shape=[jax.ShapeDtypeStruct((bsz, s, QKV_WIDTH), BF16),
                   jax.ShapeDtypeStruct((bsz, s, GROUP_WIDTH), BF16),
                   jax.ShapeDtypeStruct((bsz, s, d), BF16),
                   jax.ShapeDtypeStruct((bsz, s, d), BF16)],
        scratch_shapes=[pltpu.VMEM((tm + HALO, CONV_WIDTH), F32),
                        pltpu.VMEM((tm, CONV_WIDTH), BF16)],
        compiler_params=pltpu.CompilerParams(
            dimension_semantics=("arbitrary", "arbitrary"), vmem_limit_bytes=VMEM_LIMIT),
        name="inproj",
    )(x, ada3, w_in, b_in, conv_w, w_pc)


def _attn_bias_table():
    L = SUB_BLOCK
    tabs = []
    for group, (window, dilation) in enumerate(DILATED_GROUPS):
        span = window // dilation
        heads = np.arange(group * HEADS_PER_GROUP, (group + 1) * HEADS_PER_GROUP, dtype=np.float64)
        slopes = 2.0 ** (-ALIBI_MAX_EXP * (heads + 1.0) / N_ATTN_HEADS)
        delta = (np.arange(L)[:, None] + L - np.arange(2 * L)[None, :]).astype(np.float64)
        valid = (delta >= 0) & (delta <= span)
        bias = np.where(valid[None], -slopes[:, None, None] * (delta * dilation)[None], -np.inf)
        tabs.append(bias.reshape(HEADS_PER_GROUP // 2, 2 * L, 2 * L))
    return np.stack(tabs).astype(np.float32)


def _attn_kernel(qkv_ref, bias_ref, o_ref, nat_scr, sub_scr, res_o, res_l):
    S = qkv_ref.shape[0]
    L = SUB_BLOCK
    pair_w = 2 * HEAD_DIM
    n_pairs = HEADS_PER_GROUP // 2
    low_half = lax.broadcasted_iota(jnp.int32, (L, pair_w), 1) < HEAD_DIM

    for g in range(1, N_GROUPS):
        d = DILATED_GROUPS[g][1]
        n = S // d
        for which in range(3):
            c0 = which * ATTN_WIDTH + g * GROUP_WIDTH
            for pair in range(n_pairs):
                cs = slice(pair * pair_w, (pair + 1) * pair_w)
                nat_scr[...] = qkv_ref[:, c0 + pair * pair_w:c0 + (pair + 1) * pair_w].astype(F32)
                for r in range(d):
                    sub_scr[g - 1, which, r * n:(r + 1) * n, cs] = (
                        nat_scr[pl.ds(r, n, stride=d), :].astype(BF16))

    def load(g, which, row0, n_rows, pair):
        cs = pair * pair_w
        if g == 0:
            c0 = which * ATTN_WIDTH + cs
            return qkv_ref[pl.ds(row0, n_rows), c0:c0 + pair_w]
        return sub_scr[g - 1, which, pl.ds(row0, n_rows), cs:cs + pair_w]

    def block(g, q0, k0, n_keys, out0):
        d = DILATED_GROUPS[g][1]
        scores = []
        for pair in range(n_pairs):
            q = load(g, 0, q0, L, pair) * 0.125
            zero = jnp.zeros_like(q)
            q2 = jnp.concatenate([jnp.where(low_half, q, zero), jnp.where(low_half, zero, q)], axis=0)
            k = load(g, 1, k0, n_keys, pair)
            sc = lax.dot_general(q2, k, (((1,), (1,)), ((), ())), preferred_element_type=F32)
            scores.append(sc + bias_ref[g, pair, :, 2 * L - n_keys:])
        probs = []
        for sc in scores:
            m = jnp.max(sc, axis=-1, keepdims=True)
            p = jnp.exp(sc - m)
            den = jnp.sum(p, axis=-1, keepdims=True)
            probs.append((p.astype(BF16), m, den))
        rows = pl.ds(out0, L) if d == 1 else pl.ds(out0, L, stride=d)
        for pair, (p, m, den) in enumerate(probs):
            pv = jnp.dot(p, load(g, 2, k0, n_keys, pair), preferred_element_type=F32) / den
            lse = jnp.broadcast_to(m + jnp.log(den), (2 * L, pair_w))
            res_o[g, pair, rows, :] = jnp.where(low_half, pv[:L], pv[L:])
            res_l[g, pair, rows, :] = jnp.where(low_half, lse[:L], lse[L:])

    for g, (_, d) in enumerate(DILATED_GROUPS):
        n = S // d
        n_blocks = n // L
        if n_blocks == 1:
            def sub_body(r, carry, g=g):
                row0 = pl.multiple_of(r * L, L)
                block(g, row0, row0, L, r)
                return carry
            lax.fori_loop(0, d, sub_body, 0)
            continue
        for r in range(d):
            block(g, r * n, r * n, L, r)

            def blk_body(jb, carry, g=g, d=d, r=r, n=n):
                q0 = pl.multiple_of(r * n + jb * L, L)
                out0 = jb * (L * d) + r
                if d == 1:
                    out0 = pl.multiple_of(out0, L)
                block(g, q0, pl.multiple_of(q0 - L, L), 2 * L, out0)
                return carry
            lax.fori_loop(1, n_blocks, blk_body, 0)

    chunk = 2 * L

    def combine(i, carry):
        rows = pl.ds(pl.multiple_of(i * chunk, chunk), chunk)
        for pair in range(n_pairs):
            ls = [res_l[g, pair, rows, :] for g in range(N_GROUPS)]
            lmax = functools.reduce(jnp.maximum, ls)
            es = [jnp.exp(l - lmax) for l in ls]
            num = sum(e * res_o[g, pair, rows, :] for g, e in enumerate(es))
            o_ref[rows, pair * pair_w:(pair + 1) * pair_w] = num / sum(es)
        return carry
    lax.fori_loop(0, S // chunk, combine, 0)


def _attention(qkv):
    bsz, s, _ = qkv.shape
    bias = jnp.asarray(_attn_bias_table())
    return pl.pallas_call(
        _attn_kernel,
        grid=(bsz,),
        in_specs=[pl.BlockSpec((None, s, QKV_WIDTH), lambda b: (b, 0, 0)),
                  pl.BlockSpec(bias.shape, lambda b: (0, 0, 0, 0), pipeline_mode=pl.Buffered(1))],
        out_specs=pl.BlockSpec((None, s, GROUP_WIDTH), lambda b: (b, 0, 0)),
        out_shape=jax.ShapeDtypeStruct((bsz, s, GROUP_WIDTH), F32),
        scratch_shapes=[pltpu.VMEM((s, 2 * HEAD_DIM), F32),
                        pltpu.VMEM((N_GROUPS - 1, 3, s, GROUP_WIDTH), BF16),
                        pltpu.VMEM((N_GROUPS, HEADS_PER_GROUP // 2, s, 2 * HEAD_DIM), F32),
                        pltpu.VMEM((N_GROUPS, HEADS_PER_GROUP // 2, s, 2 * HEAD_DIM), F32)],
        compiler_params=pltpu.CompilerParams(
            dimension_semantics=("arbitrary",), vmem_limit_bytes=VMEM_LIMIT),
        name="attn",
    )(qkv, bias)


def _out_kernel(x_ref, ada_ref, oa_ref, za_ref, ga_ref, mb_ref, wpa_ref, wout_ref, bout_ref,
                lng_ref, lnb_ref, out_ref, *, alpha):
    ta = (oa_ref[...] * za_ref[...].astype(F32)).astype(BF16)
    y_attn = jnp.dot(ta, wpa_ref[...], preferred_element_type=F32)
    merged = (ga_ref[...].astype(F32) * y_attn + mb_ref[...].astype(F32)).astype(BF16)
    gate = ada_ref[2:3, :]
    sub = gate * (jnp.dot(merged, wout_ref[...], preferred_element_type=F32) + bout_ref[...])
    r = alpha * x_ref[...] + sub
    mu = jnp.mean(r, axis=-1, keepdims=True)
    cen = r - mu
    var = jnp.mean(cen * cen, axis=-1, keepdims=True)
    out_ref[...] = cen * lax.rsqrt(var + LN_EPS) * lng_ref[...] + lnb_ref[...]


def _out(x, ada3, o_attn, za, ga, mb, w_pa, w_out, b_out, ln_g, ln_b, alpha):
    bsz, s, d = x.shape
    tm = ROW_TILE
    const = dict(pipeline_mode=pl.Buffered(1))
    row = lambda width: pl.BlockSpec((None, tm, width), lambda b, i: (b, i, 0))
    vec = pl.BlockSpec((1, d), lambda b, i: (0, 0), **const)
    return pl.pallas_call(
        functools.partial(_out_kernel, alpha=alpha),
        grid=(bsz, s // tm),
        in_specs=[row(d), pl.BlockSpec((None, 3, d), lambda b, i: (b, 0, 0))]
                 + [row(GROUP_WIDTH), row(GROUP_WIDTH), row(d), row(d),
                    pl.BlockSpec((GROUP_WIDTH, d), lambda b, i: (0, 0), **const),
                    pl.BlockSpec((d, d), lambda b, i: (0, 0), **const),
                    vec, vec, vec],
        out_specs=row(d),
        out_shape=jax.ShapeDtypeStruct((bsz, s, d), F32),
        compiler_params=pltpu.CompilerParams(
            dimension_semantics=("arbitrary", "arbitrary"), vmem_limit_bytes=VMEM_LIMIT),
        name="out",
    )(x, ada3, o_attn, za, ga, mb, w_pa, w_out, b_out, ln_g, ln_b)


def kernel(x, c, w_ada, b_ada, w_in, b_in, conv_w, w_proj_attn, w_proj_conv, w_out, b_out, ln_g, ln_b):
    bsz, s, d = x.shape
    depth = w_in.shape[0]
    alpha = (2.0 * depth) ** 0.25
    for layer in range(depth):
        ada3 = _ada(c, w_ada[layer], b_ada[layer]).reshape(bsz, 3, d)
        qkv, za, ga, mb = _inproj(x, ada3, w_in[layer].astype(BF16), b_in[layer].reshape(1, D_IN),
                                  conv_w[layer], w_proj_conv[layer].astype(BF16))
        x = _out(x, ada3, _attention(qkv), za, ga, mb,
                 w_proj_attn[layer].astype(BF16), w_out[layer].astype(BF16),
                 b_out[layer].reshape(1, d), ln_g[layer].reshape(1, d), ln_b[layer].reshape(1, d), alpha)
    return x
```

```python
import functools

import numpy as np
import jax
import jax.numpy as jnp
from jax import lax
from jax.experimental import pallas as pl
from jax.experimental.pallas import tpu as pltpu

D_MODEL = 1024
HEAD_DIM = 64
HEADS_PER_GROUP = 4
DILATED_GROUPS = ((128, 1), (512, 4), (2048, 16))
N_GROUPS = len(DILATED_GROUPS)
N_ATTN_HEADS = N_GROUPS * HEADS_PER_GROUP
ATTN_WIDTH = N_ATTN_HEADS * HEAD_DIM
GROUP_WIDTH = HEADS_PER_GROUP * HEAD_DIM
QKV_WIDTH = 3 * ATTN_WIDTH
CONV_WIDTH = D_MODEL
CONV_K = 3
SUB_BLOCK = 128
ALIBI_MAX_EXP = 8.0
LN_EPS = 1e-5

OFF_Z_ATTN = QKV_WIDTH
OFF_UX = OFF_Z_ATTN + GROUP_WIDTH
OFF_GB = OFF_UX + CONV_WIDTH
OFF_GC = OFF_GB + CONV_WIDTH
OFF_ZC = OFF_GC + CONV_WIDTH
OFF_GA = OFF_ZC + CONV_WIDTH
OFF_GMB = OFF_GA + D_MODEL
D_IN = OFF_GMB + D_MODEL

ROW_TILE = 512
CONV_CHUNK = 256
HALO = 8
VMEM_LIMIT = 56 * 1024 * 1024

F32 = jnp.float32
BF16 = jnp.bfloat16


def _sigmoid(v):
    return 1.0 / (1.0 + jnp.exp(-v))


def _silu(v):
    return v / (1.0 + jnp.exp(-v))


def _ada_kernel(c_ref, w_ref, b_ref, o_ref):
    o_ref[...] = jnp.dot(_silu(c_ref[...]), w_ref[...], preferred_element_type=F32,
                         precision=lax.Precision.HIGHEST) + b_ref[...]


def _ada(c, w_ada, b_ada):
    bsz, d = c.shape
    n_out = w_ada.shape[1]
    return pl.pallas_call(
        _ada_kernel,
        grid=(n_out // d,),
        in_specs=[pl.BlockSpec((bsz, d), lambda j: (0, 0)),
                  pl.BlockSpec((d, d), lambda j: (0, j)),
                  pl.BlockSpec((1, d), lambda j: (0, j))],
        out_specs=pl.BlockSpec((bsz, d), lambda j: (0, j)),
        out_shape=jax.ShapeDtypeStruct((bsz, n_out), F32),
        name="ada",
    )(c, w_ada, b_ada.reshape(1, n_out))


def _inproj_kernel(x_ref, ada_ref, w_ref, b_ref, cw_ref, wpc_ref,
                   qkv_ref, za_ref, ga_ref, mb_ref, u_scr, t_scr):
    tm = x_ref.shape[0]
    first_tile = pl.program_id(1) == 0
    shift = ada_ref[0:1, :]
    scale = ada_ref[1:2, :]
    h = (x_ref[...] * (1.0 + scale) + shift).astype(BF16)

    def proj(lo, width):
        return (jnp.dot(h, w_ref[:, lo:lo + width], preferred_element_type=F32)
                + b_ref[:, lo:lo + width])

    qkv_ref[...] = proj(0, QKV_WIDTH).astype(BF16)
    za_ref[...] = _silu(proj(OFF_Z_ATTN, GROUP_WIDTH)).astype(BF16)
    ga_ref[...] = _sigmoid(proj(OFF_GA, D_MODEL)).astype(BF16)

    @pl.when(first_tile)
    def _():
        u_scr[0:HALO, :] = jnp.zeros((HALO, CONV_WIDTH), F32)

    @pl.when(jnp.logical_not(first_tile))
    def _():
        u_scr[0:HALO, :] = u_scr[tm:tm + HALO, :]

    for c0 in range(0, CONV_WIDTH, CONV_CHUNK):
        cs = slice(c0, c0 + CONV_CHUNK)
        u = proj(OFF_GC + c0, CONV_CHUNK) * proj(OFF_UX + c0, CONV_CHUNK)
        u_scr[HALO:HALO + tm, cs] = u
        conv = cw_ref[0:1, cs] * u_scr[HALO - 2:HALO - 2 + tm, cs]
        conv = conv + cw_ref[1:2, cs] * u_scr[HALO - 1:HALO - 1 + tm, cs]
        conv = conv + cw_ref[2:3, cs] * u
        t = proj(OFF_GB + c0, CONV_CHUNK) * conv * _silu(proj(OFF_ZC + c0, CONV_CHUNK))
        t_scr[:, cs] = t.astype(BF16)
    y_conv = jnp.dot(t_scr[...], wpc_ref[...], preferred_element_type=F32)
    mb_ref[...] = (_sigmoid(proj(OFF_GMB, D_MODEL)) * y_conv).astype(BF16)


def _inproj(x, ada3, w_in, b_in, conv_w, w_pc):
    bsz, s, d = x.shape
    tm = ROW_TILE
    const = dict(pipeline_mode=pl.Buffered(1))
    row = lambda width: pl.BlockSpec((None, tm, width), lambda b, i: (b, i, 0))
    return pl.pallas_call(
        _inproj_kernel,
        grid=(bsz, s // tm),
        in_specs=[row(d),
                  pl.BlockSpec((None, 3, d), lambda b, i: (b, 0, 0)),
                  pl.BlockSpec((d, D_IN), lambda b, i: (0, 0), **const),
                  pl.BlockSpec((1, D_IN), lambda b, i: (0, 0), **const),
                  pl.BlockSpec((CONV_K, CONV_WIDTH), lambda b, i: (0, 0), **const),
                  pl.BlockSpec((CONV_WIDTH, d), lambda b, i: (0, 0), **const)],
        out_specs=[row(QKV_WIDTH), row(GROUP_WIDTH), row(d), row(d)],
        out_shape=[jax.ShapeDtypeStruct((bsz, s, QKV_WIDTH), BF16),
                   jax.ShapeDtypeStruct((bsz, s, GROUP_WIDTH), BF16),
                   jax.ShapeDtypeStruct((bsz, s, d), BF16),
                   jax.ShapeDtypeStruct((bsz, s, d), BF16)],
        scratch_shapes=[pltpu.VMEM((tm + HALO, CONV_WIDTH), F32),
                        pltpu.VMEM((tm, CONV_WIDTH), BF16)],
        compiler_params=pltpu.CompilerParams(
            dimension_semantics=("arbitrary", "arbitrary"), vmem_limit_bytes=VMEM_LIMIT),
        name="inproj",
    )(x, ada3, w_in, b_in, conv_w, w_pc)


def _attn_bias_table():
    L = SUB_BLOCK
    tabs = []
    for group, (window, dilation) in enumerate(DILATED_GROUPS):
        span = window // dilation
        heads = np.arange(group * HEADS_PER_GROUP, (group + 1) * HEADS_PER_GROUP, dtype=np.float64)
        slopes = 2.0 ** (-ALIBI_MAX_EXP * (heads + 1.0) / N_ATTN_HEADS)
        delta = (np.arange(L)[:, None] + L - np.arange(2 * L)[None, :]).astype(np.float64)
        valid = (delta >= 0) & (delta <= span)
        bias = np.where(valid[None], -slopes[:, None, None] * (delta * dilation)[None], -np.inf)
        tabs.append(bias.reshape(HEADS_PER_GROUP // 2, 2 * L, 2 * L))
    return np.stack(tabs).astype(np.float32)


def _attn_kernel(qkv_ref, bias_ref, o_ref, nat_scr, sub_scr, res):
    S = qkv_ref.shape[0]
    L = SUB_BLOCK
    pair_w = 2 * HEAD_DIM
    n_pairs = HEADS_PER_GROUP // 2
    low_half = lax.broadcasted_iota(jnp.int32, (L, pair_w), 1) < HEAD_DIM

    for g in range(1, N_GROUPS):
        d = DILATED_GROUPS[g][1]
        n = S // d
        for which in range(3):
            c0 = which * ATTN_WIDTH + g * GROUP_WIDTH
            for pair in range(n_pairs):
                cs = slice(pair * pair_w, (pair + 1) * pair_w)
                nat_scr[...] = qkv_ref[:, c0 + pair * pair_w:c0 + (pair + 1) * pair_w].astype(F32)
                for r in range(d):
                    sub_scr[g - 1, which, r * n:(r + 1) * n, cs] = (
                        nat_scr[pl.ds(r, n, stride=d), :].astype(BF16))

    def load(g, which, row0, n_rows, pair):
        cs = pair * pair_w
        if g == 0:
            c0 = which * ATTN_WIDTH + cs
            return qkv_ref[pl.ds(row0, n_rows), c0:c0 + pair_w]
        return sub_scr[g - 1, which, pl.ds(row0, n_rows), cs:cs + pair_w]

    def attend(g, specs):
        d = DILATED_GROUPS[g][1]
        work = [spec + (pair,) for spec in specs for pair in range(n_pairs)]
        scores = []
        for q0, k0, n_keys, _, pair in work:
            q = load(g, 0, q0, L, pair) * 0.125
            zero = jnp.zeros_like(q)
            q2 = jnp.concatenate([jnp.where(low_half, q, zero), jnp.where(low_half, zero, q)], axis=0)
            k = load(g, 1, k0, n_keys, pair)
            sc = lax.dot_general(q2, k, (((1,), (1,)), ((), ())), preferred_element_type=F32)
            scores.append(sc + bias_ref[g, pair, :, 2 * L - n_keys:])
        probs = []
        for sc in scores:
            m = jnp.max(sc, axis=-1, keepdims=True)
            probs.append((jnp.exp(sc - m).astype(BF16), m))
        for (_, k0, n_keys, out0, pair), (p, m) in zip(work, probs):
            v1 = jnp.concatenate([load(g, 2, k0, n_keys, pair), jnp.ones((n_keys, pair_w), BF16)], axis=1)
            pv = jnp.dot(p, v1, preferred_element_type=F32)
            mb = jnp.broadcast_to(m, (2 * L, pair_w))
            rows = pl.ds(out0, L) if d == 1 else pl.ds(out0, L, stride=d)
            res[0, g, pair, rows, :] = jnp.where(low_half, pv[:L, :pair_w], pv[L:, :pair_w])
            res[1, g, pair, rows, :] = jnp.where(low_half, pv[:L, pair_w:], pv[L:, pair_w:])
            res[2, g, pair, rows, :] = jnp.where(low_half, mb[:L], mb[L:])

    for g, (_, d) in enumerate(DILATED_GROUPS):
        n = S // d
        n_blocks = n // L
        if n_blocks == 1:
            per_step = 4

            def sub_body(i, carry, g=g):
                specs = []
                for u in range(per_step):
                    r = i * per_step + u
                    row0 = pl.multiple_of(r * L, L)
                    specs.append((row0, row0, L, r))
                attend(g, specs)
                return carry
            lax.fori_loop(0, d // per_step, sub_body, 0)
        elif d == 1:
            per_step = 3
            attend(g, [(0, 0, L, 0)])

            def blk_body(i, carry, g=g):
                specs = []
                for u in range(per_step):
                    q0 = pl.multiple_of((1 + i * per_step + u) * L, L)
                    specs.append((q0, pl.multiple_of(q0 - L, L), 2 * L, q0))
                attend(g, specs)
                return carry
            lax.fori_loop(0, (n_blocks - 1) // per_step, blk_body, 0)
        else:
            def res_body(r, carry, g=g, d=d, n=n, n_blocks=n_blocks):
                base = pl.multiple_of(r * n, L)
                specs = [(base, base, L, r)]
                for jb in range(1, n_blocks):
                    q0 = pl.multiple_of(base + jb * L, L)
                    specs.append((q0, pl.multiple_of(q0 - L, L), 2 * L, jb * (L * d) + r))
                attend(g, specs)
                return carry
            lax.fori_loop(0, d, res_body, 0)

    chunk = 2 * L

    def combine(i, carry):
        rows = pl.ds(pl.multiple_of(i * chunk, chunk), chunk)
        for pair in range(n_pairs):
            ms = [res[2, g, pair, rows, :] for g in range(N_GROUPS)]
            mmax = functools.reduce(jnp.maximum, ms)
            es = [jnp.exp(m - mmax) for m in ms]
            num = sum(e * res[0, g, pair, rows, :] for g, e in enumerate(es))
            den = sum(e * res[1, g, pair, rows, :] for g, e in enumerate(es))
            o_ref[rows, pair * pair_w:(pair + 1) * pair_w] = num / den
        return carry
    lax.fori_loop(0, S // chunk, combine, 0)


def _attention(qkv):
    bsz, s, _ = qkv.shape
    bias = jnp.asarray(_attn_bias_table())
    return pl.pallas_call(
        _attn_kernel,
        grid=(bsz,),
        in_specs=[pl.BlockSpec((None, s, QKV_WIDTH), lambda b: (b, 0, 0)),
                  pl.BlockSpec(bias.shape, lambda b: (0, 0, 0, 0), pipeline_mode=pl.Buffered(1))],
        out_specs=pl.BlockSpec((None, s, GROUP_WIDTH), lambda b: (b, 0, 0)),
        out_shape=jax.ShapeDtypeStruct((bsz, s, GROUP_WIDTH), F32),
        scratch_shapes=[pltpu.VMEM((s, 2 * HEAD_DIM), F32),
                        pltpu.VMEM((N_GROUPS - 1, 3, s, GROUP_WIDTH), BF16),
                        pltpu.VMEM((3, N_GROUPS, HEADS_PER_GROUP // 2, s, 2 * HEAD_DIM), F32)],
        compiler_params=pltpu.CompilerParams(
            dimension_semantics=("arbitrary",), vmem_limit_bytes=VMEM_LIMIT),
        name="attn",
    )(qkv, bias)


def _out_kernel(x_ref, ada_ref, oa_ref, za_ref, ga_ref, mb_ref, wpa_ref, wout_ref, bout_ref,
                lng_ref, lnb_ref, out_ref, *, alpha):
    ta = (oa_ref[...] * za_ref[...].astype(F32)).astype(BF16)
    y_attn = jnp.dot(ta, wpa_ref[...], preferred_element_type=F32)
    merged = (ga_ref[...].astype(F32) * y_attn + mb_ref[...].astype(F32)).astype(BF16)
    gate = ada_ref[2:3, :]
    sub = gate * (jnp.dot(merged, wout_ref[...], preferred_element_type=F32) + bout_ref[...])
    r = alpha * x_ref[...] + sub
    mu = jnp.mean(r, axis=-1, keepdims=True)
    cen = r - mu
    var = jnp.mean(cen * cen, axis=-1, keepdims=True)
    out_ref[...] = cen * lax.rsqrt(var + LN_EPS) * lng_ref[...] + lnb_ref[...]


def _out(x, ada3, o_attn, za, ga, mb, w_pa, w_out, b_out, ln_g, ln_b, alpha):
    bsz, s, d = x.shape
    tm = ROW_TILE
    const = dict(pipeline_mode=pl.Buffered(1))
    row = lambda width: pl.BlockSpec((None, tm, width), lambda b, i: (b, i, 0))
    vec = pl.BlockSpec((1, d), lambda b, i: (0, 0), **const)
    return pl.pallas_call(
        functools.partial(_out_kernel, alpha=alpha),
        grid=(bsz, s // tm),
        in_specs=[row(d), pl.BlockSpec((None, 3, d), lambda b, i: (b, 0, 0))]
                 + [row(GROUP_WIDTH), row(GROUP_WIDTH), row(d), row(d),
                    pl.BlockSpec((GROUP_WIDTH, d), lambda b, i: (0, 0), **const),
                    pl.BlockSpec((d, d), lambda b, i: (0, 0), **const),
                    vec, vec, vec],
        out_specs=row(d),
        out_shape=jax.ShapeDtypeStruct((bsz, s, d), F32),
        compiler_params=pltpu.CompilerParams(
            dimension_semantics=("arbitrary", "arbitrary"), vmem_limit_bytes=VMEM_LIMIT),
        name="out",
    )(x, ada3, o_attn, za, ga, mb, w_pa, w_out, b_out, ln_g, ln_b)


def kernel(x, c, w_ada, b_ada, w_in, b_in, conv_w, w_proj_attn, w_proj_conv, w_out, b_out, ln_g, ln_b):
    bsz, s, d = x.shape
    depth = w_in.shape[0]
    alpha = (2.0 * depth) ** 0.25
    for layer in range(depth):
        ada3 = _ada(c, w_ada[layer], b_ada[layer]).reshape(bsz, 3, d)
        qkv, za, ga, mb = _inproj(x, ada3, w_in[layer].astype(BF16), b_in[layer].reshape(1, D_IN),
                                  conv_w[layer], w_proj_conv[layer].astype(BF16))
        x = _out(x, ada3, _attention(qkv), za, ga, mb,
                 w_proj_attn[layer].astype(BF16), w_out[layer].astype(BF16),
                 b_out[layer].reshape(1, d), ln_g[layer].reshape(1, d), ln_b[layer].reshape(1, d), alpha)
    return x
```

```python
import functools

import numpy as np
import jax
import jax.numpy as jnp
from jax import lax
from jax.experimental import pallas as pl
from jax.experimental.pallas import tpu as pltpu

D_MODEL = 1024
HEAD_DIM = 64
HEADS_PER_GROUP = 4
DILATED_GROUPS = ((128, 1), (512, 4), (2048, 16))
N_GROUPS = len(DILATED_GROUPS)
N_ATTN_HEADS = N_GROUPS * HEADS_PER_GROUP
ATTN_WIDTH = N_ATTN_HEADS * HEAD_DIM
GROUP_WIDTH = HEADS_PER_GROUP * HEAD_DIM
QKV_WIDTH = 3 * ATTN_WIDTH
CONV_WIDTH = D_MODEL
CONV_K = 3
SUB_BLOCK = 128
ALIBI_MAX_EXP = 8.0
LN_EPS = 1e-5

OFF_Z_ATTN = QKV_WIDTH
OFF_UX = OFF_Z_ATTN + GROUP_WIDTH
OFF_GB = OFF_UX + CONV_WIDTH
OFF_GC = OFF_GB + CONV_WIDTH
OFF_ZC = OFF_GC + CONV_WIDTH
OFF_GA = OFF_ZC + CONV_WIDTH
OFF_GMB = OFF_GA + D_MODEL
D_IN = OFF_GMB + D_MODEL

ROW_TILE = 512
OUT_ROW_TILE = 1024
OUT_CHUNK = 256
CONV_CHUNK = 256
HALO = 8
VMEM_LIMIT = 56 * 1024 * 1024

F32 = jnp.float32
BF16 = jnp.bfloat16


def _sigmoid(v):
    return 1.0 / (1.0 + jnp.exp(-v))


def _silu(v):
    return v / (1.0 + jnp.exp(-v))


def _ada_kernel(c_ref, w_ref, b_ref, o_ref):
    o_ref[...] = jnp.dot(_silu(c_ref[...]), w_ref[...], preferred_element_type=F32,
                         precision=lax.Precision.HIGHEST) + b_ref[...]


def _ada(c, w_ada, b_ada):
    bsz, d = c.shape
    n_out = w_ada.shape[1]
    return pl.pallas_call(
        _ada_kernel,
        grid=(n_out // d,),
        in_specs=[pl.BlockSpec((bsz, d), lambda j: (0, 0)),
                  pl.BlockSpec((d, d), lambda j: (0, j)),
                  pl.BlockSpec((1, d), lambda j: (0, j))],
        out_specs=pl.BlockSpec((bsz, d), lambda j: (0, j)),
        out_shape=jax.ShapeDtypeStruct((bsz, n_out), F32),
        name="ada",
    )(c, w_ada, b_ada.reshape(1, n_out))


def _inproj_kernel(x_ref, ada_ref, w_ref, b_ref, cw_ref, wpc_ref,
                   qkv_ref, za_ref, ga_ref, mb_ref, u_scr, t_scr):
    tm = x_ref.shape[0]
    first_tile = pl.program_id(1) == 0
    @pl.when(first_tile)
    def _():
        u_scr[0:HALO, :] = jnp.zeros((HALO, CONV_WIDTH), F32)

    @pl.when(jnp.logical_not(first_tile))
    def _():
        u_scr[0:HALO, :] = u_scr[tm:tm + HALO, :]

    shift = ada_ref[0:1, :]
    scale = ada_ref[1:2, :]
    h = (x_ref[...] * (1.0 + scale) + shift).astype(BF16)

    def proj(lo, width):
        return (jnp.dot(h, w_ref[:, lo:lo + width], preferred_element_type=F32)
                + b_ref[:, lo:lo + width])

    for c0 in range(0, CONV_WIDTH, CONV_CHUNK):
        cs = slice(c0, c0 + CONV_CHUNK)
        u = proj(OFF_GC + c0, CONV_CHUNK) * proj(OFF_UX + c0, CONV_CHUNK)
        u_scr[HALO:HALO + tm, cs] = u
        conv = cw_ref[0:1, cs] * u_scr[HALO - 2:HALO - 2 + tm, cs]
        conv = conv + cw_ref[1:2, cs] * u_scr[HALO - 1:HALO - 1 + tm, cs]
        conv = conv + cw_ref[2:3, cs] * u
        t = proj(OFF_GB + c0, CONV_CHUNK) * conv * _silu(proj(OFF_ZC + c0, CONV_CHUNK))
        t_scr[:, cs] = t.astype(BF16)
    za_ref[...] = _silu(proj(OFF_Z_ATTN, GROUP_WIDTH)).astype(BF16)
    ga_ref[...] = _sigmoid(proj(OFF_GA, D_MODEL)).astype(BF16)
    y_conv = jnp.dot(t_scr[...], wpc_ref[...], preferred_element_type=F32)
    mb_ref[...] = (_sigmoid(proj(OFF_GMB, D_MODEL)) * y_conv).astype(BF16)
    qkv_ref[...] = proj(0, QKV_WIDTH).astype(BF16)


def _inproj(x, ada3, w_in, b_in, conv_w, w_pc):
    bsz, s, d = x.shape
    tm = ROW_TILE
    const = dict(pipeline_mode=pl.Buffered(1))
    row = lambda width: pl.BlockSpec((None, tm, width), lambda b, i: (b, i, 0))
    return pl.pallas_call(
        _inproj_kernel,
        grid=(bsz, s // tm),
        in_specs=[row(d),
                  pl.BlockSpec((None, 3, d), lambda b, i: (b, 0, 0)),
                  pl.BlockSpec((d, D_IN), lambda b, i: (0, 0), **const),
                  pl.BlockSpec((1, D_IN), lambda b, i: (0, 0), **const),
                  pl.BlockSpec((CONV_K, CONV_WIDTH), lambda b, i: (0, 0), **const),
                  pl.BlockSpec((CONV_WIDTH, d), lambda b, i: (0, 0), **const)],
        out_specs=[row(QKV_WIDTH), row(GROUP_WIDTH), row(d), row(d)],
        out_shape=[jax.ShapeDtypeStruct((bsz, s, QKV_WIDTH), BF16),
                   jax.ShapeDtypeStruct((bsz, s, GROUP_WIDTH), BF16),
                   jax.ShapeDtypeStruct((bsz, s, d), BF16),
                   jax.ShapeDtypeStruct((bsz, s, d), BF16)],
        scratch_shapes=[pltpu.VMEM((tm + HALO, CONV_WIDTH), F32),
                        pltpu.VMEM((tm, CONV_WIDTH), BF16)],
        compiler_params=pltpu.CompilerParams(
            dimension_semantics=("arbitrary", "arbitrary"), vmem_limit_bytes=VMEM_LIMIT),
        name="inproj",
    )(x, ada3, w_in, b_in, conv_w, w_pc)


def _attn_bias_table():
    L = SUB_BLOCK
    tabs = []
    for group, (window, dilation) in enumerate(DILATED_GROUPS):
        span = window // dilation
        heads = np.arange(group * HEADS_PER_GROUP, (group + 1) * HEADS_PER_GROUP, dtype=np.float64)
        slopes = 2.0 ** (-ALIBI_MAX_EXP * (heads + 1.0) / N_ATTN_HEADS)
        delta = (np.arange(L)[:, None] + L - np.arange(2 * L)[None, :]).astype(np.float64)
        valid = (delta >= 0) & (delta <= span)
        bias = np.where(valid[None], -slopes[:, None, None] * (delta * dilation)[None], -np.inf)
        tabs.append(bias.reshape(HEADS_PER_GROUP // 2, 2 * L, 2 * L))
    return np.stack(tabs).astype(np.float32)


def _attn_kernel(qkv_ref, bias_ref, o_ref, nat_scr, sub_scr, res):
    S = qkv_ref.shape[0]
    L = SUB_BLOCK
    pair_w = 2 * HEAD_DIM
    n_pairs = HEADS_PER_GROUP // 2
    low_half = lax.broadcasted_iota(jnp.int32, (L, pair_w), 1) < HEAD_DIM

    for g in range(1, N_GROUPS):
        d = DILATED_GROUPS[g][1]
        n = S // d
        for which in range(3):
            c0 = which * ATTN_WIDTH + g * GROUP_WIDTH
            for pair in range(n_pairs):
                cs = slice(pair * pair_w, (pair + 1) * pair_w)
                nat_scr[...] = qkv_ref[:, c0 + pair * pair_w:c0 + (pair + 1) * pair_w].astype(F32)
                for r in range(d):
                    sub_scr[g - 1, which, r * n:(r + 1) * n, cs] = (
                        nat_scr[pl.ds(r, n, stride=d), :].astype(BF16))

    def load(g, which, row0, n_rows, pair):
        cs = pair * pair_w
        if g == 0:
            c0 = which * ATTN_WIDTH + cs
            return qkv_ref[pl.ds(row0, n_rows), c0:c0 + pair_w]
        return sub_scr[g - 1, which, pl.ds(row0, n_rows), cs:cs + pair_w]

    def attend(g, specs):
        d = DILATED_GROUPS[g][1]
        work = [spec + (pair,) for spec in specs for pair in range(n_pairs)]
        scores = []
        for q0, k0, n_keys, _, pair in work:
            q = load(g, 0, q0, L, pair) * 0.125
            zero = jnp.zeros_like(q)
            q2 = jnp.concatenate([jnp.where(low_half, q, zero), jnp.where(low_half, zero, q)], axis=0)
            k = load(g, 1, k0, n_keys, pair)
            sc = lax.dot_general(q2, k, (((1,), (1,)), ((), ())), preferred_element_type=F32)
            scores.append(sc + bias_ref[g, pair, :, 2 * L - n_keys:])
        probs = []
        for sc in scores:
            m = jnp.max(sc, axis=-1, keepdims=True)
            probs.append((jnp.exp(sc - m).astype(BF16), m))
        for (_, k0, n_keys, out0, pair), (p, m) in zip(work, probs):
            v1 = jnp.concatenate([load(g, 2, k0, n_keys, pair), jnp.ones((n_keys, pair_w), BF16)], axis=1)
            pv = jnp.dot(p, v1, preferred_element_type=F32)
            mb = jnp.broadcast_to(m, (2 * L, pair_w))
            rows = pl.ds(out0, L) if d == 1 else pl.ds(out0, L, stride=d)
            res[0, g, pair, rows, :] = jnp.where(low_half, pv[:L, :pair_w], pv[L:, :pair_w])
            res[1, g, pair, rows, :] = jnp.where(low_half, pv[:L, pair_w:], pv[L:, pair_w:])
            res[2, g, pair, rows, :] = jnp.where(low_half, mb[:L], mb[L:])

    for g, (_, d) in enumerate(DILATED_GROUPS):
        n = S // d
        n_blocks = n // L
        if n_blocks == 1:
            per_step = 4

            def sub_body(i, carry, g=g):
                specs = []
                for u in range(per_step):
                    r = i * per_step + u
                    row0 = pl.multiple_of(r * L, L)
                    specs.append((row0, row0, L, r))
                attend(g, specs)
                return carry
            lax.fori_loop(0, d // per_step, sub_body, 0)
        elif d == 1:
            per_step = 3
            attend(g, [(0, 0, L, 0)])

            def blk_body(i, carry, g=g):
                specs = []
                for u in range(per_step):
                    q0 = pl.multiple_of((1 + i * per_step + u) * L, L)
                    specs.append((q0, pl.multiple_of(q0 - L, L), 2 * L, q0))
                attend(g, specs)
                return carry
            lax.fori_loop(0, (n_blocks - 1) // per_step, blk_body, 0)
        else:
            def res_body(r, carry, g=g, d=d, n=n, n_blocks=n_blocks):
                base = pl.multiple_of(r * n, L)
                specs = [(base, base, L, r)]
                for jb in range(1, n_blocks):
                    q0 = pl.multiple_of(base + jb * L, L)
                    specs.append((q0, pl.multiple_of(q0 - L, L), 2 * L, jb * (L * d) + r))
                attend(g, specs)
                return carry
            lax.fori_loop(0, d, res_body, 0)

    chunk = 2 * L

    def combine(i, carry):
        rows = pl.ds(pl.multiple_of(i * chunk, chunk), chunk)
        for pair in range(n_pairs):
            ms = [res[2, g, pair, rows, :] for g in range(N_GROUPS)]
            mmax = functools.reduce(jnp.maximum, ms)
            es = [jnp.exp(m - mmax) for m in ms]
            num = sum(e * res[0, g, pair, rows, :] for g, e in enumerate(es))
            den = sum(e * res[1, g, pair, rows, :] for g, e in enumerate(es))
            o_ref[rows, pair * pair_w:(pair + 1) * pair_w] = num / den
        return carry
    lax.fori_loop(0, S // chunk, combine, 0)


def _attention(qkv):
    bsz, s, _ = qkv.shape
    bias = jnp.asarray(_attn_bias_table())
    return pl.pallas_call(
        _attn_kernel,
        grid=(bsz,),
        in_specs=[pl.BlockSpec((None, s, QKV_WIDTH), lambda b: (b, 0, 0)),
                  pl.BlockSpec(bias.shape, lambda b: (0, 0, 0, 0), pipeline_mode=pl.Buffered(1))],
        out_specs=pl.BlockSpec((None, s, GROUP_WIDTH), lambda b: (b, 0, 0)),
        out_shape=jax.ShapeDtypeStruct((bsz, s, GROUP_WIDTH), F32),
        scratch_shapes=[pltpu.VMEM((s, 2 * HEAD_DIM), F32),
                        pltpu.VMEM((N_GROUPS - 1, 3, s, GROUP_WIDTH), BF16),
                        pltpu.VMEM((3, N_GROUPS, HEADS_PER_GROUP // 2, s, 2 * HEAD_DIM), F32)],
        compiler_params=pltpu.CompilerParams(
            dimension_semantics=("arbitrary",), vmem_limit_bytes=VMEM_LIMIT),
        name="attn",
    )(qkv, bias)


def _out_kernel(x_ref, ada_ref, oa_ref, za_ref, ga_ref, mb_ref, wpa_ref, wout_ref, bout_ref,
                lng_ref, lnb_ref, out_ref, *, alpha):
    gate = ada_ref[2:3, :]
    for r0 in range(0, x_ref.shape[0], OUT_CHUNK):
        rows = slice(r0, r0 + OUT_CHUNK)
        ta = (oa_ref[rows, :] * za_ref[rows, :].astype(F32)).astype(BF16)
        y_attn = jnp.dot(ta, wpa_ref[...], preferred_element_type=F32)
        merged = ga_ref[rows, :] * y_attn.astype(BF16) + mb_ref[rows, :]
        sub = gate * (jnp.dot(merged, wout_ref[...], preferred_element_type=F32) + bout_ref[...])
        r = alpha * x_ref[rows, :] + sub
        mu = jnp.mean(r, axis=-1, keepdims=True)
        cen = r - mu
        var = jnp.mean(cen * cen, axis=-1, keepdims=True)
        out_ref[rows, :] = cen * lax.rsqrt(var + LN_EPS) * lng_ref[...] + lnb_ref[...]


def _out(x, ada3, o_attn, za, ga, mb, w_pa, w_out, b_out, ln_g, ln_b, alpha):
    bsz, s, d = x.shape
    tm = OUT_ROW_TILE
    const = dict(pipeline_mode=pl.Buffered(1))
    row = lambda width: pl.BlockSpec((None, tm, width), lambda b, i: (b, i, 0))
    vec = pl.BlockSpec((1, d), lambda b, i: (0, 0), **const)
    return pl.pallas_call(
        functools.partial(_out_kernel, alpha=alpha),
        grid=(bsz, s // tm),
        in_specs=[row(d), pl.BlockSpec((None, 3, d), lambda b, i: (b, 0, 0))]
                 + [row(GROUP_WIDTH), row(GROUP_WIDTH), row(d), row(d),
                    pl.BlockSpec((GROUP_WIDTH, d), lambda b, i: (0, 0), **const),
                    pl.BlockSpec((d, d), lambda b, i: (0, 0), **const),
                    vec, vec, vec],
        out_specs=row(d),
        out_shape=jax.ShapeDtypeStruct((bsz, s, d), F32),
        compiler_params=pltpu.CompilerParams(
            dimension_semantics=("arbitrary", "arbitrary"), vmem_limit_bytes=VMEM_LIMIT),
        name="out",
    )(x, ada3, o_attn, za, ga, mb, w_pa, w_out, b_out, ln_g, ln_b)


def kernel(x, c, w_ada, b_ada, w_in, b_in, conv_w, w_proj_attn, w_proj_conv, w_out, b_out, ln_g, ln_b):
    bsz, s, d = x.shape
    depth = w_in.shape[0]
    alpha = (2.0 * depth) ** 0.25
    for layer in range(depth):
        ada3 = _ada(c, w_ada[layer], b_ada[layer]).reshape(bsz, 3, d)
        qkv, za, ga, mb = _inproj(x, ada3, w_in[layer].astype(BF16), b_in[layer].reshape(1, D_IN),
                                  conv_w[layer], w_proj_conv[layer].astype(BF16))
        x = _out(x, ada3, _attention(qkv), za, ga, mb,
                 w_proj_attn[layer].astype(BF16), w_out[layer].astype(BF16),
                 b_out[layer].reshape(1, d), ln_g[layer].reshape(1, d), ln_b[layer].reshape(1, d), alpha)
    return x
```

```python
import functools

import numpy as np
import jax
import jax.numpy as jnp
from jax import lax
from jax.experimental import pallas as pl
from jax.experimental.pallas import tpu as pltpu

D_MODEL = 1024
HEAD_DIM = 64
HEADS_PER_GROUP = 4
DILATED_GROUPS = ((128, 1), (512, 4), (2048, 16))
N_GROUPS = len(DILATED_GROUPS)
N_ATTN_HEADS = N_GROUPS * HEADS_PER_GROUP
ATTN_WIDTH = N_ATTN_HEADS * HEAD_DIM
GROUP_WIDTH = HEADS_PER_GROUP * HEAD_DIM
QKV_WIDTH = 3 * ATTN_WIDTH
CONV_WIDTH = D_MODEL
CONV_K = 3
SUB_BLOCK = 128
ALIBI_MAX_EXP = 8.0
LN_EPS = 1e-5

OFF_Z_ATTN = QKV_WIDTH
OFF_UX = OFF_Z_ATTN + GROUP_WIDTH
OFF_GB = OFF_UX + CONV_WIDTH
OFF_GC = OFF_GB + CONV_WIDTH
OFF_ZC = OFF_GC + CONV_WIDTH
OFF_GA = OFF_ZC + CONV_WIDTH
D_IN = OFF_GA + 2 * D_MODEL

ROW_TILE = 512
OUT_ROW_TILE = 1024
OUT_CHUNK = 256
CONV_CHUNK = 256
HALO = 8
VMEM_LIMIT = 56 * 1024 * 1024

F32 = jnp.float32
BF16 = jnp.bfloat16


def _sigmoid(v):
    return 0.5 * jnp.tanh(0.5 * v) + 0.5


def _silu(v):
    return v * _sigmoid(v)


def _ada_kernel(c_ref, w_ref, b_ref, o_ref):
    o_ref[...] = jnp.dot(_silu(c_ref[...]), w_ref[...], preferred_element_type=F32,
                         precision=lax.Precision.HIGHEST) + b_ref[...]


def _ada(c, w_ada, b_ada):
    bsz, d = c.shape
    n_out = w_ada.shape[1]
    return pl.pallas_call(
        _ada_kernel,
        grid=(n_out // d,),
        in_specs=[pl.BlockSpec((bsz, d), lambda j: (0, 0)),
                  pl.BlockSpec((d, d), lambda j: (0, j)),
                  pl.BlockSpec((1, d), lambda j: (0, j))],
        out_specs=pl.BlockSpec((bsz, d), lambda j: (0, j)),
        out_shape=jax.ShapeDtypeStruct((bsz, n_out), F32),
        name="ada",
    )(c, w_ada, b_ada.reshape(1, n_out))


def _inproj_kernel(x_ref, ada_ref, w_ref, b_ref, cw_ref, wpc_ref,
                   qkv_ref, za_ref, yc_ref, u_scr, t_scr):
    tm = x_ref.shape[0]
    first_tile = pl.program_id(1) == 0
    @pl.when(first_tile)
    def _():
        u_scr[0:HALO, :] = jnp.zeros((HALO, CONV_WIDTH), F32)

    @pl.when(jnp.logical_not(first_tile))
    def _():
        u_scr[0:HALO, :] = u_scr[tm:tm + HALO, :]

    shift = ada_ref[0:1, :]
    scale = ada_ref[1:2, :]
    h = (x_ref[...] * (1.0 + scale) + shift).astype(BF16)

    def proj(lo, width):
        return (jnp.dot(h, w_ref[:, lo:lo + width], preferred_element_type=F32)
                + b_ref[:, lo:lo + width])

    for c0 in range(0, CONV_WIDTH, CONV_CHUNK):
        cs = slice(c0, c0 + CONV_CHUNK)
        u = proj(OFF_GC + c0, CONV_CHUNK) * proj(OFF_UX + c0, CONV_CHUNK)
        u_scr[HALO:HALO + tm, cs] = u
        conv = cw_ref[0:1, cs] * u_scr[HALO - 2:HALO - 2 + tm, cs]
        conv = conv + cw_ref[1:2, cs] * u_scr[HALO - 1:HALO - 1 + tm, cs]
        conv = conv + cw_ref[2:3, cs] * u
        t = proj(OFF_GB + c0, CONV_CHUNK) * conv * _silu(proj(OFF_ZC + c0, CONV_CHUNK))
        t_scr[:, cs] = t.astype(BF16)
    za_ref[...] = _silu(proj(OFF_Z_ATTN, GROUP_WIDTH)).astype(BF16)
    yc_ref[...] = jnp.dot(t_scr[...], wpc_ref[...], preferred_element_type=F32).astype(BF16)
    qkv_ref[...] = proj(0, QKV_WIDTH).astype(BF16)


def _inproj(x, ada3, w_in, b_in, conv_w, w_pc):
    bsz, s, d = x.shape
    tm = ROW_TILE
    const = dict(pipeline_mode=pl.Buffered(1))
    row = lambda width: pl.BlockSpec((None, tm, width), lambda b, i: (b, i, 0))
    return pl.pallas_call(
        _inproj_kernel,
        grid=(bsz, s // tm),
        in_specs=[row(d),
                  pl.BlockSpec((None, 3, d), lambda b, i: (b, 0, 0)),
                  pl.BlockSpec((d, OFF_GA), lambda b, i: (0, 0), **const),
                  pl.BlockSpec((1, OFF_GA), lambda b, i: (0, 0), **const),
                  pl.BlockSpec((CONV_K, CONV_WIDTH), lambda b, i: (0, 0), **const),
                  pl.BlockSpec((CONV_WIDTH, d), lambda b, i: (0, 0), **const)],
        out_specs=[row(QKV_WIDTH), row(GROUP_WIDTH), row(d)],
        out_shape=[jax.ShapeDtypeStruct((bsz, s, QKV_WIDTH), BF16),
                   jax.ShapeDtypeStruct((bsz, s, GROUP_WIDTH), BF16),
                   jax.ShapeDtypeStruct((bsz, s, d), BF16)],
        scratch_shapes=[pltpu.VMEM((tm + HALO, CONV_WIDTH), F32),
                        pltpu.VMEM((tm, CONV_WIDTH), BF16)],
        compiler_params=pltpu.CompilerParams(
            dimension_semantics=("arbitrary", "arbitrary"), vmem_limit_bytes=VMEM_LIMIT),
        name="inproj",
    )(x, ada3, w_in, b_in, conv_w, w_pc)


def _attn_bias_table():
    L = SUB_BLOCK
    tabs = []
    for group, (window, dilation) in enumerate(DILATED_GROUPS):
        span = window // dilation
        heads = np.arange(group * HEADS_PER_GROUP, (group + 1) * HEADS_PER_GROUP, dtype=np.float64)
        slopes = 2.0 ** (-ALIBI_MAX_EXP * (heads + 1.0) / N_ATTN_HEADS)
        delta = (np.arange(L)[:, None] + L - np.arange(2 * L)[None, :]).astype(np.float64)
        valid = (delta >= 0) & (delta <= span)
        bias = np.where(valid[None], -slopes[:, None, None] * (delta * dilation)[None], -np.inf)
        tabs.append(bias.reshape(HEADS_PER_GROUP // 2, 2 * L, 2 * L))
    return np.stack(tabs).astype(np.float32)


def _attn_kernel(qkv_ref, bias_ref, o_ref, nat_scr, sub_scr, res):
    S = qkv_ref.shape[0]
    L = SUB_BLOCK
    pair_w = 2 * HEAD_DIM
    n_pairs = HEADS_PER_GROUP // 2
    low_half = lax.broadcasted_iota(jnp.int32, (L, pair_w), 1) < HEAD_DIM

    for g in range(1, N_GROUPS):
        d = DILATED_GROUPS[g][1]
        n = S // d
        for which in range(3):
            c0 = which * ATTN_WIDTH + g * GROUP_WIDTH
            for pair in range(n_pairs):
                cs = slice(pair * pair_w, (pair + 1) * pair_w)
                nat_scr[...] = qkv_ref[:, c0 + pair * pair_w:c0 + (pair + 1) * pair_w].astype(F32)
                for r in range(d):
                    sub_scr[g - 1, which, r * n:(r + 1) * n, cs] = (
                        nat_scr[pl.ds(r, n, stride=d), :].astype(BF16))

    def load(g, which, row0, n_rows, pair):
        cs = pair * pair_w
        if g == 0:
            c0 = which * ATTN_WIDTH + cs
            return qkv_ref[pl.ds(row0, n_rows), c0:c0 + pair_w]
        return sub_scr[g - 1, which, pl.ds(row0, n_rows), cs:cs + pair_w]

    def attend(g, specs):
        d = DILATED_GROUPS[g][1]
        work = [spec + (pair,) for spec in specs for pair in range(n_pairs)]
        scores = []
        for q0, k0, n_keys, _, pair in work:
            q = load(g, 0, q0, L, pair) * 0.125
            zero = jnp.zeros_like(q)
            q2 = jnp.concatenate([jnp.where(low_half, q, zero), jnp.where(low_half, zero, q)], axis=0)
            k = load(g, 1, k0, n_keys, pair)
            sc = lax.dot_general(q2, k, (((1,), (1,)), ((), ())), preferred_element_type=F32)
            scores.append(sc + bias_ref[g, pair, :, 2 * L - n_keys:])
        probs = []
        for sc in scores:
            m = jnp.max(sc, axis=-1, keepdims=True)
            probs.append((jnp.exp(sc - m).astype(BF16), m))
        for (_, k0, n_keys, out0, pair), (p, m) in zip(work, probs):
            v1 = jnp.concatenate([load(g, 2, k0, n_keys, pair), jnp.ones((n_keys, pair_w), BF16)], axis=1)
            pv = jnp.dot(p, v1, preferred_element_type=F32)
            mb = jnp.broadcast_to(m, (2 * L, pair_w))
            rows = pl.ds(out0, L) if d == 1 else pl.ds(out0, L, stride=d)
            res[0, g, pair, rows, :] = jnp.where(low_half, pv[:L, :pair_w], pv[L:, :pair_w])
            res[1, g, pair, rows, :] = jnp.where(low_half, pv[:L, pair_w:], pv[L:, pair_w:])
            res[2, g, pair, rows, :] = jnp.where(low_half, mb[:L], mb[L:])

    for g, (_, d) in enumerate(DILATED_GROUPS):
        n = S // d
        n_blocks = n // L
        if n_blocks == 1:
            per_step = 4

            def sub_body(i, carry, g=g):
                specs = []
                for u in range(per_step):
                    r = i * per_step + u
                    row0 = pl.multiple_of(r * L, L)
                    specs.append((row0, row0, L, r))
                attend(g, specs)
                return carry
            lax.fori_loop(0, d // per_step, sub_body, 0)
        elif d == 1:
            per_step = 3
            attend(g, [(0, 0, L, 0)])

            def blk_body(i, carry, g=g):
                specs = []
                for u in range(per_step):
                    q0 = pl.multiple_of((1 + i * per_step + u) * L, L)
                    specs.append((q0, pl.multiple_of(q0 - L, L), 2 * L, q0))
                attend(g, specs)
                return carry
            lax.fori_loop(0, (n_blocks - 1) // per_step, blk_body, 0)
        else:
            def res_body(r, carry, g=g, d=d, n=n, n_blocks=n_blocks):
                base = pl.multiple_of(r * n, L)
                specs = [(base, base, L, r)]
                for jb in range(1, n_blocks):
                    q0 = pl.multiple_of(base + jb * L, L)
                    specs.append((q0, pl.multiple_of(q0 - L, L), 2 * L, jb * (L * d) + r))
                attend(g, specs)
                return carry
            lax.fori_loop(0, d, res_body, 0)

    chunk = 2 * L

    def combine(i, carry):
        rows = pl.ds(pl.multiple_of(i * chunk, chunk), chunk)
        for pair in range(n_pairs):
            ms = [res[2, g, pair, rows, :] for g in range(N_GROUPS)]
            mmax = functools.reduce(jnp.maximum, ms)
            es = [jnp.exp(m - mmax) for m in ms]
            num = sum(e * res[0, g, pair, rows, :] for g, e in enumerate(es))
            den = sum(e * res[1, g, pair, rows, :] for g, e in enumerate(es))
            o_ref[rows, pair * pair_w:(pair + 1) * pair_w] = num / den
        return carry
    lax.fori_loop(0, S // chunk, combine, 0)


def _attention(qkv):
    bsz, s, _ = qkv.shape
    bias = jnp.asarray(_attn_bias_table())
    return pl.pallas_call(
        _attn_kernel,
        grid=(bsz,),
        in_specs=[pl.BlockSpec((None, s, QKV_WIDTH), lambda b: (b, 0, 0)),
                  pl.BlockSpec(bias.shape, lambda b: (0, 0, 0, 0), pipeline_mode=pl.Buffered(1))],
        out_specs=pl.BlockSpec((None, s, GROUP_WIDTH), lambda b: (b, 0, 0)),
        out_shape=jax.ShapeDtypeStruct((bsz, s, GROUP_WIDTH), F32),
        scratch_shapes=[pltpu.VMEM((s, 2 * HEAD_DIM), F32),
                        pltpu.VMEM((N_GROUPS - 1, 3, s, GROUP_WIDTH), BF16),
                        pltpu.VMEM((3, N_GROUPS, HEADS_PER_GROUP // 2, s, 2 * HEAD_DIM), F32)],
        compiler_params=pltpu.CompilerParams(
            dimension_semantics=("arbitrary",), vmem_limit_bytes=VMEM_LIMIT),
        name="attn",
    )(qkv, bias)


def _out_kernel(x_ref, ada_ref, oa_ref, za_ref, yc_ref, wg_ref, bg_ref, wpa_ref, wout_ref, bout_ref,
                lng_ref, lnb_ref, out_ref, *, alpha):
    shift = ada_ref[0:1, :]
    scale = ada_ref[1:2, :]
    gate = ada_ref[2:3, :]
    def mix_gates(r0):
        h = (x_ref[r0:r0 + OUT_CHUNK, :] * (1.0 + scale) + shift).astype(BF16)
        return _sigmoid(jnp.dot(h, wg_ref[...], preferred_element_type=F32) + bg_ref[...]).astype(BF16)

    starts = list(range(0, x_ref.shape[0], OUT_CHUNK))
    g_next = mix_gates(starts[0])
    for idx, r0 in enumerate(starts):
        rows = slice(r0, r0 + OUT_CHUNK)
        g_mix = g_next
        ta = (oa_ref[rows, :] * za_ref[rows, :].astype(F32)).astype(BF16)
        y_attn = jnp.dot(ta, wpa_ref[...], preferred_element_type=F32)
        if idx + 1 < len(starts):
            g_next = mix_gates(starts[idx + 1])
        merged = g_mix[:, :D_MODEL] * y_attn.astype(BF16) + g_mix[:, D_MODEL:] * yc_ref[rows, :]
        sub = gate * (jnp.dot(merged, wout_ref[...], preferred_element_type=F32) + bout_ref[...])
        r = alpha * x_ref[rows, :] + sub
        mu = jnp.mean(r, axis=-1, keepdims=True)
        cen = r - mu
        var = jnp.mean(cen * cen, axis=-1, keepdims=True)
        out_ref[rows, :] = cen * lax.rsqrt(var + LN_EPS) * lng_ref[...] + lnb_ref[...]


def _out(x, ada3, o_attn, za, yc, w_gate, b_gate, w_pa, w_out, b_out, ln_g, ln_b, alpha):
    bsz, s, d = x.shape
    tm = OUT_ROW_TILE
    const = dict(pipeline_mode=pl.Buffered(1))
    row = lambda width: pl.BlockSpec((None, tm, width), lambda b, i: (b, i, 0))
    vec = pl.BlockSpec((1, d), lambda b, i: (0, 0), **const)
    return pl.pallas_call(
        functools.partial(_out_kernel, alpha=alpha),
        grid=(bsz, s // tm),
        in_specs=[row(d), pl.BlockSpec((None, 3, d), lambda b, i: (b, 0, 0))]
                 + [row(GROUP_WIDTH), row(GROUP_WIDTH), row(d),
                    pl.BlockSpec((d, 2 * d), lambda b, i: (0, 0), **const),
                    pl.BlockSpec((1, 2 * d), lambda b, i: (0, 0), **const),
                    pl.BlockSpec((GROUP_WIDTH, d), lambda b, i: (0, 0), **const),
                    pl.BlockSpec((d, d), lambda b, i: (0, 0), **const),
                    vec, vec, vec],
        out_specs=row(d),
        out_shape=jax.ShapeDtypeStruct((bsz, s, d), F32),
        compiler_params=pltpu.CompilerParams(
            dimension_semantics=("arbitrary", "arbitrary"), vmem_limit_bytes=VMEM_LIMIT),
        name="out",
    )(x, ada3, o_attn, za, yc, w_gate, b_gate, w_pa, w_out, b_out, ln_g, ln_b)


def kernel(x, c, w_ada, b_ada, w_in, b_in, conv_w, w_proj_attn, w_proj_conv, w_out, b_out, ln_g, ln_b):
    bsz, s, d = x.shape
    depth = w_in.shape[0]
    alpha = (2.0 * depth) ** 0.25
    for layer in range(depth):
        ada3 = _ada(c, w_ada[layer], b_ada[layer]).reshape(bsz, 3, d)
        b_in2 = b_in[layer].reshape(1, D_IN)
        qkv, za, yc = _inproj(x, ada3, w_in[layer, :, :OFF_GA].astype(BF16), b_in2[:, :OFF_GA],
                              conv_w[layer], w_proj_conv[layer].astype(BF16))
        x = _out(x, ada3, _attention(qkv), za, yc,
                 w_in[layer, :, OFF_GA:].astype(BF16), b_in2[:, OFF_GA:],
                 w_proj_attn[layer].astype(BF16), w_out[layer].astype(BF16),
                 b_out[layer].reshape(1, d), ln_g[layer].reshape(1, d), ln_b[layer].reshape(1, d), alpha)
    return x
```

```python
import functools

import numpy as np
import jax
import jax.numpy as jnp
from jax import lax
from jax.experimental import pallas as pl
from jax.experimental.pallas import tpu as pltpu

D_MODEL = 1024
HEAD_DIM = 64
HEADS_PER_GROUP = 4
DILATED_GROUPS = ((128, 1), (512, 4), (2048, 16))
N_GROUPS = len(DILATED_GROUPS)
N_ATTN_HEADS = N_GROUPS * HEADS_PER_GROUP
ATTN_WIDTH = N_ATTN_HEADS * HEAD_DIM
GROUP_WIDTH = HEADS_PER_GROUP * HEAD_DIM
QKV_WIDTH = 3 * ATTN_WIDTH
CONV_WIDTH = D_MODEL
CONV_K = 3
SUB_BLOCK = 128
ALIBI_MAX_EXP = 8.0
LN_EPS = 1e-5

OFF_Z_ATTN = QKV_WIDTH
OFF_UX = OFF_Z_ATTN + GROUP_WIDTH
OFF_GB = OFF_UX + CONV_WIDTH
OFF_GC = OFF_GB + CONV_WIDTH
OFF_ZC = OFF_GC + CONV_WIDTH
OFF_GA = OFF_ZC + CONV_WIDTH
D_IN = OFF_GA + 2 * D_MODEL

ROW_TILE = 512
OUT_ROW_TILE = 1024
OUT_CHUNK = 256
CONV_CHUNK = 256
LANES = 128
HALO = 8
VMEM_LIMIT = 56 * 1024 * 1024

F32 = jnp.float32
BF16 = jnp.bfloat16


def _sigmoid(v):
    return 0.5 * jnp.tanh(0.5 * v) + 0.5


def _silu(v):
    return v * _sigmoid(v)


def _ada_kernel(c_ref, w_ref, b_ref, o_ref):
    o_ref[...] = jnp.dot(_silu(c_ref[...]), w_ref[...], preferred_element_type=F32,
                         precision=lax.Precision.HIGHEST) + b_ref[...]


def _ada(c, w_ada, b_ada):
    bsz, d = c.shape
    n_out = w_ada.shape[1]
    return pl.pallas_call(
        _ada_kernel,
        grid=(n_out // d,),
        in_specs=[pl.BlockSpec((bsz, d), lambda j: (0, 0)),
                  pl.BlockSpec((d, d), lambda j: (0, j)),
                  pl.BlockSpec((1, d), lambda j: (0, j))],
        out_specs=pl.BlockSpec((bsz, d), lambda j: (0, j)),
        out_shape=jax.ShapeDtypeStruct((bsz, n_out), F32),
        name="ada",
    )(c, w_ada, b_ada.reshape(1, n_out))


def _inproj_kernel(x_ref, ada_ref, w_ref, b_ref, cw_ref, wpc_ref,
                   qkv_ref, za_ref, yc_ref, u_scr, t_scr, stage_scr):
    tm = x_ref.shape[0]
    first_tile = pl.program_id(1) == 0
    @pl.when(first_tile)
    def _():
        u_scr[0:HALO, :] = jnp.zeros((HALO, CONV_WIDTH), F32)

    @pl.when(jnp.logical_not(first_tile))
    def _():
        u_scr[0:HALO, :] = u_scr[tm:tm + HALO, :]

    shift = ada_ref[0:1, :]
    scale = ada_ref[1:2, :]
    h = (x_ref[...] * (1.0 + scale) + shift).astype(BF16)

    def proj(lo, width):
        return (jnp.dot(h, w_ref[:, lo:lo + width], preferred_element_type=F32)
                + b_ref[:, lo:lo + width])

    slab = 0
    for which in range(3):
        for g, (_, d) in enumerate(DILATED_GROUPS):
            c0 = which * ATTN_WIDTH + g * GROUP_WIDTH
            blk = proj(c0, GROUP_WIDTH)
            if d == 1:
                qkv_ref[:, c0:c0 + GROUP_WIDTH] = blk.astype(BF16)
                continue
            n_t = tm // d
            for lo in range(0, GROUP_WIDTH, LANES):
                stage_scr[slab] = blk[:, lo:lo + LANES]
                for r in range(d):
                    qkv_ref[r * n_t:(r + 1) * n_t, c0 + lo:c0 + lo + LANES] = (
                        stage_scr[slab, pl.ds(r, n_t, stride=d), :].astype(BF16))
                slab += 1

    for c0 in range(0, CONV_WIDTH, CONV_CHUNK):
        cs = slice(c0, c0 + CONV_CHUNK)
        u = proj(OFF_GC + c0, CONV_CHUNK) * proj(OFF_UX + c0, CONV_CHUNK)
        u_scr[HALO:HALO + tm, cs] = u
        conv = cw_ref[0:1, cs] * u_scr[HALO - 2:HALO - 2 + tm, cs]
        conv = conv + cw_ref[1:2, cs] * u_scr[HALO - 1:HALO - 1 + tm, cs]
        conv = conv + cw_ref[2:3, cs] * u
        t = proj(OFF_GB + c0, CONV_CHUNK) * conv * _silu(proj(OFF_ZC + c0, CONV_CHUNK))
        t_scr[:, cs] = t.astype(BF16)
    za_ref[...] = _silu(proj(OFF_Z_ATTN, GROUP_WIDTH)).astype(BF16)
    yc_ref[...] = jnp.dot(t_scr[...], wpc_ref[...], preferred_element_type=F32).astype(BF16)


def _inproj(x, ada3, w_in, b_in, conv_w, w_pc):
    bsz, s, d = x.shape
    tm = ROW_TILE
    const = dict(pipeline_mode=pl.Buffered(1))
    row = lambda width: pl.BlockSpec((None, tm, width), lambda b, i: (b, i, 0))
    return pl.pallas_call(
        _inproj_kernel,
        grid=(bsz, s // tm),
        in_specs=[row(d),
                  pl.BlockSpec((None, 3, d), lambda b, i: (b, 0, 0)),
                  pl.BlockSpec((d, OFF_GA), lambda b, i: (0, 0), **const),
                  pl.BlockSpec((1, OFF_GA), lambda b, i: (0, 0), **const),
                  pl.BlockSpec((CONV_K, CONV_WIDTH), lambda b, i: (0, 0), **const),
                  pl.BlockSpec((CONV_WIDTH, d), lambda b, i: (0, 0), **const)],
        out_specs=[row(QKV_WIDTH), row(GROUP_WIDTH), row(d)],
        out_shape=[jax.ShapeDtypeStruct((bsz, s, QKV_WIDTH), BF16),
                   jax.ShapeDtypeStruct((bsz, s, GROUP_WIDTH), BF16),
                   jax.ShapeDtypeStruct((bsz, s, d), BF16)],
        scratch_shapes=[pltpu.VMEM((tm + HALO, CONV_WIDTH), F32),
                        pltpu.VMEM((tm, CONV_WIDTH), BF16),
                        pltpu.VMEM((3 * (N_GROUPS - 1) * GROUP_WIDTH // LANES, tm, LANES), F32)],
        compiler_params=pltpu.CompilerParams(
            dimension_semantics=("arbitrary", "arbitrary"), vmem_limit_bytes=VMEM_LIMIT),
        name="inproj",
    )(x, ada3, w_in, b_in, conv_w, w_pc)


def _attn_bias_table():
    L = SUB_BLOCK
    tabs = []
    for group, (window, dilation) in enumerate(DILATED_GROUPS):
        span = window // dilation
        heads = np.arange(group * HEADS_PER_GROUP, (group + 1) * HEADS_PER_GROUP, dtype=np.float64)
        slopes = 2.0 ** (-ALIBI_MAX_EXP * (heads + 1.0) / N_ATTN_HEADS)
        delta = (np.arange(L)[:, None] + L - np.arange(2 * L)[None, :]).astype(np.float64)
        valid = (delta >= 0) & (delta <= span)
        bias = np.where(valid[None], -slopes[:, None, None] * (delta * dilation)[None], -np.inf)
        tabs.append(bias.reshape(HEADS_PER_GROUP // 2, 2 * L, 2 * L))
    return np.stack(tabs).astype(np.float32)


def _attn_kernel(qkv_ref, bias_ref, o_ref, res):
    S = qkv_ref.shape[0]
    L = SUB_BLOCK
    pair_w = 2 * HEAD_DIM
    n_pairs = HEADS_PER_GROUP // 2
    low_half = lax.broadcasted_iota(jnp.int32, (L, pair_w), 1) < HEAD_DIM

    def load(g, which, r, blk, pair):
        col = which * ATTN_WIDTH + g * GROUP_WIDTH + pair * pair_w
        d = DILATED_GROUPS[g][1]
        if d == 1:
            return qkv_ref[pl.ds(pl.multiple_of(blk * L, L), L), col:col + pair_w]
        n_t = ROW_TILE // d
        pieces = []
        for t in range(blk * L // n_t, (blk + 1) * L // n_t):
            start = pl.multiple_of(t * ROW_TILE + r * n_t, n_t)
            pieces.append(qkv_ref[pl.ds(start, n_t), col:col + pair_w])
        return pieces[0] if len(pieces) == 1 else jnp.concatenate(pieces, axis=0)

    def attend(g, specs):
        d = DILATED_GROUPS[g][1]
        work = [spec + (pair,) for spec in specs for pair in range(n_pairs)]

        def keys_values(which, r, blk, pair):
            cur = load(g, which, r, blk, pair)
            if isinstance(blk, int) and blk == 0:
                return cur
            return jnp.concatenate([load(g, which, r, blk - 1, pair), cur], axis=0)

        scores = []
        for r, blk, pair in work:
            q = load(g, 0, r, blk, pair) * 0.125
            zero = jnp.zeros_like(q)
            q2 = jnp.concatenate([jnp.where(low_half, q, zero), jnp.where(low_half, zero, q)], axis=0)
            k = keys_values(1, r, blk, pair)
            sc = lax.dot_general(q2, k, (((1,), (1,)), ((), ())), preferred_element_type=F32)
            scores.append(sc + bias_ref[g, pair, :, 2 * L - k.shape[0]:])
        probs = []
        for sc in scores:
            m = jnp.max(sc, axis=-1, keepdims=True)
            probs.append((jnp.exp(sc - m).astype(BF16), m))
        for (r, blk, pair), (p, m) in zip(work, probs):
            v = keys_values(2, r, blk, pair)
            v1 = jnp.concatenate([v, jnp.ones(v.shape, BF16)], axis=1)
            pv = jnp.dot(p, v1, preferred_element_type=F32)
            mb = jnp.broadcast_to(m, (2 * L, pair_w))
            if d == 1:
                rows = pl.ds(pl.multiple_of(blk * L, L), L)
            else:
                rows = pl.ds(blk * (L * d) + r, L, stride=d)
            res[0, g, pair, rows, :] = jnp.where(low_half, pv[:L, :pair_w], pv[L:, :pair_w])
            res[1, g, pair, rows, :] = jnp.where(low_half, pv[:L, pair_w:], pv[L:, pair_w:])
            res[2, g, pair, rows, :] = jnp.where(low_half, mb[:L], mb[L:])

    for g, (_, d) in enumerate(DILATED_GROUPS):
        n_blocks = S // d // L
        if n_blocks == 1:
            per_step = 4

            def sub_body(i, carry, g=g):
                attend(g, [(i * per_step + u, 0) for u in range(per_step)])
                return carry
            lax.fori_loop(0, d // per_step, sub_body, 0)
        elif d == 1:
            per_step = 3
            attend(g, [(0, 0)])

            def blk_body(i, carry, g=g):
                attend(g, [(0, 1 + i * per_step + u) for u in range(per_step)])
                return carry
            lax.fori_loop(0, (n_blocks - 1) // per_step, blk_body, 0)
        else:
            def res_body(r, carry, g=g, n_blocks=n_blocks):
                attend(g, [(r, blk) for blk in range(n_blocks)])
                return carry
            lax.fori_loop(0, d, res_body, 0)

    chunk = 2 * L

    def combine(i, carry):
        rows = pl.ds(pl.multiple_of(i * chunk, chunk), chunk)
        for pair in range(n_pairs):
            ms = [res[2, g, pair, rows, :] for g in range(N_GROUPS)]
            mmax = functools.reduce(jnp.maximum, ms)
            es = [jnp.exp(m - mmax) for m in ms]
            num = sum(e * res[0, g, pair, rows, :] for g, e in enumerate(es))
            den = sum(e * res[1, g, pair, rows, :] for g, e in enumerate(es))
            o_ref[rows, pair * pair_w:(pair + 1) * pair_w] = num / den
        return carry
    lax.fori_loop(0, S // chunk, combine, 0)


def _attention(qkv):
    bsz, s, _ = qkv.shape
    bias = jnp.asarray(_attn_bias_table())
    return pl.pallas_call(
        _attn_kernel,
        grid=(bsz,),
        in_specs=[pl.BlockSpec((None, s, QKV_WIDTH), lambda b: (b, 0, 0)),
                  pl.BlockSpec(bias.shape, lambda b: (0, 0, 0, 0), pipeline_mode=pl.Buffered(1))],
        out_specs=pl.BlockSpec((None, s, GROUP_WIDTH), lambda b: (b, 0, 0)),
        out_shape=jax.ShapeDtypeStruct((bsz, s, GROUP_WIDTH), F32),
        scratch_shapes=[pltpu.VMEM((3, N_GROUPS, HEADS_PER_GROUP // 2, s, 2 * HEAD_DIM), F32)],
        compiler_params=pltpu.CompilerParams(
            dimension_semantics=("arbitrary",), vmem_limit_bytes=VMEM_LIMIT),
        name="attn",
    )(qkv, bias)


def _out_kernel(x_ref, ada_ref, oa_ref, za_ref, yc_ref, wg_ref, bg_ref, wpa_ref, wout_ref, bout_ref,
                lng_ref, lnb_ref, out_ref, *, alpha):
    shift = ada_ref[0:1, :]
    scale = ada_ref[1:2, :]
    gate = ada_ref[2:3, :]
    def mix_gates(r0):
        h = (x_ref[r0:r0 + OUT_CHUNK, :] * (1.0 + scale) + shift).astype(BF16)
        return _sigmoid(jnp.dot(h, wg_ref[...], preferred_element_type=F32) + bg_ref[...]).astype(BF16)

    starts = list(range(0, x_ref.shape[0], OUT_CHUNK))
    g_next = mix_gates(starts[0])
    for idx, r0 in enumerate(starts):
        rows = slice(r0, r0 + OUT_CHUNK)
        g_mix = g_next
        ta = (oa_ref[rows, :] * za_ref[rows, :].astype(F32)).astype(BF16)
        y_attn = jnp.dot(ta, wpa_ref[...], preferred_element_type=F32)
        if idx + 1 < len(starts):
            g_next = mix_gates(starts[idx + 1])
        merged = g_mix[:, :D_MODEL] * y_attn.astype(BF16) + g_mix[:, D_MODEL:] * yc_ref[rows, :]
        sub = gate * (jnp.dot(merged, wout_ref[...], preferred_element_type=F32) + bout_ref[...])
        r = alpha * x_ref[rows, :] + sub
        mu = jnp.mean(r, axis=-1, keepdims=True)
        cen = r - mu
        var = jnp.mean(cen * cen, axis=-1, keepdims=True)
        out_ref[rows, :] = cen * lax.rsqrt(var + LN_EPS) * lng_ref[...] + lnb_ref[...]


def _out(x, ada3, o_attn, za, yc, w_gate, b_gate, w_pa, w_out, b_out, ln_g, ln_b, alpha):
    bsz, s, d = x.shape
    tm = OUT_ROW_TILE
    const = dict(pipeline_mode=pl.Buffered(1))
    row = lambda width: pl.BlockSpec((None, tm, width), lambda b, i: (b, i, 0))
    vec = pl.BlockSpec((1, d), lambda b, i: (0, 0), **const)
    return pl.pallas_call(
        functools.partial(_out_kernel, alpha=alpha),
        grid=(bsz, s // tm),
        in_specs=[row(d), pl.BlockSpec((None, 3, d), lambda b, i: (b, 0, 0))]
                 + [row(GROUP_WIDTH), row(GROUP_WIDTH), row(d),
                    pl.BlockSpec((d, 2 * d), lambda b, i: (0, 0), **const),
                    pl.BlockSpec((1, 2 * d), lambda b, i: (0, 0), **const),
                    pl.BlockSpec((GROUP_WIDTH, d), lambda b, i: (0, 0), **const),
                    pl.BlockSpec((d, d), lambda b, i: (0, 0), **const),
                    vec, vec, vec],
        out_specs=row(d),
        out_shape=jax.ShapeDtypeStruct((bsz, s, d), F32),
        compiler_params=pltpu.CompilerParams(
            dimension_semantics=("arbitrary", "arbitrary"), vmem_limit_bytes=VMEM_LIMIT),
        name="out",
    )(x, ada3, o_attn, za, yc, w_gate, b_gate, w_pa, w_out, b_out, ln_g, ln_b)


def kernel(x, c, w_ada, b_ada, w_in, b_in, conv_w, w_proj_attn, w_proj_conv, w_out, b_out, ln_g, ln_b):
    bsz, s, d = x.shape
    depth = w_in.shape[0]
    alpha = (2.0 * depth) ** 0.25
    for layer in range(depth):
        ada3 = _ada(c, w_ada[layer], b_ada[layer]).reshape(bsz, 3, d)
        b_in2 = b_in[layer].reshape(1, D_IN)
        qkv, za, yc = _inproj(x, ada3, w_in[layer, :, :OFF_GA].astype(BF16), b_in2[:, :OFF_GA],
                              conv_w[layer], w_proj_conv[layer].astype(BF16))
        x = _out(x, ada3, _attention(qkv), za, yc,
                 w_in[layer, :, OFF_GA:].astype(BF16), b_in2[:, OFF_GA:],
                 w_proj_attn[layer].astype(BF16), w_out[layer].astype(BF16),
                 b_out[layer].reshape(1, d), ln_g[layer].reshape(1, d), ln_b[layer].reshape(1, d), alpha)
    return x
```

```python
import functools

import numpy as np
import jax
import jax.numpy as jnp
from jax import lax
from jax.experimental import pallas as pl
from jax.experimental.pallas import tpu as pltpu

D_MODEL = 1024
HEAD_DIM = 64
HEADS_PER_GROUP = 4
DILATED_GROUPS = ((128, 1), (512, 4), (2048, 16))
N_GROUPS = len(DILATED_GROUPS)
N_ATTN_HEADS = N_GROUPS * HEADS_PER_GROUP
ATTN_WIDTH = N_ATTN_HEADS * HEAD_DIM
GROUP_WIDTH = HEADS_PER_GROUP * HEAD_DIM
QKV_WIDTH = 3 * ATTN_WIDTH
CONV_WIDTH = D_MODEL
CONV_K = 3
SUB_BLOCK = 128
ALIBI_MAX_EXP = 8.0
LN_EPS = 1e-5

OFF_Z_ATTN = QKV_WIDTH
OFF_UX = OFF_Z_ATTN + GROUP_WIDTH
OFF_GB = OFF_UX + CONV_WIDTH
OFF_GC = OFF_GB + CONV_WIDTH
OFF_ZC = OFF_GC + CONV_WIDTH
OFF_GA = OFF_ZC + CONV_WIDTH
D_IN = OFF_GA + 2 * D_MODEL

ROW_TILE = 512
OUT_ROW_TILE = 1024
OUT_CHUNK = 512
CONV_CHUNK = 256
LANES = 128
REGROUP_STRIDE = 4
N_STAGE_SLABS = 3 * (GROUP_WIDTH // LANES) * sum(
    1 if d == REGROUP_STRIDE else 2 for _, d in DILATED_GROUPS if d > 1)
HALO = 8
VMEM_LIMIT = 56 * 1024 * 1024

F32 = jnp.float32
BF16 = jnp.bfloat16


def _sigmoid(v):
    return 0.5 * jnp.tanh(0.5 * v) + 0.5


def _silu(v):
    return v * _sigmoid(v)


def _ada_kernel(c_ref, w_ref, b_ref, o_ref):
    o_ref[...] = jnp.dot(_silu(c_ref[...]), w_ref[...], preferred_element_type=F32,
                         precision=lax.Precision.HIGHEST) + b_ref[...]


def _ada(c, w_ada, b_ada):
    bsz, d = c.shape
    n_out = w_ada.shape[1]
    return pl.pallas_call(
        _ada_kernel,
        grid=(n_out // d,),
        in_specs=[pl.BlockSpec((bsz, d), lambda j: (0, 0)),
                  pl.BlockSpec((d, d), lambda j: (0, j)),
                  pl.BlockSpec((1, d), lambda j: (0, j))],
        out_specs=pl.BlockSpec((bsz, d), lambda j: (0, j)),
        out_shape=jax.ShapeDtypeStruct((bsz, n_out), F32),
        name="ada",
    )(c, w_ada, b_ada.reshape(1, n_out))


def _inproj_kernel(x_ref, ada_ref, w_ref, b_ref, cw_ref, wpc_ref,
                   qkv_ref, za_ref, yc_ref, u_scr, t_scr, stage_scr):
    tm = x_ref.shape[0]
    first_tile = pl.program_id(1) == 0
    @pl.when(first_tile)
    def _():
        u_scr[0:HALO, :] = jnp.zeros((HALO, CONV_WIDTH), F32)

    @pl.when(jnp.logical_not(first_tile))
    def _():
        u_scr[0:HALO, :] = u_scr[tm:tm + HALO, :]

    shift = ada_ref[0:1, :]
    scale = ada_ref[1:2, :]
    h = (x_ref[...] * (1.0 + scale) + shift).astype(BF16)

    def proj(lo, width):
        return (jnp.dot(h, w_ref[:, lo:lo + width], preferred_element_type=F32)
                + b_ref[:, lo:lo + width])

    slab = 0
    for which in range(3):
        for g, (_, d) in enumerate(DILATED_GROUPS):
            c0 = which * ATTN_WIDTH + g * GROUP_WIDTH
            blk = proj(c0, GROUP_WIDTH)
            if d == 1:
                qkv_ref[:, c0:c0 + GROUP_WIDTH] = blk.astype(BF16)
                continue
            n_t = tm // d
            for lo in range(0, GROUP_WIDTH, LANES):
                cols = slice(c0 + lo, c0 + lo + LANES)
                stage_scr[slab] = blk[:, lo:lo + LANES]
                if d == REGROUP_STRIDE:
                    for r in range(d):
                        qkv_ref[r * n_t:(r + 1) * n_t, cols] = (
                            stage_scr[slab, pl.ds(r, n_t, stride=d), :].astype(BF16))
                else:
                    assert d == REGROUP_STRIDE * REGROUP_STRIDE
                    n_1 = tm // REGROUP_STRIDE
                    for r1 in range(REGROUP_STRIDE):
                        stage_scr[slab + 1, r1 * n_1:(r1 + 1) * n_1, :] = (
                            stage_scr[slab, pl.ds(r1, n_1, stride=REGROUP_STRIDE), :])
                    for r1 in range(REGROUP_STRIDE):
                        for r2 in range(REGROUP_STRIDE):
                            r = r2 * REGROUP_STRIDE + r1
                            qkv_ref[r * n_t:(r + 1) * n_t, cols] = stage_scr[
                                slab + 1, pl.ds(r1 * n_1 + r2, n_t, stride=REGROUP_STRIDE), :].astype(BF16)
                    slab += 1
                slab += 1

    for c0 in range(0, CONV_WIDTH, CONV_CHUNK):
        cs = slice(c0, c0 + CONV_CHUNK)
        u = proj(OFF_GC + c0, CONV_CHUNK) * proj(OFF_UX + c0, CONV_CHUNK)
        u_scr[HALO:HALO + tm, cs] = u
        conv = cw_ref[0:1, cs] * u_scr[HALO - 2:HALO - 2 + tm, cs]
        conv = conv + cw_ref[1:2, cs] * u_scr[HALO - 1:HALO - 1 + tm, cs]
        conv = conv + cw_ref[2:3, cs] * u
        t = proj(OFF_GB + c0, CONV_CHUNK) * conv * _silu(proj(OFF_ZC + c0, CONV_CHUNK))
        t_scr[:, cs] = t.astype(BF16)
    za_ref[...] = _silu(proj(OFF_Z_ATTN, GROUP_WIDTH)).astype(BF16)
    yc_ref[...] = jnp.dot(t_scr[...], wpc_ref[...], preferred_element_type=F32).astype(BF16)


def _inproj(x, ada3, w_in, b_in, conv_w, w_pc):
    bsz, s, d = x.shape
    tm = ROW_TILE
    const = dict(pipeline_mode=pl.Buffered(1))
    row = lambda width: pl.BlockSpec((None, tm, width), lambda b, i: (b, i, 0))
    return pl.pallas_call(
        _inproj_kernel,
        grid=(bsz, s // tm),
        in_specs=[row(d),
                  pl.BlockSpec((None, 3, d), lambda b, i: (b, 0, 0)),
                  pl.BlockSpec((d, OFF_GA), lambda b, i: (0, 0), **const),
                  pl.BlockSpec((1, OFF_GA), lambda b, i: (0, 0), **const),
                  pl.BlockSpec((CONV_K, CONV_WIDTH), lambda b, i: (0, 0), **const),
                  pl.BlockSpec((CONV_WIDTH, d), lambda b, i: (0, 0), **const)],
        out_specs=[row(QKV_WIDTH), row(GROUP_WIDTH), row(d)],
        out_shape=[jax.ShapeDtypeStruct((bsz, s, QKV_WIDTH), BF16),
                   jax.ShapeDtypeStruct((bsz, s, GROUP_WIDTH), BF16),
                   jax.ShapeDtypeStruct((bsz, s, d), BF16)],
        scratch_shapes=[pltpu.VMEM((tm + HALO, CONV_WIDTH), F32),
                        pltpu.VMEM((tm, CONV_WIDTH), BF16),
                        pltpu.VMEM((N_STAGE_SLABS, tm, LANES), F32)],
        compiler_params=pltpu.CompilerParams(
            dimension_semantics=("arbitrary", "arbitrary"), vmem_limit_bytes=VMEM_LIMIT),
        name="inproj",
    )(x, ada3, w_in, b_in, conv_w, w_pc)


def _attn_bias_table():
    L = SUB_BLOCK
    tabs = []
    for group, (window, dilation) in enumerate(DILATED_GROUPS):
        span = window // dilation
        heads = np.arange(group * HEADS_PER_GROUP, (group + 1) * HEADS_PER_GROUP, dtype=np.float64)
        slopes = 2.0 ** (-ALIBI_MAX_EXP * (heads + 1.0) / N_ATTN_HEADS)
        delta = (np.arange(L)[:, None] + L - np.arange(2 * L)[None, :]).astype(np.float64)
        valid = (delta >= 0) & (delta <= span)
        bias = np.where(valid[None], -slopes[:, None, None] * (delta * dilation)[None], -np.inf)
        tabs.append(bias.reshape(HEADS_PER_GROUP // 2, 2 * L, 2 * L))
    return np.stack(tabs).astype(np.float32)


def _attn_kernel(qkv_ref, bias_ref, o_ref, res):
    S = qkv_ref.shape[0]
    L = SUB_BLOCK
    pair_w = 2 * HEAD_DIM
    n_pairs = HEADS_PER_GROUP // 2
    low_half = lax.broadcasted_iota(jnp.int32, (L, pair_w), 1) < HEAD_DIM

    def load(g, which, r, blk, pair):
        col = which * ATTN_WIDTH + g * GROUP_WIDTH + pair * pair_w
        d = DILATED_GROUPS[g][1]
        if d == 1:
            return qkv_ref[pl.ds(pl.multiple_of(blk * L, L), L), col:col + pair_w]
        n_t = ROW_TILE // d
        pieces = []
        for t in range(blk * L // n_t, (blk + 1) * L // n_t):
            start = pl.multiple_of(t * ROW_TILE + r * n_t, n_t)
            pieces.append(qkv_ref[pl.ds(start, n_t), col:col + pair_w])
        return pieces[0] if len(pieces) == 1 else jnp.concatenate(pieces, axis=0)

    def attend(g, specs):
        d = DILATED_GROUPS[g][1]
        work = [spec + (pair,) for spec in specs for pair in range(n_pairs)]

        def keys_values(which, r, blk, pair):
            cur = load(g, which, r, blk, pair)
            if isinstance(blk, int) and blk == 0:
                return cur
            return jnp.concatenate([load(g, which, r, blk - 1, pair), cur], axis=0)

        scores = []
        for r, blk, pair in work:
            q = load(g, 0, r, blk, pair) * 0.125
            zero = jnp.zeros_like(q)
            q2 = jnp.concatenate([jnp.where(low_half, q, zero), jnp.where(low_half, zero, q)], axis=0)
            k = keys_values(1, r, blk, pair)
            sc = lax.dot_general(q2, k, (((1,), (1,)), ((), ())), preferred_element_type=F32)
            scores.append(sc + bias_ref[g, pair, :, 2 * L - k.shape[0]:])
        probs = []
        for sc in scores:
            m = jnp.max(sc, axis=-1, keepdims=True)
            probs.append((jnp.exp(sc - m).astype(BF16), m))
        for (r, blk, pair), (p, m) in zip(work, probs):
            v = keys_values(2, r, blk, pair)
            v1 = jnp.concatenate([v, jnp.ones(v.shape, BF16)], axis=1)
            pv = jnp.dot(p, v1, preferred_element_type=F32)
            mb = jnp.broadcast_to(m, (2 * L, pair_w))
            if d == 1:
                rows = pl.ds(pl.multiple_of(blk * L, L), L)
            else:
                rows = pl.ds(blk * (L * d) + r, L, stride=d)
            res[0, g, pair, rows, :] = jnp.where(low_half, pv[:L, :pair_w], pv[L:, :pair_w])
            res[1, g, pair, rows, :] = jnp.where(low_half, pv[:L, pair_w:], pv[L:, pair_w:])
            res[2, g, pair, rows, :] = jnp.where(low_half, mb[:L], mb[L:])

    for g, (_, d) in enumerate(DILATED_GROUPS):
        n_blocks = S // d // L
        if n_blocks == 1:
            per_step = 4

            def sub_body(i, carry, g=g):
                attend(g, [(i * per_step + u, 0) for u in range(per_step)])
                return carry
            lax.fori_loop(0, d // per_step, sub_body, 0)
        elif d == 1:
            per_step = 3
            attend(g, [(0, 0)])

            def blk_body(i, carry, g=g):
                attend(g, [(0, 1 + i * per_step + u) for u in range(per_step)])
                return carry
            lax.fori_loop(0, (n_blocks - 1) // per_step, blk_body, 0)
        else:
            def res_body(r, carry, g=g, n_blocks=n_blocks):
                attend(g, [(r, blk) for blk in range(n_blocks)])
                return carry
            lax.fori_loop(0, d, res_body, 0)

    chunk = 2 * L

    def combine(i, carry):
        rows = pl.ds(pl.multiple_of(i * chunk, chunk), chunk)
        for pair in range(n_pairs):
            ms = [res[2, g, pair, rows, :] for g in range(N_GROUPS)]
            mmax = functools.reduce(jnp.maximum, ms)
            es = [jnp.exp(m - mmax) for m in ms]
            num = sum(e * res[0, g, pair, rows, :] for g, e in enumerate(es))
            den = sum(e * res[1, g, pair, rows, :] for g, e in enumerate(es))
            o_ref[rows, pair * pair_w:(pair + 1) * pair_w] = num / den
        return carry
    lax.fori_loop(0, S // chunk, combine, 0)


def _attention(qkv):
    bsz, s, _ = qkv.shape
    bias = jnp.asarray(_attn_bias_table())
    return pl.pallas_call(
        _attn_kernel,
        grid=(bsz,),
        in_specs=[pl.BlockSpec((None, s, QKV_WIDTH), lambda b: (b, 0, 0)),
                  pl.BlockSpec(bias.shape, lambda b: (0, 0, 0, 0), pipeline_mode=pl.Buffered(1))],
        out_specs=pl.BlockSpec((None, s, GROUP_WIDTH), lambda b: (b, 0, 0)),
        out_shape=jax.ShapeDtypeStruct((bsz, s, GROUP_WIDTH), F32),
        scratch_shapes=[pltpu.VMEM((3, N_GROUPS, HEADS_PER_GROUP // 2, s, 2 * HEAD_DIM), F32)],
        compiler_params=pltpu.CompilerParams(
            dimension_semantics=("arbitrary",), vmem_limit_bytes=VMEM_LIMIT),
        name="attn",
    )(qkv, bias)


def _out_kernel(x_ref, ada_ref, oa_ref, za_ref, yc_ref, wg_ref, bg_ref, wpa_ref, wout_ref, bout_ref,
                lng_ref, lnb_ref, out_ref, *, alpha):
    shift = ada_ref[0:1, :]
    scale = ada_ref[1:2, :]
    gate = ada_ref[2:3, :]
    def mix_gates(r0):
        h = (x_ref[r0:r0 + OUT_CHUNK, :] * (1.0 + scale) + shift).astype(BF16)
        return _sigmoid(jnp.dot(h, wg_ref[...], preferred_element_type=F32) + bg_ref[...]).astype(BF16)

    starts = list(range(0, x_ref.shape[0], OUT_CHUNK))
    g_next = mix_gates(starts[0])
    for idx, r0 in enumerate(starts):
        rows = slice(r0, r0 + OUT_CHUNK)
        g_mix = g_next
        ta = (oa_ref[rows, :] * za_ref[rows, :].astype(F32)).astype(BF16)
        y_attn = jnp.dot(ta, wpa_ref[...], preferred_element_type=F32)
        if idx + 1 < len(starts):
            g_next = mix_gates(starts[idx + 1])
        merged = g_mix[:, :D_MODEL] * y_attn.astype(BF16) + g_mix[:, D_MODEL:] * yc_ref[rows, :]
        sub = gate * (jnp.dot(merged, wout_ref[...], preferred_element_type=F32) + bout_ref[...])
        r = alpha * x_ref[rows, :] + sub
        mu = jnp.mean(r, axis=-1, keepdims=True)
        cen = r - mu
        var = jnp.mean(cen * cen, axis=-1, keepdims=True)
        out_ref[rows, :] = cen * lax.rsqrt(var + LN_EPS) * lng_ref[...] + lnb_ref[...]


def _out(x, ada3, o_attn, za, yc, w_gate, b_gate, w_pa, w_out, b_out, ln_g, ln_b, alpha):
    bsz, s, d = x.shape
    tm = OUT_ROW_TILE
    const = dict(pipeline_mode=pl.Buffered(1))
    row = lambda width: pl.BlockSpec((None, tm, width), lambda b, i: (b, i, 0))
    vec = pl.BlockSpec((1, d), lambda b, i: (0, 0), **const)
    return pl.pallas_call(
        functools.partial(_out_kernel, alpha=alpha),
        grid=(bsz, s // tm),
        in_specs=[row(d), pl.BlockSpec((None, 3, d), lambda b, i: (b, 0, 0))]
                 + [row(GROUP_WIDTH), row(GROUP_WIDTH), row(d),
                    pl.BlockSpec((d, 2 * d), lambda b, i: (0, 0), **const),
                    pl.BlockSpec((1, 2 * d), lambda b, i: (0, 0), **const),
                    pl.BlockSpec((GROUP_WIDTH, d), lambda b, i: (0, 0), **const),
                    pl.BlockSpec((d, d), lambda b, i: (0, 0), **const),
                    vec, vec, vec],
        out_specs=row(d),
        out_shape=jax.ShapeDtypeStruct((bsz, s, d), F32),
        compiler_params=pltpu.CompilerParams(
            dimension_semantics=("arbitrary", "arbitrary"), vmem_limit_bytes=VMEM_LIMIT),
        name="out",
    )(x, ada3, o_attn, za, yc, w_gate, b_gate, w_pa, w_out, b_out, ln_g, ln_b)


def kernel(x, c, w_ada, b_ada, w_in, b_in, conv_w, w_proj_attn, w_proj_conv, w_out, b_out, ln_g, ln_b):
    bsz, s, d = x.shape
    depth = w_in.shape[0]
    alpha = (2.0 * depth) ** 0.25
    for layer in range(depth):
        ada3 = _ada(c, w_ada[layer], b_ada[layer]).reshape(bsz, 3, d)
        b_in2 = b_in[layer].reshape(1, D_IN)
        qkv, za, yc = _inproj(x, ada3, w_in[layer, :, :OFF_GA].astype(BF16), b_in2[:, :OFF_GA],
                              conv_w[layer], w_proj_conv[layer].astype(BF16))
        x = _out(x, ada3, _attention(qkv), za, yc,
                 w_in[layer, :, OFF_GA:].astype(BF16), b_in2[:, OFF_GA:],
                 w_proj_attn[layer].astype(BF16), w_out[layer].astype(BF16),
                 b_out[layer].reshape(1, d), ln_g[layer].reshape(1, d), ln_b[layer].reshape(1, d), alpha)
    return x
```

```python
import functools

import numpy as np
import jax
import jax.numpy as jnp
from jax import lax
from jax.experimental import pallas as pl
from jax.experimental.pallas import tpu as pltpu

D_MODEL = 1024
HEAD_DIM = 64
HEADS_PER_GROUP = 4
DILATED_GROUPS = ((128, 1), (512, 4), (2048, 16))
N_GROUPS = len(DILATED_GROUPS)
N_ATTN_HEADS = N_GROUPS * HEADS_PER_GROUP
ATTN_WIDTH = N_ATTN_HEADS * HEAD_DIM
GROUP_WIDTH = HEADS_PER_GROUP * HEAD_DIM
QKV_WIDTH = 3 * ATTN_WIDTH
CONV_WIDTH = D_MODEL
CONV_K = 3
SUB_BLOCK = 128
ALIBI_MAX_EXP = 8.0
LN_EPS = 1e-5

OFF_Z_ATTN = QKV_WIDTH
OFF_UX = OFF_Z_ATTN + GROUP_WIDTH
OFF_GB = OFF_UX + CONV_WIDTH
OFF_GC = OFF_GB + CONV_WIDTH
OFF_ZC = OFF_GC + CONV_WIDTH
OFF_GA = OFF_ZC + CONV_WIDTH
D_IN = OFF_GA + 2 * D_MODEL

ROW_TILE = 512
OUT_ROW_TILE = 1024
OUT_CHUNK = 512
CONV_CHUNK = 256
LANES = 128
PAIR_WIDTH = 2 * HEAD_DIM
N_PAIRS = HEADS_PER_GROUP // 2
REGROUP_STRIDE = 4
N_STAGE_SLABS = 3 * (GROUP_WIDTH // LANES) * sum(
    1 if d == REGROUP_STRIDE else 2 for _, d in DILATED_GROUPS if d > 1)
HALO = 8
VMEM_LIMIT = 56 * 1024 * 1024

F32 = jnp.float32
BF16 = jnp.bfloat16

assert PAIR_WIDTH == LANES and ROW_TILE // DILATED_GROUPS[1][1] == SUB_BLOCK
assert CONV_WIDTH // CONV_CHUNK == ROW_TILE // SUB_BLOCK


def _sigmoid(v):
    return 0.5 * jnp.tanh(0.5 * v) + 0.5


def _silu(v):
    return v * _sigmoid(v)


def _ada_kernel(c_ref, w_ref, b_ref, o_ref):
    o_ref[...] = jnp.dot(_silu(c_ref[...]), w_ref[...], preferred_element_type=F32,
                         precision=lax.Precision.HIGHEST) + b_ref[...]


def _ada(c, w_ada, b_ada):
    bsz, d = c.shape
    n_out = w_ada.shape[1]
    return pl.pallas_call(
        _ada_kernel,
        grid=(n_out // d,),
        in_specs=[pl.BlockSpec((bsz, d), lambda j: (0, 0)),
                  pl.BlockSpec((d, d), lambda j: (0, j)),
                  pl.BlockSpec((1, d), lambda j: (0, j))],
        out_specs=pl.BlockSpec((bsz, d), lambda j: (0, j)),
        out_shape=jax.ShapeDtypeStruct((bsz, n_out), F32),
        name="ada",
    )(c, w_ada, b_ada.reshape(1, n_out))


def _attn_bias_table():
    L = SUB_BLOCK
    tabs = []
    for group, (window, dilation) in enumerate(DILATED_GROUPS):
        span = window // dilation
        heads = np.arange(group * HEADS_PER_GROUP, (group + 1) * HEADS_PER_GROUP, dtype=np.float64)
        slopes = 2.0 ** (-ALIBI_MAX_EXP * (heads + 1.0) / N_ATTN_HEADS)
        delta = (np.arange(L)[:, None] + L - np.arange(2 * L)[None, :]).astype(np.float64)
        valid = (delta >= 0) & (delta <= span)
        bias = np.where(valid[None], -slopes[:, None, None] * (delta * dilation)[None], -np.inf)
        tabs.append(bias.reshape(HEADS_PER_GROUP // 2, 2 * L, 2 * L))
    return np.stack(tabs).astype(np.float32)


def _proj_attn_kernel(x_ref, ada_ref, w_ref, b_ref, cw_ref, wpc_ref, bias_ref,
                      za_ref, yc_ref, o_ref, u_scr, t_scr, stage_scr, qkv_scr, res):
    tm = x_ref.shape[0]
    S = qkv_scr.shape[0]
    L = SUB_BLOCK
    tile = pl.program_id(1)
    first_tile = tile == 0
    base = pl.multiple_of(tile * tm, tm)
    prev_base = pl.multiple_of(jnp.maximum(tile - 1, 0) * tm, tm)

    @pl.when(first_tile)
    def _():
        u_scr[0:HALO, :] = jnp.zeros((HALO, CONV_WIDTH), F32)

    @pl.when(jnp.logical_not(first_tile))
    def _():
        u_scr[0:HALO, :] = u_scr[tm:tm + HALO, :]

    shift = ada_ref[0:1, :]
    scale = ada_ref[1:2, :]
    h = (x_ref[...] * (1.0 + scale) + shift).astype(BF16)

    def proj(lo, width):
        return (jnp.dot(h, w_ref[:, lo:lo + width], preferred_element_type=F32)
                + b_ref[:, lo:lo + width])

    slab = 0
    for which in range(3):
        for g, (_, d) in enumerate(DILATED_GROUPS):
            c0 = which * ATTN_WIDTH + g * GROUP_WIDTH
            blk = proj(c0, GROUP_WIDTH)
            if d == 1:
                qkv_scr[pl.ds(base, tm), c0:c0 + GROUP_WIDTH] = blk.astype(BF16)
                continue
            n_t = tm // d
            for lo in range(0, GROUP_WIDTH, LANES):
                cols = slice(c0 + lo, c0 + lo + LANES)
                stage_scr[slab] = blk[:, lo:lo + LANES]
                if d == REGROUP_STRIDE:
                    for r in range(d):
                        qkv_scr[pl.ds(base + r * n_t, n_t), cols] = (
                            stage_scr[slab, pl.ds(r, n_t, stride=d), :].astype(BF16))
                else:
                    assert d == REGROUP_STRIDE * REGROUP_STRIDE
                    n_1 = tm // REGROUP_STRIDE
                    for r1 in range(REGROUP_STRIDE):
                        stage_scr[slab + 1, r1 * n_1:(r1 + 1) * n_1, :] = (
                            stage_scr[slab, pl.ds(r1, n_1, stride=REGROUP_STRIDE), :])
                    for r1 in range(REGROUP_STRIDE):
                        for r2 in range(REGROUP_STRIDE):
                            r = r2 * REGROUP_STRIDE + r1
                            qkv_scr[pl.ds(base + r * n_t, n_t), cols] = stage_scr[
                                slab + 1, pl.ds(r1 * n_1 + r2, n_t, stride=REGROUP_STRIDE), :].astype(BF16)
                    slab += 1
                slab += 1

    low_half = lax.broadcasted_iota(jnp.int32, (L, PAIR_WIDTH), 1) < HEAD_DIM
    no_prev = jnp.logical_and(lax.broadcasted_iota(jnp.int32, (2 * L, 2 * L), 1) < L, first_tile)

    def col(g, which, pair):
        return which * ATTN_WIDTH + g * GROUP_WIDTH + pair * PAIR_WIDTH

    def probabilities(g, pair, q, k, maybe_no_prev):
        q = q * 0.125
        zero = jnp.zeros_like(q)
        q2 = jnp.concatenate([jnp.where(low_half, q, zero), jnp.where(low_half, zero, q)], axis=0)
        sc = lax.dot_general(q2, k, (((1,), (1,)), ((), ())), preferred_element_type=F32)
        sc = sc + bias_ref[g, pair, :, 2 * L - k.shape[0]:]
        if maybe_no_prev:
            sc = jnp.where(no_prev, -jnp.inf, sc)
        m = jnp.max(sc, axis=-1, keepdims=True)
        return jnp.exp(sc - m).astype(BF16), m

    def weighted_values(p, m, v):
        v1 = jnp.concatenate([v, jnp.ones(v.shape, BF16)], axis=1)
        pv = jnp.dot(p, v1, preferred_element_type=F32)
        mb = jnp.broadcast_to(m, (2 * L, PAIR_WIDTH))
        return (jnp.where(low_half, pv[:L, :PAIR_WIDTH], pv[L:, :PAIR_WIDTH]),
                jnp.where(low_half, pv[:L, PAIR_WIDTH:], pv[L:, PAIR_WIDTH:]),
                jnp.where(low_half, mb[:L], mb[L:]))

    def merged(pair, rows, acc, den, m):
        m_old = res[2, pair, rows, :]
        m_new = jnp.maximum(m_old, m)
        a = jnp.exp(m_old - m_new)
        b = jnp.exp(m - m_new)
        return a * res[0, pair, rows, :] + b * acc, a * res[1, pair, rows, :] + b * den, m_new

    g1_d = DILATED_GROUPS[1][1]

    def g1_scores(r):
        out = []
        for pair in range(N_PAIRS):
            cur = pl.ds(base + r * L, L)
            prev = pl.ds(prev_base + r * L, L)
            k = jnp.concatenate([qkv_scr[prev, pl.ds(col(1, 1, pair), PAIR_WIDTH)],
                                 qkv_scr[cur, pl.ds(col(1, 1, pair), PAIR_WIDTH)]], axis=0)
            out.append(probabilities(1, pair, qkv_scr[cur, pl.ds(col(1, 0, pair), PAIR_WIDTH)], k, True))
        return out

    def g1_finish(r, parts):
        for pair, (p, m) in enumerate(parts):
            cur = pl.ds(base + r * L, L)
            prev = pl.ds(prev_base + r * L, L)
            v = jnp.concatenate([qkv_scr[prev, pl.ds(col(1, 2, pair), PAIR_WIDTH)],
                                 qkv_scr[cur, pl.ds(col(1, 2, pair), PAIR_WIDTH)]], axis=0)
            acc, den, mm = weighted_values(p, m, v)
            rows = pl.ds(base + r, L, stride=g1_d)
            res[0, pair, rows, :] = acc
            res[1, pair, rows, :] = den
            res[2, pair, rows, :] = mm

    def g0_rows(c):
        q0 = base + c * L
        k0 = jnp.maximum(q0 - L, 0) if c == 0 else q0 - L
        return pl.ds(pl.multiple_of(q0, L), L), pl.ds(pl.multiple_of(k0, L), L)

    def g0_scores(c):
        cur, prev = g0_rows(c)
        out = []
        for pair in range(N_PAIRS):
            k = jnp.concatenate([qkv_scr[prev, pl.ds(col(0, 1, pair), PAIR_WIDTH)],
                                 qkv_scr[cur, pl.ds(col(0, 1, pair), PAIR_WIDTH)]], axis=0)
            out.append(probabilities(0, pair, qkv_scr[cur, pl.ds(col(0, 0, pair), PAIR_WIDTH)], k, c == 0))
        return out

    def g0_finish(c, parts):
        cur, prev = g0_rows(c)
        for pair, (p, m) in enumerate(parts):
            v = jnp.concatenate([qkv_scr[prev, pl.ds(col(0, 2, pair), PAIR_WIDTH)],
                                 qkv_scr[cur, pl.ds(col(0, 2, pair), PAIR_WIDTH)]], axis=0)
            acc, den, mm = merged(pair, cur, *weighted_values(p, m, v))
            res[0, pair, cur, :] = acc
            res[1, pair, cur, :] = den
            res[2, pair, cur, :] = mm

    items = ([(g1_scores, g1_finish, (2 * j, 2 * j + 1)) for j in range(2)]
             + [(g0_scores, g0_finish, (2 * j, 2 * j + 1)) for j in range(2)])

    def start(item):
        scores_fn, finish_fn, args = item
        return finish_fn, [(a, scores_fn(a)) for a in args]

    def finish(pending):
        finish_fn, parts = pending
        for a, part in parts:
            finish_fn(a, part)

    pending = start(items[0])
    for ci, c0 in enumerate(range(0, CONV_WIDTH, CONV_CHUNK)):
        cs = slice(c0, c0 + CONV_CHUNK)
        u = proj(OFF_GC + c0, CONV_CHUNK) * proj(OFF_UX + c0, CONV_CHUNK)
        u_scr[HALO:HALO + tm, cs] = u
        conv = cw_ref[0:1, cs] * u_scr[HALO - 2:HALO - 2 + tm, cs]
        conv = conv + cw_ref[1:2, cs] * u_scr[HALO - 1:HALO - 1 + tm, cs]
        conv = conv + cw_ref[2:3, cs] * u
        t = proj(OFF_GB + c0, CONV_CHUNK) * conv * _silu(proj(OFF_ZC + c0, CONV_CHUNK))
        t_scr[:, cs] = t.astype(BF16)
        finish(pending)
        pending = start(items[ci + 1]) if ci + 1 < len(items) else None
    za_ref[...] = _silu(proj(OFF_Z_ATTN, GROUP_WIDTH)).astype(BF16)
    yc_ref[...] = jnp.dot(t_scr[...], wpc_ref[...], preferred_element_type=F32).astype(BF16)

    g2_d = DILATED_GROUPS[2][1]
    n_t2 = tm // g2_d
    per_step = 4

    def g2_piece(which, r, pair):
        pieces = [qkv_scr[pl.ds(pl.multiple_of(t * tm + r * n_t2, n_t2), n_t2),
                          pl.ds(col(2, which, pair), PAIR_WIDTH)] for t in range(S // tm)]
        return jnp.concatenate(pieces, axis=0)

    @pl.when(tile == S // tm - 1)
    def _():
        def body(step, carry):
            work = [(step * per_step + u, pair) for u in range(per_step) for pair in range(N_PAIRS)]
            parts = [probabilities(2, pair, g2_piece(0, r, pair), g2_piece(1, r, pair), False)
                     for r, pair in work]
            for (r, pair), (p, m) in zip(work, parts):
                rows = pl.ds(r, L, stride=g2_d)
                acc, den, _ = merged(pair, rows, *weighted_values(p, m, g2_piece(2, r, pair)))
                o_ref[pair, rows, :] = acc / den
            return carry
        lax.fori_loop(0, g2_d // per_step, body, 0)


def _proj_attn(x, ada3, w_in, b_in, conv_w, w_pc):
    bsz, s, d = x.shape
    tm = ROW_TILE
    bias = jnp.asarray(_attn_bias_table())
    const = dict(pipeline_mode=pl.Buffered(1))
    row = lambda width: pl.BlockSpec((None, tm, width), lambda b, i: (b, i, 0))
    return pl.pallas_call(
        _proj_attn_kernel,
        grid=(bsz, s // tm),
        in_specs=[row(d),
                  pl.BlockSpec((None, 3, d), lambda b, i: (b, 0, 0)),
                  pl.BlockSpec((d, OFF_GA), lambda b, i: (0, 0), **const),
                  pl.BlockSpec((1, OFF_GA), lambda b, i: (0, 0), **const),
                  pl.BlockSpec((CONV_K, CONV_WIDTH), lambda b, i: (0, 0), **const),
                  pl.BlockSpec((CONV_WIDTH, d), lambda b, i: (0, 0), **const),
                  pl.BlockSpec(bias.shape, lambda b, i: (0, 0, 0, 0), **const)],
        out_specs=[row(GROUP_WIDTH), row(d),
                   pl.BlockSpec((None, N_PAIRS, s, PAIR_WIDTH), lambda b, i: (b, 0, 0, 0))],
        out_shape=[jax.ShapeDtypeStruct((bsz, s, GROUP_WIDTH), BF16),
                   jax.ShapeDtypeStruct((bsz, s, d), BF16),
                   jax.ShapeDtypeStruct((bsz, N_PAIRS, s, PAIR_WIDTH), F32)],
        scratch_shapes=[pltpu.VMEM((tm + HALO, CONV_WIDTH), F32),
                        pltpu.VMEM((tm, CONV_WIDTH), BF16),
                        pltpu.VMEM((N_STAGE_SLABS, tm, LANES), F32),
                        pltpu.VMEM((s, QKV_WIDTH), BF16),
                        pltpu.VMEM((3, N_PAIRS, s, PAIR_WIDTH), F32)],
        compiler_params=pltpu.CompilerParams(
            dimension_semantics=("arbitrary", "arbitrary"), vmem_limit_bytes=VMEM_LIMIT),
        name="proj_attn",
    )(x, ada3, w_in, b_in, conv_w, w_pc, bias)


def _out_kernel(x_ref, ada_ref, oa_ref, za_ref, yc_ref, wg_ref, bg_ref, wpa_ref, wout_ref, bout_ref,
                lng_ref, lnb_ref, out_ref, *, alpha):
    shift = ada_ref[0:1, :]
    scale = ada_ref[1:2, :]
    gate = ada_ref[2:3, :]

    def mix_gates(r0):
        h = (x_ref[r0:r0 + OUT_CHUNK, :] * (1.0 + scale) + shift).astype(BF16)
        return _sigmoid(jnp.dot(h, wg_ref[...], preferred_element_type=F32) + bg_ref[...]).astype(BF16)

    starts = list(range(0, x_ref.shape[0], OUT_CHUNK))
    g_next = mix_gates(starts[0])
    for idx, r0 in enumerate(starts):
        rows = slice(r0, r0 + OUT_CHUNK)
        g_mix = g_next
        o_attn = jnp.concatenate([oa_ref[pair, rows, :] for pair in range(N_PAIRS)], axis=1)
        ta = (o_attn * za_ref[rows, :].astype(F32)).astype(BF16)
        y_attn = jnp.dot(ta, wpa_ref[...], preferred_element_type=F32)
        if idx + 1 < len(starts):
            g_next = mix_gates(starts[idx + 1])
        merged = g_mix[:, :D_MODEL] * y_attn.astype(BF16) + g_mix[:, D_MODEL:] * yc_ref[rows, :]
        sub = gate * (jnp.dot(merged, wout_ref[...], preferred_element_type=F32) + bout_ref[...])
        r = alpha * x_ref[rows, :] + sub
        mu = jnp.mean(r, axis=-1, keepdims=True)
        cen = r - mu
        var = jnp.mean(cen * cen, axis=-1, keepdims=True)
        out_ref[rows, :] = cen * lax.rsqrt(var + LN_EPS) * lng_ref[...] + lnb_ref[...]


def _out(x, ada3, o_attn, za, yc, w_gate, b_gate, w_pa, w_out, b_out, ln_g, ln_b, alpha):
    bsz, s, d = x.shape
    tm = OUT_ROW_TILE
    const = dict(pipeline_mode=pl.Buffered(1))
    row = lambda width: pl.BlockSpec((None, tm, width), lambda b, i: (b, i, 0))
    vec = pl.BlockSpec((1, d), lambda b, i: (0, 0), **const)
    return pl.pallas_call(
        functools.partial(_out_kernel, alpha=alpha),
        grid=(bsz, s // tm),
        in_specs=[row(d), pl.BlockSpec((None, 3, d), lambda b, i: (b, 0, 0)),
                  pl.BlockSpec((None, N_PAIRS, tm, PAIR_WIDTH), lambda b, i: (b, 0, i, 0)),
                  row(GROUP_WIDTH), row(d),
                  pl.BlockSpec((d, 2 * d), lambda b, i: (0, 0), **const),
                  pl.BlockSpec((1, 2 * d), lambda b, i: (0, 0), **const),
                  pl.BlockSpec((GROUP_WIDTH, d), lambda b, i: (0, 0), **const),
                  pl.BlockSpec((d, d), lambda b, i: (0, 0), **const),
                  vec, vec, vec],
        out_specs=row(d),
        out_shape=jax.ShapeDtypeStruct((bsz, s, d), F32),
        compiler_params=pltpu.CompilerParams(
            dimension_semantics=("arbitrary", "arbitrary"), vmem_limit_bytes=VMEM_LIMIT),
        name="out",
    )(x, ada3, o_attn, za, yc, w_gate, b_gate, w_pa, w_out, b_out, ln_g, ln_b)


def kernel(x, c, w_ada, b_ada, w_in, b_in, conv_w, w_proj_attn, w_proj_conv, w_out, b_out, ln_g, ln_b):
    bsz, s, d = x.shape
    depth = w_in.shape[0]
    alpha = (2.0 * depth) ** 0.25
    for layer in range(depth):
        ada3 = _ada(c, w_ada[layer], b_ada[layer]).reshape(bsz, 3, d)
        b_in2 = b_in[layer].reshape(1, D_IN)
        za, yc, o_attn = _proj_attn(x, ada3, w_in[layer, :, :OFF_GA].astype(BF16), b_in2[:, :OFF_GA],
                                    conv_w[layer], w_proj_conv[layer].astype(BF16))
        x = _out(x, ada3, o_attn, za, yc,
                 w_in[layer, :, OFF_GA:].astype(BF16), b_in2[:, OFF_GA:],
                 w_proj_attn[layer].astype(BF16), w_out[layer].astype(BF16),
                 b_out[layer].reshape(1, d), ln_g[layer].reshape(1, d), ln_b[layer].reshape(1, d), alpha)
    return x
```

```python
import functools
import itertools

import numpy as np
import jax
import jax.numpy as jnp
from jax import lax
from jax.experimental import pallas as pl
from jax.experimental.pallas import tpu as pltpu

D_MODEL = 1024
HEAD_DIM = 64
HEADS_PER_GROUP = 4
DILATED_GROUPS = ((128, 1), (512, 4), (2048, 16))
N_GROUPS = len(DILATED_GROUPS)
N_ATTN_HEADS = N_GROUPS * HEADS_PER_GROUP
ATTN_WIDTH = N_ATTN_HEADS * HEAD_DIM
GROUP_WIDTH = HEADS_PER_GROUP * HEAD_DIM
QKV_WIDTH = 3 * ATTN_WIDTH
CONV_WIDTH = D_MODEL
CONV_K = 3
SUB_BLOCK = 128
ALIBI_MAX_EXP = 8.0
LN_EPS = 1e-5

OFF_Z_ATTN = QKV_WIDTH
OFF_UX = OFF_Z_ATTN + GROUP_WIDTH
OFF_GB = OFF_UX + CONV_WIDTH
OFF_GC = OFF_GB + CONV_WIDTH
OFF_ZC = OFF_GC + CONV_WIDTH
OFF_GA = OFF_ZC + CONV_WIDTH
D_IN = OFF_GA + 2 * D_MODEL

ROW_TILE = 512
OUT_ROW_TILE = 1024
OUT_CHUNK = 512
CONV_CHUNK = 256
LANES = 128
PAIR_WIDTH = 2 * HEAD_DIM
N_PAIRS = HEADS_PER_GROUP // 2
REGROUP_STRIDE = 4
RES_STRIDE = REGROUP_STRIDE
N_STAGE_SLABS = 6
HALO = 8
VMEM_LIMIT = 56 * 1024 * 1024

F32 = jnp.float32
BF16 = jnp.bfloat16

assert PAIR_WIDTH == LANES and ROW_TILE // DILATED_GROUPS[1][1] == SUB_BLOCK
assert DILATED_GROUPS[1][1] == RES_STRIDE and DILATED_GROUPS[2][1] == RES_STRIDE * RES_STRIDE
assert CONV_WIDTH // CONV_CHUNK == ROW_TILE // SUB_BLOCK


def _sigmoid(v):
    return 0.5 * jnp.tanh(0.5 * v) + 0.5


def _silu(v):
    return v * _sigmoid(v)


def _ada_kernel(c_ref, w_ref, b_ref, o_ref):
    o_ref[...] = jnp.dot(_silu(c_ref[...]), w_ref[...], preferred_element_type=F32,
                         precision=lax.Precision.HIGHEST) + b_ref[...]


def _ada(c, w_ada, b_ada):
    bsz, d = c.shape
    n_out = w_ada.shape[1]
    return pl.pallas_call(
        _ada_kernel,
        grid=(n_out // d,),
        in_specs=[pl.BlockSpec((bsz, d), lambda j: (0, 0)),
                  pl.BlockSpec((d, d), lambda j: (0, j)),
                  pl.BlockSpec((1, d), lambda j: (0, j))],
        out_specs=pl.BlockSpec((bsz, d), lambda j: (0, j)),
        out_shape=jax.ShapeDtypeStruct((bsz, n_out), F32),
        name="ada",
    )(c, w_ada, b_ada.reshape(1, n_out))


def _attn_bias_table():
    L = SUB_BLOCK
    tabs = []
    for group, (window, dilation) in enumerate(DILATED_GROUPS):
        span = window // dilation
        heads = np.arange(group * HEADS_PER_GROUP, (group + 1) * HEADS_PER_GROUP, dtype=np.float64)
        slopes = 2.0 ** (-ALIBI_MAX_EXP * (heads + 1.0) / N_ATTN_HEADS)
        delta = (np.arange(L)[:, None] + L - np.arange(2 * L)[None, :]).astype(np.float64)
        valid = (delta >= 0) & (delta <= span)
        bias = np.where(valid[None], -slopes[:, None, None] * (delta * dilation)[None], -np.inf)
        tabs.append(bias.reshape(HEADS_PER_GROUP // 2, 2 * L, 2 * L))
    return np.stack(tabs).astype(np.float32)


def _proj_attn_kernel(x_ref, ada_ref, w_ref, b_ref, cw_ref, wpc_ref, bias_ref,
                      za_ref, yc_ref, o_ref, u_scr, t_scr, stage_scr, qkv_scr, res, tmp_scr):
    tm = x_ref.shape[0]
    S = qkv_scr.shape[0]
    L = SUB_BLOCK
    tile = pl.program_id(1)
    first_tile = tile == 0
    base = pl.multiple_of(tile * tm, tm)
    prev_base = pl.multiple_of(jnp.maximum(tile - 1, 0) * tm, tm)

    @pl.when(first_tile)
    def _():
        u_scr[0:HALO, :] = jnp.zeros((HALO, CONV_WIDTH), F32)

    @pl.when(jnp.logical_not(first_tile))
    def _():
        u_scr[0:HALO, :] = u_scr[tm:tm + HALO, :]

    shift = ada_ref[0:1, :]
    scale = ada_ref[1:2, :]
    h = (x_ref[...] * (1.0 + scale) + shift).astype(BF16)

    def proj(lo, width):
        return (jnp.dot(h, w_ref[:, lo:lo + width], preferred_element_type=F32)
                + b_ref[:, lo:lo + width])

    slabs = itertools.cycle(range(N_STAGE_SLABS))

    def project_qkv(g):
        d = DILATED_GROUPS[g][1]
        for which in range(3):
            c0 = which * ATTN_WIDTH + g * GROUP_WIDTH
            blk = proj(c0, GROUP_WIDTH)
            if d == 1:
                qkv_scr[pl.ds(base, tm), c0:c0 + GROUP_WIDTH] = blk.astype(BF16)
                continue
            n_t = tm // d
            for lo in range(0, GROUP_WIDTH, LANES):
                cols = slice(c0 + lo, c0 + lo + LANES)
                slab = next(slabs)
                stage_scr[slab] = blk[:, lo:lo + LANES]
                if d == REGROUP_STRIDE:
                    for r in range(d):
                        qkv_scr[pl.ds(base + r * n_t, n_t), cols] = (
                            stage_scr[slab, pl.ds(r, n_t, stride=d), :].astype(BF16))
                else:
                    assert d == REGROUP_STRIDE * REGROUP_STRIDE
                    n_1 = tm // REGROUP_STRIDE
                    slab2 = next(slabs)
                    for r1 in range(REGROUP_STRIDE):
                        stage_scr[slab2, r1 * n_1:(r1 + 1) * n_1, :] = (
                            stage_scr[slab, pl.ds(r1, n_1, stride=REGROUP_STRIDE), :])
                    for r1 in range(REGROUP_STRIDE):
                        for r2 in range(REGROUP_STRIDE):
                            r = r2 * REGROUP_STRIDE + r1
                            qkv_scr[pl.ds(base + r * n_t, n_t), cols] = stage_scr[
                                slab2, pl.ds(r1 * n_1 + r2, n_t, stride=REGROUP_STRIDE), :].astype(BF16)

    project_qkv(1)
    project_qkv(0)

    low_half = lax.broadcasted_iota(jnp.int32, (L, PAIR_WIDTH), 1) < HEAD_DIM
    no_prev = jnp.logical_and(lax.broadcasted_iota(jnp.int32, (2 * L, 2 * L), 1) < L, first_tile)

    def col(g, which, pair):
        return which * ATTN_WIDTH + g * GROUP_WIDTH + pair * PAIR_WIDTH

    def probabilities(g, pair, q, k, maybe_no_prev):
        q = q * 0.125
        zero = jnp.zeros_like(q)
        q2 = jnp.concatenate([jnp.where(low_half, q, zero), jnp.where(low_half, zero, q)], axis=0)
        sc = lax.dot_general(q2, k, (((1,), (1,)), ((), ())), preferred_element_type=F32)
        sc = sc + bias_ref[g, pair, :, 2 * L - k.shape[0]:]
        if maybe_no_prev:
            sc = jnp.where(no_prev, -jnp.inf, sc)
        m = jnp.max(sc, axis=-1, keepdims=True)
        return jnp.exp(sc - m).astype(BF16), m

    def weighted_values(p, m, v):
        v1 = jnp.concatenate([v, jnp.ones(v.shape, BF16)], axis=1)
        pv = jnp.dot(p, v1, preferred_element_type=F32)
        mb = jnp.broadcast_to(m, (2 * L, PAIR_WIDTH))
        return (jnp.where(low_half, pv[:L, :PAIR_WIDTH], pv[L:, :PAIR_WIDTH]),
                jnp.where(low_half, pv[:L, PAIR_WIDTH:], pv[L:, PAIR_WIDTH:]),
                jnp.where(low_half, mb[:L], mb[L:]))

    def merged(pair, rows, acc, den, m):
        m_old = res[2, pair, rows, :]
        m_new = jnp.maximum(m_old, m)
        a = jnp.exp(m_old - m_new)
        b = jnp.exp(m - m_new)
        return a * res[0, pair, rows, :] + b * acc, a * res[1, pair, rows, :] + b * den, m_new

    g1_d = DILATED_GROUPS[1][1]

    def g1_scores(r):
        out = []
        for pair in range(N_PAIRS):
            cur = pl.ds(base + r * L, L)
            prev = pl.ds(prev_base + r * L, L)
            k = jnp.concatenate([qkv_scr[prev, pl.ds(col(1, 1, pair), PAIR_WIDTH)],
                                 qkv_scr[cur, pl.ds(col(1, 1, pair), PAIR_WIDTH)]], axis=0)
            out.append(probabilities(1, pair, qkv_scr[cur, pl.ds(col(1, 0, pair), PAIR_WIDTH)], k, True))
        return out

    def g1_finish(r, parts):
        for pair, (p, m) in enumerate(parts):
            cur = pl.ds(base + r * L, L)
            prev = pl.ds(prev_base + r * L, L)
            v = jnp.concatenate([qkv_scr[prev, pl.ds(col(1, 2, pair), PAIR_WIDTH)],
                                 qkv_scr[cur, pl.ds(col(1, 2, pair), PAIR_WIDTH)]], axis=0)
            acc, den, mm = weighted_values(p, m, v)
            rows = pl.ds(base + r * L, L)
            res[0, pair, rows, :] = acc
            res[1, pair, rows, :] = den
            res[2, pair, rows, :] = mm

    def g0_rows(c):
        q0 = base + c * L
        k0 = jnp.maximum(q0 - L, 0) if c == 0 else q0 - L
        return pl.ds(pl.multiple_of(q0, L), L), pl.ds(pl.multiple_of(k0, L), L)

    def g0_scores(c):
        cur, prev = g0_rows(c)
        out = []
        for pair in range(N_PAIRS):
            k = jnp.concatenate([qkv_scr[prev, pl.ds(col(0, 1, pair), PAIR_WIDTH)],
                                 qkv_scr[cur, pl.ds(col(0, 1, pair), PAIR_WIDTH)]], axis=0)
            out.append(probabilities(0, pair, qkv_scr[cur, pl.ds(col(0, 0, pair), PAIR_WIDTH)], k, c == 0))
        return out

    def g0_finish(c, parts):
        cur, prev = g0_rows(c)
        for pair, (p, m) in enumerate(parts):
            v = jnp.concatenate([qkv_scr[prev, pl.ds(col(0, 2, pair), PAIR_WIDTH)],
                                 qkv_scr[cur, pl.ds(col(0, 2, pair), PAIR_WIDTH)]], axis=0)
            slab = c * N_PAIRS + pair
            for k, val in enumerate(weighted_values(p, m, v)):
                tmp_scr[slab, k] = val
            n_r = L // RES_STRIDE
            for r in range(RES_STRIDE):
                rows = pl.ds(pl.multiple_of(base + r * (tm // RES_STRIDE) + c * n_r, n_r), n_r)
                acc, den, mm = merged(pair, rows, *[tmp_scr[slab, k, pl.ds(r, n_r, stride=RES_STRIDE), :]
                                                    for k in range(3)])
                res[0, pair, rows, :] = acc
                res[1, pair, rows, :] = den
                res[2, pair, rows, :] = mm

    items = ([(g1_scores, g1_finish, (2 * j, 2 * j + 1)) for j in range(2)]
             + [(g0_scores, g0_finish, (2 * j, 2 * j + 1)) for j in range(2)])

    def start(item):
        scores_fn, finish_fn, args = item
        return finish_fn, [(a, scores_fn(a)) for a in args]

    def finish(pending):
        finish_fn, parts = pending
        for a, part in parts:
            finish_fn(a, part)

    in_flight = []
    for ci, c0 in enumerate(range(0, CONV_WIDTH, CONV_CHUNK)):
        in_flight.append(start(items[ci]))
        if len(in_flight) > 2:
            finish(in_flight.pop(0))
        cs = slice(c0, c0 + CONV_CHUNK)
        u = proj(OFF_GC + c0, CONV_CHUNK) * proj(OFF_UX + c0, CONV_CHUNK)
        u_scr[HALO:HALO + tm, cs] = u
        conv = cw_ref[0:1, cs] * u_scr[HALO - 2:HALO - 2 + tm, cs]
        conv = conv + cw_ref[1:2, cs] * u_scr[HALO - 1:HALO - 1 + tm, cs]
        conv = conv + cw_ref[2:3, cs] * u
        t = proj(OFF_GB + c0, CONV_CHUNK) * conv * _silu(proj(OFF_ZC + c0, CONV_CHUNK))
        t_scr[:, cs] = t.astype(BF16)
    finish(in_flight.pop(0))
    za_ref[...] = _silu(proj(OFF_Z_ATTN, GROUP_WIDTH)).astype(BF16)
    yc_ref[...] = jnp.dot(t_scr[...], wpc_ref[...], preferred_element_type=F32).astype(BF16)
    finish(in_flight.pop(0))
    project_qkv(2)

    g2_d = DILATED_GROUPS[2][1]
    n_t2 = tm // g2_d

    def g2_piece(which, r, pair):
        pieces = [qkv_scr[pl.ds(pl.multiple_of(t * tm + r * n_t2, n_t2), n_t2),
                          pl.ds(col(2, which, pair), PAIR_WIDTH)] for t in range(S // tm)]
        return jnp.concatenate(pieces, axis=0)

    @pl.when(tile == S // tm - 1)
    def _():
        def body(step, carry):
            work = [(u, pair) for u in range(RES_STRIDE) for pair in range(N_PAIRS)]
            parts = [probabilities(2, pair, g2_piece(0, step * RES_STRIDE + u, pair),
                                   g2_piece(1, step * RES_STRIDE + u, pair), False) for u, pair in work]
            for (u, pair), (p, m) in zip(work, parts):
                r = step * RES_STRIDE + u
                new = weighted_values(p, m, g2_piece(2, r, pair))
                outs = []
                for t in range(S // tm):
                    start_row = t * tm + u * (tm // RES_STRIDE) + step
                    rows = pl.ds(start_row, n_t2, stride=RES_STRIDE)
                    acc, den, _ = merged(pair, rows, *[x[t * n_t2:(t + 1) * n_t2] for x in new])
                    outs.append(acc / den)
                o_ref[pair, pl.ds(r, L, stride=g2_d), :] = jnp.concatenate(outs, axis=0)
            return carry
        lax.fori_loop(0, g2_d // RES_STRIDE, body, 0)


def _proj_attn(x, ada3, w_in, b_in, conv_w, w_pc):
    bsz, s, d = x.shape
    tm = ROW_TILE
    bias = jnp.asarray(_attn_bias_table())
    const = dict(pipeline_mode=pl.Buffered(1))
    row = lambda width: pl.BlockSpec((None, tm, width), lambda b, i: (b, i, 0))
    return pl.pallas_call(
        _proj_attn_kernel,
        grid=(bsz, s // tm),
        in_specs=[row(d),
                  pl.BlockSpec((None, 3, d), lambda b, i: (b, 0, 0)),
                  pl.BlockSpec((d, OFF_GA), lambda b, i: (0, 0), **const),
                  pl.BlockSpec((1, OFF_GA), lambda b, i: (0, 0), **const),
                  pl.BlockSpec((CONV_K, CONV_WIDTH), lambda b, i: (0, 0), **const),
                  pl.BlockSpec((CONV_WIDTH, d), lambda b, i: (0, 0), **const),
                  pl.BlockSpec(bias.shape, lambda b, i: (0, 0, 0, 0), **const)],
        out_specs=[row(GROUP_WIDTH), row(d),
                   pl.BlockSpec((None, N_PAIRS, s, PAIR_WIDTH), lambda b, i: (b, 0, 0, 0))],
        out_shape=[jax.ShapeDtypeStruct((bsz, s, GROUP_WIDTH), BF16),
                   jax.ShapeDtypeStruct((bsz, s, d), BF16),
                   jax.ShapeDtypeStruct((bsz, N_PAIRS, s, PAIR_WIDTH), F32)],
        scratch_shapes=[pltpu.VMEM((tm + HALO, CONV_WIDTH), F32),
                        pltpu.VMEM((tm, CONV_WIDTH), BF16),
                        pltpu.VMEM((N_STAGE_SLABS, tm, LANES), F32),
                        pltpu.VMEM((s, QKV_WIDTH), BF16),
                        pltpu.VMEM((3, N_PAIRS, s, PAIR_WIDTH), F32),
                        pltpu.VMEM((N_PAIRS * (tm // SUB_BLOCK), 3, SUB_BLOCK, PAIR_WIDTH), F32)],
        compiler_params=pltpu.CompilerParams(
            dimension_semantics=("arbitrary", "arbitrary"), vmem_limit_bytes=VMEM_LIMIT),
        name="proj_attn",
    )(x, ada3, w_in, b_in, conv_w, w_pc, bias)


def _out_kernel(x_ref, ada_ref, oa_ref, za_ref, yc_ref, wg_ref, bg_ref, wpa_ref, wout_ref, bout_ref,
                lng_ref, lnb_ref, out_ref, *, alpha):
    shift = ada_ref[0:1, :]
    scale = ada_ref[1:2, :]
    gate = ada_ref[2:3, :]

    def mix_gates(r0):
        h = (x_ref[r0:r0 + OUT_CHUNK, :] * (1.0 + scale) + shift).astype(BF16)
        return _sigmoid(jnp.dot(h, wg_ref[...], preferred_element_type=F32) + bg_ref[...]).astype(BF16)

    starts = list(range(0, x_ref.shape[0], OUT_CHUNK))
    g_next = mix_gates(starts[0])
    for idx, r0 in enumerate(starts):
        rows = slice(r0, r0 + OUT_CHUNK)
        g_mix = g_next
        o_attn = jnp.concatenate([oa_ref[pair, rows, :] for pair in range(N_PAIRS)], axis=1)
        ta = (o_attn * za_ref[rows, :].astype(F32)).astype(BF16)
        y_attn = jnp.dot(ta, wpa_ref[...], preferred_element_type=F32)
        if idx + 1 < len(starts):
            g_next = mix_gates(starts[idx + 1])
        merged = g_mix[:, :D_MODEL] * y_attn.astype(BF16) + g_mix[:, D_MODEL:] * yc_ref[rows, :]
        sub = gate * (jnp.dot(merged, wout_ref[...], preferred_element_type=F32) + bout_ref[...])
        r = alpha * x_ref[rows, :] + sub
        mu = jnp.mean(r, axis=-1, keepdims=True)
        cen = r - mu
        var = jnp.mean(cen * cen, axis=-1, keepdims=True)
        out_ref[rows, :] = cen * lax.rsqrt(var + LN_EPS) * lng_ref[...] + lnb_ref[...]


def _out(x, ada3, o_attn, za, yc, w_gate, b_gate, w_pa, w_out, b_out, ln_g, ln_b, alpha):
    bsz, s, d = x.shape
    tm = OUT_ROW_TILE
    const = dict(pipeline_mode=pl.Buffered(1))
    row = lambda width: pl.BlockSpec((None, tm, width), lambda b, i: (b, i, 0))
    vec = pl.BlockSpec((1, d), lambda b, i: (0, 0), **const)
    return pl.pallas_call(
        functools.partial(_out_kernel, alpha=alpha),
        grid=(bsz, s // tm),
        in_specs=[row(d), pl.BlockSpec((None, 3, d), lambda b, i: (b, 0, 0)),
                  pl.BlockSpec((None, N_PAIRS, tm, PAIR_WIDTH), lambda b, i: (b, 0, i, 0)),
                  row(GROUP_WIDTH), row(d),
                  pl.BlockSpec((d, 2 * d), lambda b, i: (0, 0), **const),
                  pl.BlockSpec((1, 2 * d), lambda b, i: (0, 0), **const),
                  pl.BlockSpec((GROUP_WIDTH, d), lambda b, i: (0, 0), **const),
                  pl.BlockSpec((d, d), lambda b, i: (0, 0), **const),
                  vec, vec, vec],
        out_specs=row(d),
        out_shape=jax.ShapeDtypeStruct((bsz, s, d), F32),
        compiler_params=pltpu.CompilerParams(
            dimension_semantics=("arbitrary", "arbitrary"), vmem_limit_bytes=VMEM_LIMIT),
        name="out",
    )(x, ada3, o_attn, za, yc, w_gate, b_gate, w_pa, w_out, b_out, ln_g, ln_b)


def kernel(x, c, w_ada, b_ada, w_in, b_in, conv_w, w_proj_attn, w_proj_conv, w_out, b_out, ln_g, ln_b):
    bsz, s, d = x.shape
    depth = w_in.shape[0]
    alpha = (2.0 * depth) ** 0.25
    for layer in range(depth):
        ada3 = _ada(c, w_ada[layer], b_ada[layer]).reshape(bsz, 3, d)
        b_in2 = b_in[layer].reshape(1, D_IN)
        za, yc, o_attn = _proj_attn(x, ada3, w_in[layer, :, :OFF_GA].astype(BF16), b_in2[:, :OFF_GA],
                                    conv_w[layer], w_proj_conv[layer].astype(BF16))
        x = _out(x, ada3, o_attn, za, yc,
                 w_in[layer, :, OFF_GA:].astype(BF16), b_in2[:, OFF_GA:],
                 w_proj_attn[layer].astype(BF16), w_out[layer].astype(BF16),
                 b_out[layer].reshape(1, d), ln_g[layer].reshape(1, d), ln_b[layer].reshape(1, d), alpha)
    return x
```

```python
import functools
import itertools

import numpy as np
import jax
import jax.numpy as jnp
from jax import lax
from jax.experimental import pallas as pl
from jax.experimental.pallas import tpu as pltpu

D_MODEL = 1024
HEAD_DIM = 64
HEADS_PER_GROUP = 4
DILATED_GROUPS = ((128, 1), (512, 4), (2048, 16))
N_GROUPS = len(DILATED_GROUPS)
N_ATTN_HEADS = N_GROUPS * HEADS_PER_GROUP
ATTN_WIDTH = N_ATTN_HEADS * HEAD_DIM
GROUP_WIDTH = HEADS_PER_GROUP * HEAD_DIM
QKV_WIDTH = 3 * ATTN_WIDTH
CONV_WIDTH = D_MODEL
CONV_K = 3
SUB_BLOCK = 128
ALIBI_MAX_EXP = 8.0
LN_EPS = 1e-5

OFF_Z_ATTN = QKV_WIDTH
OFF_UX = OFF_Z_ATTN + GROUP_WIDTH
OFF_GB = OFF_UX + CONV_WIDTH
OFF_GC = OFF_GB + CONV_WIDTH
OFF_ZC = OFF_GC + CONV_WIDTH
OFF_GA = OFF_ZC + CONV_WIDTH
D_IN = OFF_GA + 2 * D_MODEL

ROW_TILE = 512
OUT_ROW_TILE = 1024
OUT_CHUNK = 512
CONV_CHUNK = 256
LANES = 128
PAIR_WIDTH = 2 * HEAD_DIM
N_PAIRS = HEADS_PER_GROUP // 2
REGROUP_STRIDE = 4
RES_STRIDE = REGROUP_STRIDE
N_STAGE_SLABS = 6
HALO = 8
VMEM_LIMIT = 56 * 1024 * 1024

F32 = jnp.float32
BF16 = jnp.bfloat16

assert PAIR_WIDTH == LANES and ROW_TILE // DILATED_GROUPS[1][1] == SUB_BLOCK
assert DILATED_GROUPS[1][1] == RES_STRIDE and DILATED_GROUPS[2][1] == RES_STRIDE * RES_STRIDE
assert CONV_WIDTH // CONV_CHUNK == ROW_TILE // SUB_BLOCK


def _sigmoid(v):
    return 0.5 * jnp.tanh(0.5 * v) + 0.5


def _silu(v):
    return v * _sigmoid(v)


def _ada_kernel(c_ref, w_ref, b_ref, o_ref):
    o_ref[...] = jnp.dot(_silu(c_ref[...]).astype(BF16), w_ref[...].astype(BF16),
                         preferred_element_type=F32) + b_ref[...]


def _ada(c, w_ada, b_ada):
    bsz, d = c.shape
    n_out = w_ada.shape[1]
    return pl.pallas_call(
        _ada_kernel,
        grid=(n_out // d,),
        in_specs=[pl.BlockSpec((bsz, d), lambda j: (0, 0)),
                  pl.BlockSpec((d, d), lambda j: (0, j)),
                  pl.BlockSpec((1, d), lambda j: (0, j))],
        out_specs=pl.BlockSpec((bsz, d), lambda j: (0, j)),
        out_shape=jax.ShapeDtypeStruct((bsz, n_out), F32),
        name="ada",
    )(c, w_ada, b_ada.reshape(1, n_out))


def _attn_bias_table():
    L = SUB_BLOCK
    tabs = []
    for group, (window, dilation) in enumerate(DILATED_GROUPS):
        span = window // dilation
        heads = np.arange(group * HEADS_PER_GROUP, (group + 1) * HEADS_PER_GROUP, dtype=np.float64)
        slopes = 2.0 ** (-ALIBI_MAX_EXP * (heads + 1.0) / N_ATTN_HEADS)
        delta = (np.arange(L)[:, None] + L - np.arange(2 * L)[None, :]).astype(np.float64)
        valid = (delta >= 0) & (delta <= span)
        bias = np.where(valid[None], -slopes[:, None, None] * (delta * dilation)[None], -np.inf)
        tabs.append(bias.reshape(HEADS_PER_GROUP // 2, 2 * L, 2 * L))
    return np.stack(tabs).astype(np.float32)


def _proj_attn_kernel(x_ref, ada_ref, w_ref, b_ref, cw_ref, wpc_ref, bias_ref,
                      za_ref, yc_ref, o_ref, u_scr, t_scr, stage_scr, qkv_scr, res, tmp_scr, kt_scr):
    tm = x_ref.shape[0]
    S = qkv_scr.shape[0]
    L = SUB_BLOCK
    tile = pl.program_id(1)
    first_tile = tile == 0
    base = pl.multiple_of(tile * tm, tm)
    prev_base = pl.multiple_of(jnp.maximum(tile - 1, 0) * tm, tm)

    @pl.when(first_tile)
    def _():
        u_scr[0:HALO, :] = jnp.zeros((HALO, CONV_WIDTH), F32)

    @pl.when(jnp.logical_not(first_tile))
    def _():
        u_scr[0:HALO, :] = u_scr[tm:tm + HALO, :]

    shift = ada_ref[0:1, :]
    scale = ada_ref[1:2, :]
    h = (x_ref[...] * (1.0 + scale) + shift).astype(BF16)

    def proj(lo, width):
        return (jnp.dot(h, w_ref[:, lo:lo + width], preferred_element_type=F32)
                + b_ref[:, lo:lo + width])

    slabs = itertools.cycle(range(N_STAGE_SLABS))

    def project_qkv(g):
        d = DILATED_GROUPS[g][1]
        for which in range(3):
            c0 = which * ATTN_WIDTH + g * GROUP_WIDTH
            blk = proj(c0, GROUP_WIDTH)
            if d == 1:
                qkv_scr[pl.ds(base, tm), c0:c0 + GROUP_WIDTH] = blk.astype(BF16)
                continue
            n_t = tm // d
            for lo in range(0, GROUP_WIDTH, LANES):
                cols = slice(c0 + lo, c0 + lo + LANES)
                slab = next(slabs)
                stage_scr[slab] = blk[:, lo:lo + LANES]
                if d == REGROUP_STRIDE:
                    for r in range(d):
                        qkv_scr[pl.ds(base + r * n_t, n_t), cols] = (
                            stage_scr[slab, pl.ds(r, n_t, stride=d), :].astype(BF16))
                else:
                    assert d == REGROUP_STRIDE * REGROUP_STRIDE
                    n_1 = tm // REGROUP_STRIDE
                    slab2 = next(slabs)
                    for r1 in range(REGROUP_STRIDE):
                        stage_scr[slab2, r1 * n_1:(r1 + 1) * n_1, :] = (
                            stage_scr[slab, pl.ds(r1, n_1, stride=REGROUP_STRIDE), :])
                    for r1 in range(REGROUP_STRIDE):
                        for r2 in range(REGROUP_STRIDE):
                            r = r2 * REGROUP_STRIDE + r1
                            qkv_scr[pl.ds(base + r * n_t, n_t), cols] = stage_scr[
                                slab2, pl.ds(r1 * n_1 + r2, n_t, stride=REGROUP_STRIDE), :].astype(BF16)

    project_qkv(1)
    project_qkv(0)

    low_half = lax.broadcasted_iota(jnp.int32, (L, PAIR_WIDTH), 1) < HEAD_DIM
    no_prev = jnp.logical_and(lax.broadcasted_iota(jnp.int32, (2 * L, 2 * L), 1) < L, first_tile)

    def col(g, which, pair):
        return which * ATTN_WIDTH + g * GROUP_WIDTH + pair * PAIR_WIDTH

    kt_slots = itertools.cycle(range(kt_scr.shape[0]))

    def probabilities(g, pair, q, k, maybe_no_prev):
        q = q * 0.125
        zero = jnp.zeros_like(q)
        q2 = jnp.concatenate([jnp.where(low_half, q, zero), jnp.where(low_half, zero, q)], axis=0)
        slot = next(kt_slots)
        n_keys = k.shape[0]
        kt_scr[slot, :, :n_keys] = k.T
        sc = jnp.dot(q2, kt_scr[slot, :, :n_keys], preferred_element_type=F32)
        sc = sc + bias_ref[g, pair, :, 2 * L - k.shape[0]:]
        if maybe_no_prev:
            sc = jnp.where(no_prev, -jnp.inf, sc)
        m = jnp.max(sc, axis=-1, keepdims=True)
        return jnp.exp(sc - m).astype(BF16), m

    def weighted_values(p, m, v):
        v1 = jnp.concatenate([v, jnp.ones(v.shape, BF16)], axis=1)
        pv = jnp.dot(p, v1, preferred_element_type=F32)
        mb = jnp.broadcast_to(m, (2 * L, PAIR_WIDTH))
        return (jnp.where(low_half, pv[:L, :PAIR_WIDTH], pv[L:, :PAIR_WIDTH]),
                jnp.where(low_half, pv[:L, PAIR_WIDTH:], pv[L:, PAIR_WIDTH:]),
                jnp.where(low_half, mb[:L], mb[L:]))

    def merged(pair, rows, acc, den, m):
        m_old = res[2, pair, rows, :]
        m_new = jnp.maximum(m_old, m)
        a = jnp.exp(m_old - m_new)
        b = jnp.exp(m - m_new)
        return a * res[0, pair, rows, :] + b * acc, a * res[1, pair, rows, :] + b * den, m_new

    g1_d = DILATED_GROUPS[1][1]

    def g1_scores(r):
        out = []
        for pair in range(N_PAIRS):
            cur = pl.ds(base + r * L, L)
            prev = pl.ds(prev_base + r * L, L)
            k = jnp.concatenate([qkv_scr[prev, pl.ds(col(1, 1, pair), PAIR_WIDTH)],
                                 qkv_scr[cur, pl.ds(col(1, 1, pair), PAIR_WIDTH)]], axis=0)
            out.append(probabilities(1, pair, qkv_scr[cur, pl.ds(col(1, 0, pair), PAIR_WIDTH)], k, True))
        return out

    def g1_finish(r, parts):
        for pair, (p, m) in enumerate(parts):
            cur = pl.ds(base + r * L, L)
            prev = pl.ds(prev_base + r * L, L)
            v = jnp.concatenate([qkv_scr[prev, pl.ds(col(1, 2, pair), PAIR_WIDTH)],
                                 qkv_scr[cur, pl.ds(col(1, 2, pair), PAIR_WIDTH)]], axis=0)
            acc, den, mm = weighted_values(p, m, v)
            rows = pl.ds(base + r * L, L)
            res[0, pair, rows, :] = acc
            res[1, pair, rows, :] = den
            res[2, pair, rows, :] = mm

    def g0_rows(c):
        q0 = base + c * L
        k0 = jnp.maximum(q0 - L, 0) if c == 0 else q0 - L
        return pl.ds(pl.multiple_of(q0, L), L), pl.ds(pl.multiple_of(k0, L), L)

    def g0_scores(c):
        cur, prev = g0_rows(c)
        out = []
        for pair in range(N_PAIRS):
            k = jnp.concatenate([qkv_scr[prev, pl.ds(col(0, 1, pair), PAIR_WIDTH)],
                                 qkv_scr[cur, pl.ds(col(0, 1, pair), PAIR_WIDTH)]], axis=0)
            out.append(probabilities(0, pair, qkv_scr[cur, pl.ds(col(0, 0, pair), PAIR_WIDTH)], k, c == 0))
        return out

    def g0_finish(c, parts):
        cur, prev = g0_rows(c)
        for pair, (p, m) in enumerate(parts):
            v = jnp.concatenate([qkv_scr[prev, pl.ds(col(0, 2, pair), PAIR_WIDTH)],
                                 qkv_scr[cur, pl.ds(col(0, 2, pair), PAIR_WIDTH)]], axis=0)
            slab = c * N_PAIRS + pair
            for k, val in enumerate(weighted_values(p, m, v)):
                tmp_scr[slab, k] = val
            n_r = L // RES_STRIDE
            for r in range(RES_STRIDE):
                rows = pl.ds(pl.multiple_of(base + r * (tm // RES_STRIDE) + c * n_r, n_r), n_r)
                acc, den, mm = merged(pair, rows, *[tmp_scr[slab, k, pl.ds(r, n_r, stride=RES_STRIDE), :]
                                                    for k in range(3)])
                res[0, pair, rows, :] = acc
                res[1, pair, rows, :] = den
                res[2, pair, rows, :] = mm

    items = ([(g1_scores, g1_finish, (2 * j, 2 * j + 1)) for j in range(2)]
             + [(g0_scores, g0_finish, (2 * j, 2 * j + 1)) for j in range(2)])

    def start(item):
        scores_fn, finish_fn, args = item
        return finish_fn, [(a, scores_fn(a)) for a in args]

    def finish(pending):
        finish_fn, parts = pending
        for a, part in parts:
            finish_fn(a, part)

    in_flight = []
    for ci, c0 in enumerate(range(0, CONV_WIDTH, CONV_CHUNK)):
        in_flight.append(start(items[ci]))
        if len(in_flight) > 2:
            finish(in_flight.pop(0))
        cs = slice(c0, c0 + CONV_CHUNK)
        u = proj(OFF_GC + c0, CONV_CHUNK) * proj(OFF_UX + c0, CONV_CHUNK)
        u_scr[HALO:HALO + tm, cs] = u
        conv = cw_ref[0:1, cs] * u_scr[HALO - 2:HALO - 2 + tm, cs]
        conv = conv + cw_ref[1:2, cs] * u_scr[HALO - 1:HALO - 1 + tm, cs]
        conv = conv + cw_ref[2:3, cs] * u
        t = proj(OFF_GB + c0, CONV_CHUNK) * conv * _silu(proj(OFF_ZC + c0, CONV_CHUNK))
        t_scr[:, cs] = t.astype(BF16)
    finish(in_flight.pop(0))
    za_ref[...] = _silu(proj(OFF_Z_ATTN, GROUP_WIDTH)).astype(BF16)
    yc_ref[...] = jnp.dot(t_scr[...], wpc_ref[...], preferred_element_type=F32).astype(BF16)
    finish(in_flight.pop(0))
    project_qkv(2)

    g2_d = DILATED_GROUPS[2][1]
    n_t2 = tm // g2_d

    def g2_piece(which, r, pair):
        pieces = [qkv_scr[pl.ds(pl.multiple_of(t * tm + r * n_t2, n_t2), n_t2),
                          pl.ds(col(2, which, pair), PAIR_WIDTH)] for t in range(S // tm)]
        return jnp.concatenate(pieces, axis=0)

    @pl.when(tile == S // tm - 1)
    def _():
        def body(step, carry):
            work = [(u, pair) for u in range(RES_STRIDE) for pair in range(N_PAIRS)]
            parts = [probabilities(2, pair, g2_piece(0, step * RES_STRIDE + u, pair),
                                   g2_piece(1, step * RES_STRIDE + u, pair), False) for u, pair in work]
            for (u, pair), (p, m) in zip(work, parts):
                r = step * RES_STRIDE + u
                new = weighted_values(p, m, g2_piece(2, r, pair))
                outs = []
                for t in range(S // tm):
                    start_row = t * tm + u * (tm // RES_STRIDE) + step
                    rows = pl.ds(start_row, n_t2, stride=RES_STRIDE)
                    acc, den, _ = merged(pair, rows, *[x[t * n_t2:(t + 1) * n_t2] for x in new])
                    outs.append(acc / den)
                o_ref[pair, pl.ds(r, L, stride=g2_d), :] = jnp.concatenate(outs, axis=0)
            return carry
        lax.fori_loop(0, g2_d // RES_STRIDE, body, 0)


def _proj_attn(x, ada3, w_in, b_in, conv_w, w_pc):
    bsz, s, d = x.shape
    tm = ROW_TILE
    bias = jnp.asarray(_attn_bias_table())
    const = dict(pipeline_mode=pl.Buffered(1))
    row = lambda width: pl.BlockSpec((None, tm, width), lambda b, i: (b, i, 0))
    return pl.pallas_call(
        _proj_attn_kernel,
        grid=(bsz, s // tm),
        in_specs=[row(d),
                  pl.BlockSpec((None, 3, d), lambda b, i: (b, 0, 0)),
                  pl.BlockSpec((d, OFF_GA), lambda b, i: (0, 0), **const),
                  pl.BlockSpec((1, OFF_GA), lambda b, i: (0, 0), **const),
                  pl.BlockSpec((CONV_K, CONV_WIDTH), lambda b, i: (0, 0), **const),
                  pl.BlockSpec((CONV_WIDTH, d), lambda b, i: (0, 0), **const),
                  pl.BlockSpec(bias.shape, lambda b, i: (0, 0, 0, 0), **const)],
        out_specs=[row(GROUP_WIDTH), row(d),
                   pl.BlockSpec((None, N_PAIRS, s, PAIR_WIDTH), lambda b, i: (b, 0, 0, 0))],
        out_shape=[jax.ShapeDtypeStruct((bsz, s, GROUP_WIDTH), BF16),
                   jax.ShapeDtypeStruct((bsz, s, d), BF16),
                   jax.ShapeDtypeStruct((bsz, N_PAIRS, s, PAIR_WIDTH), F32)],
        scratch_shapes=[pltpu.VMEM((tm + HALO, CONV_WIDTH), F32),
                        pltpu.VMEM((tm, CONV_WIDTH), BF16),
                        pltpu.VMEM((N_STAGE_SLABS, tm, LANES), F32),
                        pltpu.VMEM((s, QKV_WIDTH), BF16),
                        pltpu.VMEM((3, N_PAIRS, s, PAIR_WIDTH), F32),
                        pltpu.VMEM((N_PAIRS * (tm // SUB_BLOCK), 3, SUB_BLOCK, PAIR_WIDTH), F32),
                        pltpu.VMEM((4, PAIR_WIDTH, 2 * SUB_BLOCK), BF16)],
        compiler_params=pltpu.CompilerParams(
            dimension_semantics=("arbitrary", "arbitrary"), vmem_limit_bytes=VMEM_LIMIT),
        name="proj_attn",
    )(x, ada3, w_in, b_in, conv_w, w_pc, bias)


def _out_kernel(x_ref, ada_ref, oa_ref, za_ref, yc_ref, wg_ref, bg_ref, wpa_ref, wout_ref, bout_ref,
                lng_ref, lnb_ref, out_ref, *, alpha):
    shift = ada_ref[0:1, :]
    scale = ada_ref[1:2, :]
    gate = ada_ref[2:3, :]

    def mix_gates(r0):
        h = (x_ref[r0:r0 + OUT_CHUNK, :] * (1.0 + scale) + shift).astype(BF16)
        return _sigmoid(jnp.dot(h, wg_ref[...], preferred_element_type=F32) + bg_ref[...]).astype(BF16)

    starts = list(range(0, x_ref.shape[0], OUT_CHUNK))
    g_next = mix_gates(starts[0])
    for idx, r0 in enumerate(starts):
        rows = slice(r0, r0 + OUT_CHUNK)
        g_mix = g_next
        o_attn = jnp.concatenate([oa_ref[pair, rows, :] for pair in range(N_PAIRS)], axis=1)
        ta = (o_attn * za_ref[rows, :].astype(F32)).astype(BF16)
        y_attn = jnp.dot(ta, wpa_ref[...], preferred_element_type=F32)
        if idx + 1 < len(starts):
            g_next = mix_gates(starts[idx + 1])
        merged = g_mix[:, :D_MODEL] * y_attn.astype(BF16) + g_mix[:, D_MODEL:] * yc_ref[rows, :]
        sub = gate * (jnp.dot(merged, wout_ref[...], preferred_element_type=F32) + bout_ref[...])
        r = alpha * x_ref[rows, :] + sub
        mu = jnp.mean(r, axis=-1, keepdims=True)
        cen = r - mu
        var = jnp.mean(cen * cen, axis=-1, keepdims=True)
        out_ref[rows, :] = cen * lax.rsqrt(var + LN_EPS) * lng_ref[...] + lnb_ref[...]


def _out(x, ada3, o_attn, za, yc, w_gate, b_gate, w_pa, w_out, b_out, ln_g, ln_b, alpha):
    bsz, s, d = x.shape
    tm = OUT_ROW_TILE
    const = dict(pipeline_mode=pl.Buffered(1))
    row = lambda width: pl.BlockSpec((None, tm, width), lambda b, i: (b, i, 0))
    vec = pl.BlockSpec((1, d), lambda b, i: (0, 0), **const)
    return pl.pallas_call(
        functools.partial(_out_kernel, alpha=alpha),
        grid=(bsz, s // tm),
        in_specs=[row(d), pl.BlockSpec((None, 3, d), lambda b, i: (b, 0, 0)),
                  pl.BlockSpec((None, N_PAIRS, tm, PAIR_WIDTH), lambda b, i: (b, 0, i, 0)),
                  row(GROUP_WIDTH), row(d),
                  pl.BlockSpec((d, 2 * d), lambda b, i: (0, 0), **const),
                  pl.BlockSpec((1, 2 * d), lambda b, i: (0, 0), **const),
                  pl.BlockSpec((GROUP_WIDTH, d), lambda b, i: (0, 0), **const),
                  pl.BlockSpec((d, d), lambda b, i: (0, 0), **const),
                  vec, vec, vec],
        out_specs=row(d),
        out_shape=jax.ShapeDtypeStruct((bsz, s, d), F32),
        compiler_params=pltpu.CompilerParams(
            dimension_semantics=("arbitrary", "arbitrary"), vmem_limit_bytes=VMEM_LIMIT),
        name="out",
    )(x, ada3, o_attn, za, yc, w_gate, b_gate, w_pa, w_out, b_out, ln_g, ln_b)


def kernel(x, c, w_ada, b_ada, w_in, b_in, conv_w, w_proj_attn, w_proj_conv, w_out, b_out, ln_g, ln_b):
    bsz, s, d = x.shape
    depth = w_in.shape[0]
    alpha = (2.0 * depth) ** 0.25
    for layer in range(depth):
        ada3 = _ada(c, w_ada[layer], b_ada[layer]).reshape(bsz, 3, d)
        b_in2 = b_in[layer].reshape(1, D_IN)
        za, yc, o_attn = _proj_attn(x, ada3, w_in[layer, :, :OFF_GA].astype(BF16), b_in2[:, :OFF_GA],
                                    conv_w[layer], w_proj_conv[layer].astype(BF16))
        x = _out(x, ada3, o_attn, za, yc,
                 w_in[layer, :, OFF_GA:].astype(BF16), b_in2[:, OFF_GA:],
                 w_proj_attn[layer].astype(BF16), w_out[layer].astype(BF16),
                 b_out[layer].reshape(1, d), ln_g[layer].reshape(1, d), ln_b[layer].reshape(1, d), alpha)
    return x
```

```python
import functools
import itertools

import numpy as np
import jax
import jax.numpy as jnp
from jax import lax
from jax.experimental import pallas as pl
from jax.experimental.pallas import tpu as pltpu

D_MODEL = 1024
HEAD_DIM = 64
HEADS_PER_GROUP = 4
DILATED_GROUPS = ((128, 1), (512, 4), (2048, 16))
N_GROUPS = len(DILATED_GROUPS)
N_ATTN_HEADS = N_GROUPS * HEADS_PER_GROUP
ATTN_WIDTH = N_ATTN_HEADS * HEAD_DIM
GROUP_WIDTH = HEADS_PER_GROUP * HEAD_DIM
QKV_WIDTH = 3 * ATTN_WIDTH
CONV_WIDTH = D_MODEL
CONV_K = 3
SUB_BLOCK = 128
ALIBI_MAX_EXP = 8.0
LN_EPS = 1e-5

OFF_Z_ATTN = QKV_WIDTH
OFF_UX = OFF_Z_ATTN + GROUP_WIDTH
OFF_GB = OFF_UX + CONV_WIDTH
OFF_GC = OFF_GB + CONV_WIDTH
OFF_ZC = OFF_GC + CONV_WIDTH
OFF_GA = OFF_ZC + CONV_WIDTH
D_IN = OFF_GA + 2 * D_MODEL

ROW_TILE = 512
OUT_ROW_TILE = 1024
OUT_CHUNK = 256
OUT_COL_CHUNK = 256
CONV_CHUNK = 256
LANES = 128
PAIR_WIDTH = 2 * HEAD_DIM
N_PAIRS = HEADS_PER_GROUP // 2
REGROUP_STRIDE = 4
RES_STRIDE = REGROUP_STRIDE
N_STAGE_SLABS = 6
HALO = 8
VMEM_LIMIT = 56 * 1024 * 1024

F32 = jnp.float32
BF16 = jnp.bfloat16

assert PAIR_WIDTH == LANES and ROW_TILE // DILATED_GROUPS[1][1] == SUB_BLOCK
assert DILATED_GROUPS[1][1] == RES_STRIDE and DILATED_GROUPS[2][1] == RES_STRIDE * RES_STRIDE
assert 2 * (CONV_WIDTH // CONV_CHUNK) == 2 * (ROW_TILE // SUB_BLOCK)


def _sigmoid(v):
    return 0.5 * jnp.tanh(0.5 * v) + 0.5


def _silu(v):
    return v * _sigmoid(v)


def _ada_kernel(c_ref, w_ref, b_ref, o_ref):
    o_ref[...] = jnp.dot(_silu(c_ref[...]).astype(BF16), w_ref[...].astype(BF16),
                         preferred_element_type=F32) + b_ref[...]


def _ada(c, w_ada, b_ada):
    bsz, d = c.shape
    n_out = w_ada.shape[1]
    return pl.pallas_call(
        _ada_kernel,
        grid=(n_out // d,),
        in_specs=[pl.BlockSpec((bsz, d), lambda j: (0, 0)),
                  pl.BlockSpec((d, d), lambda j: (0, j)),
                  pl.BlockSpec((1, d), lambda j: (0, j))],
        out_specs=pl.BlockSpec((bsz, d), lambda j: (0, j)),
        out_shape=jax.ShapeDtypeStruct((bsz, n_out), F32),
        name="ada",
    )(c, w_ada, b_ada.reshape(1, n_out))


def _attn_bias_table():
    L = SUB_BLOCK
    tabs = []
    for group, (window, dilation) in enumerate(DILATED_GROUPS):
        span = window // dilation
        heads = np.arange(group * HEADS_PER_GROUP, (group + 1) * HEADS_PER_GROUP, dtype=np.float64)
        slopes = 2.0 ** (-ALIBI_MAX_EXP * (heads + 1.0) / N_ATTN_HEADS)
        delta = (np.arange(L)[:, None] + L - np.arange(2 * L)[None, :]).astype(np.float64)
        valid = (delta >= 0) & (delta <= span)
        bias = np.where(valid[None], -slopes[:, None, None] * (delta * dilation)[None], -np.inf)
        tabs.append(bias.reshape(HEADS_PER_GROUP // 2, 2 * L, 2 * L))
    return np.stack(tabs).astype(np.float32)


def _proj_attn_kernel(x_ref, ada_ref, w_ref, b_ref, cw_ref, wpc_ref, bias_ref,
                      za_ref, yc_ref, o_ref, u_scr, t_scr, stage_scr, qkv_scr, res, tmp_scr, kt_scr):
    tm = x_ref.shape[0]
    S = qkv_scr.shape[0]
    L = SUB_BLOCK
    tile = pl.program_id(1)
    first_tile = tile == 0
    base = pl.multiple_of(tile * tm, tm)
    prev_base = pl.multiple_of(jnp.maximum(tile - 1, 0) * tm, tm)

    @pl.when(first_tile)
    def _():
        u_scr[0:HALO, :] = jnp.zeros((HALO, CONV_WIDTH), F32)

    @pl.when(jnp.logical_not(first_tile))
    def _():
        u_scr[0:HALO, :] = u_scr[tm:tm + HALO, :]

    shift = ada_ref[0:1, :]
    scale = ada_ref[1:2, :]
    h = (x_ref[...] * (1.0 + scale) + shift).astype(BF16)

    def proj(lo, width):
        return (jnp.dot(h, w_ref[:, lo:lo + width], preferred_element_type=F32)
                + b_ref[:, lo:lo + width])

    slabs = itertools.cycle(range(N_STAGE_SLABS))

    def project_qkv(g):
        d = DILATED_GROUPS[g][1]
        for which in range(3):
            c0 = which * ATTN_WIDTH + g * GROUP_WIDTH
            blk = proj(c0, GROUP_WIDTH)
            if d == 1:
                qkv_scr[pl.ds(base, tm), c0:c0 + GROUP_WIDTH] = blk.astype(BF16)
                continue
            n_t = tm // d
            for lo in range(0, GROUP_WIDTH, LANES):
                cols = slice(c0 + lo, c0 + lo + LANES)
                slab = next(slabs)
                stage_scr[slab] = blk[:, lo:lo + LANES]
                if d == REGROUP_STRIDE:
                    for r in range(d):
                        qkv_scr[pl.ds(base + r * n_t, n_t), cols] = (
                            stage_scr[slab, pl.ds(r, n_t, stride=d), :].astype(BF16))
                else:
                    assert d == REGROUP_STRIDE * REGROUP_STRIDE
                    n_1 = tm // REGROUP_STRIDE
                    slab2 = next(slabs)
                    for r1 in range(REGROUP_STRIDE):
                        stage_scr[slab2, r1 * n_1:(r1 + 1) * n_1, :] = (
                            stage_scr[slab, pl.ds(r1, n_1, stride=REGROUP_STRIDE), :])
                    for r1 in range(REGROUP_STRIDE):
                        for r2 in range(REGROUP_STRIDE):
                            r = r2 * REGROUP_STRIDE + r1
                            qkv_scr[pl.ds(base + r * n_t, n_t), cols] = stage_scr[
                                slab2, pl.ds(r1 * n_1 + r2, n_t, stride=REGROUP_STRIDE), :].astype(BF16)

    project_qkv(1)
    project_qkv(0)

    low_half = lax.broadcasted_iota(jnp.int32, (L, PAIR_WIDTH), 1) < HEAD_DIM
    no_prev = jnp.logical_and(lax.broadcasted_iota(jnp.int32, (2 * L, 2 * L), 1) < L, first_tile)

    def col(g, which, pair):
        return which * ATTN_WIDTH + g * GROUP_WIDTH + pair * PAIR_WIDTH

    kt_slots = itertools.cycle(range(kt_scr.shape[0]))

    def probabilities(g, pair, q, k, maybe_no_prev):
        q = q * 0.125
        zero = jnp.zeros_like(q)
        q2 = jnp.concatenate([jnp.where(low_half, q, zero), jnp.where(low_half, zero, q)], axis=0)
        slot = next(kt_slots)
        n_keys = k.shape[0]
        kt_scr[slot, :, :n_keys] = k.T
        sc = jnp.dot(q2, kt_scr[slot, :, :n_keys], preferred_element_type=F32)
        sc = sc + bias_ref[g, pair, :, 2 * L - k.shape[0]:]
        if maybe_no_prev:
            sc = jnp.where(no_prev, -jnp.inf, sc)
        m = jnp.max(sc, axis=-1, keepdims=True)
        return jnp.exp(sc - m).astype(BF16), m

    def weighted_values(p, m, v):
        v1 = jnp.concatenate([v, jnp.ones(v.shape, BF16)], axis=1)
        pv = jnp.dot(p, v1, preferred_element_type=F32)
        mb = jnp.broadcast_to(m, (2 * L, PAIR_WIDTH))
        return (jnp.where(low_half, pv[:L, :PAIR_WIDTH], pv[L:, :PAIR_WIDTH]),
                jnp.where(low_half, pv[:L, PAIR_WIDTH:], pv[L:, PAIR_WIDTH:]),
                jnp.where(low_half, mb[:L], mb[L:]))

    def merged(pair, rows, acc, den, m):
        m_old = res[2, pair, rows, :]
        m_new = jnp.maximum(m_old, m)
        a = jnp.exp(m_old - m_new)
        b = jnp.exp(m - m_new)
        return a * res[0, pair, rows, :] + b * acc, a * res[1, pair, rows, :] + b * den, m_new

    g1_d = DILATED_GROUPS[1][1]

    def g1_scores(r):
        out = []
        for pair in range(N_PAIRS):
            cur = pl.ds(base + r * L, L)
            prev = pl.ds(prev_base + r * L, L)
            k = jnp.concatenate([qkv_scr[prev, pl.ds(col(1, 1, pair), PAIR_WIDTH)],
                                 qkv_scr[cur, pl.ds(col(1, 1, pair), PAIR_WIDTH)]], axis=0)
            out.append(probabilities(1, pair, qkv_scr[cur, pl.ds(col(1, 0, pair), PAIR_WIDTH)], k, True))
        return out

    def g1_finish(r, parts):
        for pair, (p, m) in enumerate(parts):
            cur = pl.ds(base + r * L, L)
            prev = pl.ds(prev_base + r * L, L)
            v = jnp.concatenate([qkv_scr[prev, pl.ds(col(1, 2, pair), PAIR_WIDTH)],
                                 qkv_scr[cur, pl.ds(col(1, 2, pair), PAIR_WIDTH)]], axis=0)
            acc, den, mm = weighted_values(p, m, v)
            rows = pl.ds(base + r * L, L)
            res[0, pair, rows, :] = acc
            res[1, pair, rows, :] = den
            res[2, pair, rows, :] = mm

    def g0_rows(c):
        q0 = base + c * L
        k0 = jnp.maximum(q0 - L, 0) if c == 0 else q0 - L
        return pl.ds(pl.multiple_of(q0, L), L), pl.ds(pl.multiple_of(k0, L), L)

    def g0_scores(c):
        cur, prev = g0_rows(c)
        out = []
        for pair in range(N_PAIRS):
            k = jnp.concatenate([qkv_scr[prev, pl.ds(col(0, 1, pair), PAIR_WIDTH)],
                                 qkv_scr[cur, pl.ds(col(0, 1, pair), PAIR_WIDTH)]], axis=0)
            out.append(probabilities(0, pair, qkv_scr[cur, pl.ds(col(0, 0, pair), PAIR_WIDTH)], k, c == 0))
        return out

    def g0_finish(c, parts):
        cur, prev = g0_rows(c)
        for pair, (p, m) in enumerate(parts):
            v = jnp.concatenate([qkv_scr[prev, pl.ds(col(0, 2, pair), PAIR_WIDTH)],
                                 qkv_scr[cur, pl.ds(col(0, 2, pair), PAIR_WIDTH)]], axis=0)
            slab = c * N_PAIRS + pair
            for k, val in enumerate(weighted_values(p, m, v)):
                tmp_scr[slab, k] = val
            n_r = L // RES_STRIDE
            for r in range(RES_STRIDE):
                rows = pl.ds(pl.multiple_of(base + r * (tm // RES_STRIDE) + c * n_r, n_r), n_r)
                acc, den, mm = merged(pair, rows, *[tmp_scr[slab, k, pl.ds(r, n_r, stride=RES_STRIDE), :]
                                                    for k in range(3)])
                res[0, pair, rows, :] = acc
                res[1, pair, rows, :] = den
                res[2, pair, rows, :] = mm

    items = ([(g1_scores, g1_finish, (j,)) for j in range(tm // L)]
             + [(g0_scores, g0_finish, (j,)) for j in range(tm // L)])

    def start(item):
        scores_fn, finish_fn, args = item
        return finish_fn, [(a, scores_fn(a)) for a in args]

    def finish(pending):
        finish_fn, parts = pending
        for a, part in parts:
            finish_fn(a, part)

    in_flight = []
    pending_items = list(items)

    def rotate():
        in_flight.append(start(pending_items.pop(0)))
        if len(in_flight) > 2:
            finish(in_flight.pop(0))

    for c0 in range(0, CONV_WIDTH, CONV_CHUNK):
        cs = slice(c0, c0 + CONV_CHUNK)
        rotate()
        u = proj(OFF_GC + c0, CONV_CHUNK) * proj(OFF_UX + c0, CONV_CHUNK)
        u_scr[HALO:HALO + tm, cs] = u
        conv = cw_ref[0:1, cs] * u_scr[HALO - 2:HALO - 2 + tm, cs]
        conv = conv + cw_ref[1:2, cs] * u_scr[HALO - 1:HALO - 1 + tm, cs]
        conv = conv + cw_ref[2:3, cs] * u
        rotate()
        t = proj(OFF_GB + c0, CONV_CHUNK) * conv * _silu(proj(OFF_ZC + c0, CONV_CHUNK))
        t_scr[:, cs] = t.astype(BF16)
    finish(in_flight.pop(0))
    za_ref[...] = _silu(proj(OFF_Z_ATTN, GROUP_WIDTH)).astype(BF16)
    yc_ref[...] = jnp.dot(t_scr[...], wpc_ref[...], preferred_element_type=F32).astype(BF16)
    finish(in_flight.pop(0))
    project_qkv(2)

    g2_d = DILATED_GROUPS[2][1]
    n_t2 = tm // g2_d

    def g2_piece(which, r, pair):
        pieces = [qkv_scr[pl.ds(pl.multiple_of(t * tm + r * n_t2, n_t2), n_t2),
                          pl.ds(col(2, which, pair), PAIR_WIDTH)] for t in range(S // tm)]
        return jnp.concatenate(pieces, axis=0)

    @pl.when(tile == S // tm - 1)
    def _():
        def body(step, carry):
            work = [(u, pair) for u in range(RES_STRIDE) for pair in range(N_PAIRS)]
            parts = [probabilities(2, pair, g2_piece(0, step * RES_STRIDE + u, pair),
                                   g2_piece(1, step * RES_STRIDE + u, pair), False) for u, pair in work]
            for (u, pair), (p, m) in zip(work, parts):
                r = step * RES_STRIDE + u
                new = weighted_values(p, m, g2_piece(2, r, pair))
                outs = []
                for t in range(S // tm):
                    start_row = t * tm + u * (tm // RES_STRIDE) + step
                    rows = pl.ds(start_row, n_t2, stride=RES_STRIDE)
                    acc, den, _ = merged(pair, rows, *[x[t * n_t2:(t + 1) * n_t2] for x in new])
                    outs.append(acc / den)
                o_ref[pair, pl.ds(r, L, stride=g2_d), :] = jnp.concatenate(outs, axis=0)
            return carry
        lax.fori_loop(0, g2_d // RES_STRIDE, body, 0)


def _proj_attn(x, ada3, w_in, b_in, conv_w, w_pc):
    bsz, s, d = x.shape
    tm = ROW_TILE
    bias = jnp.asarray(_attn_bias_table())
    const = dict(pipeline_mode=pl.Buffered(1))
    row = lambda width: pl.BlockSpec((None, tm, width), lambda b, i: (b, i, 0))
    return pl.pallas_call(
        _proj_attn_kernel,
        grid=(bsz, s // tm),
        in_specs=[row(d),
                  pl.BlockSpec((None, 3, d), lambda b, i: (b, 0, 0)),
                  pl.BlockSpec((d, OFF_GA), lambda b, i: (0, 0), **const),
                  pl.BlockSpec((1, OFF_GA), lambda b, i: (0, 0), **const),
                  pl.BlockSpec((CONV_K, CONV_WIDTH), lambda b, i: (0, 0), **const),
                  pl.BlockSpec((CONV_WIDTH, d), lambda b, i: (0, 0), **const),
                  pl.BlockSpec(bias.shape, lambda b, i: (0, 0, 0, 0), **const)],
        out_specs=[row(GROUP_WIDTH), row(d),
                   pl.BlockSpec((None, N_PAIRS, s, PAIR_WIDTH), lambda b, i: (b, 0, 0, 0))],
        out_shape=[jax.ShapeDtypeStruct((bsz, s, GROUP_WIDTH), BF16),
                   jax.ShapeDtypeStruct((bsz, s, d), BF16),
                   jax.ShapeDtypeStruct((bsz, N_PAIRS, s, PAIR_WIDTH), F32)],
        scratch_shapes=[pltpu.VMEM((tm + HALO, CONV_WIDTH), F32),
                        pltpu.VMEM((tm, CONV_WIDTH), BF16),
                        pltpu.VMEM((N_STAGE_SLABS, tm, LANES), F32),
                        pltpu.VMEM((s, QKV_WIDTH), BF16),
                        pltpu.VMEM((3, N_PAIRS, s, PAIR_WIDTH), F32),
                        pltpu.VMEM((N_PAIRS * (tm // SUB_BLOCK), 3, SUB_BLOCK, PAIR_WIDTH), F32),
                        pltpu.VMEM((4, PAIR_WIDTH, 2 * SUB_BLOCK), BF16)],
        compiler_params=pltpu.CompilerParams(
            dimension_semantics=("arbitrary", "arbitrary"), vmem_limit_bytes=VMEM_LIMIT),
        name="proj_attn",
    )(x, ada3, w_in, b_in, conv_w, w_pc, bias)


def _out_kernel(x_ref, ada_ref, oa_ref, za_ref, yc_ref, wg_ref, bg_ref, wpa_ref, wout_ref, bout_ref,
                lng_ref, lnb_ref, out_ref, m_scr, *, alpha):
    shift = ada_ref[0:1, :]
    scale = ada_ref[1:2, :]
    gate = ada_ref[2:3, :]
    h = (x_ref[...] * (1.0 + scale) + shift).astype(BF16)
    o_attn = jnp.concatenate([oa_ref[pair] for pair in range(N_PAIRS)], axis=1)
    ta = (o_attn * za_ref[...].astype(F32)).astype(BF16)

    for c0 in range(0, D_MODEL, OUT_COL_CHUNK):
        cs = slice(c0, c0 + OUT_COL_CHUNK)
        gs = slice(D_MODEL + c0, D_MODEL + c0 + OUT_COL_CHUNK)
        g_a = _sigmoid(jnp.dot(h, wg_ref[:, cs], preferred_element_type=F32) + bg_ref[:, cs])
        g_b = _sigmoid(jnp.dot(h, wg_ref[:, gs], preferred_element_type=F32) + bg_ref[:, gs])
        y_attn = jnp.dot(ta, wpa_ref[:, cs], preferred_element_type=F32)
        m_scr[:, cs] = g_a.astype(BF16) * y_attn.astype(BF16) + g_b.astype(BF16) * yc_ref[:, cs]

    for r0 in range(0, x_ref.shape[0], OUT_CHUNK):
        rows = slice(r0, r0 + OUT_CHUNK)
        sub = gate * (jnp.dot(m_scr[rows, :], wout_ref[...], preferred_element_type=F32) + bout_ref[...])
        r = alpha * x_ref[rows, :] + sub
        mu = jnp.mean(r, axis=-1, keepdims=True)
        cen = r - mu
        var = jnp.mean(cen * cen, axis=-1, keepdims=True)
        out_ref[rows, :] = cen * lax.rsqrt(var + LN_EPS) * lng_ref[...] + lnb_ref[...]


def _out(x, ada3, o_attn, za, yc, w_gate, b_gate, w_pa, w_out, b_out, ln_g, ln_b, alpha):
    bsz, s, d = x.shape
    tm = OUT_ROW_TILE
    const = dict(pipeline_mode=pl.Buffered(1))
    row = lambda width: pl.BlockSpec((None, tm, width), lambda b, i: (b, i, 0))
    vec = pl.BlockSpec((1, d), lambda b, i: (0, 0), **const)
    return pl.pallas_call(
        functools.partial(_out_kernel, alpha=alpha),
        grid=(bsz, s // tm),
        in_specs=[row(d), pl.BlockSpec((None, 3, d), lambda b, i: (b, 0, 0)),
                  pl.BlockSpec((None, N_PAIRS, tm, PAIR_WIDTH), lambda b, i: (b, 0, i, 0)),
                  row(GROUP_WIDTH), row(d),
                  pl.BlockSpec((d, 2 * d), lambda b, i: (0, 0), **const),
                  pl.BlockSpec((1, 2 * d), lambda b, i: (0, 0), **const),
                  pl.BlockSpec((GROUP_WIDTH, d), lambda b, i: (0, 0), **const),
                  pl.BlockSpec((d, d), lambda b, i: (0, 0), **const),
                  vec, vec, vec],
        out_specs=row(d),
        out_shape=jax.ShapeDtypeStruct((bsz, s, d), F32),
        scratch_shapes=[pltpu.VMEM((tm, d), BF16)],
        compiler_params=pltpu.CompilerParams(
            dimension_semantics=("arbitrary", "arbitrary"), vmem_limit_bytes=VMEM_LIMIT),
        name="out",
    )(x, ada3, o_attn, za, yc, w_gate, b_gate, w_pa, w_out, b_out, ln_g, ln_b)


def kernel(x, c, w_ada, b_ada, w_in, b_in, conv_w, w_proj_attn, w_proj_conv, w_out, b_out, ln_g, ln_b):
    bsz, s, d = x.shape
    depth = w_in.shape[0]
    alpha = (2.0 * depth) ** 0.25
    for layer in range(depth):
        ada3 = _ada(c, w_ada[layer], b_ada[layer]).reshape(bsz, 3, d)
        b_in2 = b_in[layer].reshape(1, D_IN)
        za, yc, o_attn = _proj_attn(x, ada3, w_in[layer, :, :OFF_GA].astype(BF16), b_in2[:, :OFF_GA],
                                    conv_w[layer], w_proj_conv[layer].astype(BF16))
        x = _out(x, ada3, o_attn, za, yc,
                 w_in[layer, :, OFF_GA:].astype(BF16), b_in2[:, OFF_GA:],
                 w_proj_attn[layer].astype(BF16), w_out[layer].astype(BF16),
                 b_out[layer].reshape(1, d), ln_g[layer].reshape(1, d), ln_b[layer].reshape(1, d), alpha)
    return x
```

```python
import functools
import itertools

import numpy as np
import jax
import jax.numpy as jnp
from jax import lax
from jax.experimental import pallas as pl
from jax.experimental.pallas import tpu as pltpu

D_MODEL = 1024
HEAD_DIM = 64
HEADS_PER_GROUP = 4
DILATED_GROUPS = ((128, 1), (512, 4), (2048, 16))
N_GROUPS = len(DILATED_GROUPS)
N_ATTN_HEADS = N_GROUPS * HEADS_PER_GROUP
ATTN_WIDTH = N_ATTN_HEADS * HEAD_DIM
GROUP_WIDTH = HEADS_PER_GROUP * HEAD_DIM
QKV_WIDTH = 3 * ATTN_WIDTH
CONV_WIDTH = D_MODEL
CONV_K = 3
SUB_BLOCK = 128
ALIBI_MAX_EXP = 8.0
LN_EPS = 1e-5

OFF_Z_ATTN = QKV_WIDTH
OFF_UX = OFF_Z_ATTN + GROUP_WIDTH
OFF_GB = OFF_UX + CONV_WIDTH
OFF_GC = OFF_GB + CONV_WIDTH
OFF_ZC = OFF_GC + CONV_WIDTH
OFF_GA = OFF_ZC + CONV_WIDTH
D_IN = OFF_GA + 2 * D_MODEL

ROW_TILE = 512
OUT_ROW_TILE = 1024
OUT_CHUNK = 256
OUT_COL_CHUNK = 256
CONV_CHUNK = 256
LANES = 128
PAIR_WIDTH = 2 * HEAD_DIM
N_PAIRS = HEADS_PER_GROUP // 2
REGROUP_STRIDE = 4
RES_STRIDE = REGROUP_STRIDE
N_STAGE_SLABS = 6
HALO = 8
VMEM_LIMIT = 56 * 1024 * 1024

F32 = jnp.float32
BF16 = jnp.bfloat16

assert PAIR_WIDTH == LANES and ROW_TILE // DILATED_GROUPS[1][1] == SUB_BLOCK
assert DILATED_GROUPS[1][1] == RES_STRIDE and DILATED_GROUPS[2][1] == RES_STRIDE * RES_STRIDE
assert 2 * (CONV_WIDTH // CONV_CHUNK) == 2 * (ROW_TILE // SUB_BLOCK)


def _sigmoid(v):
    return 0.5 * jnp.tanh(0.5 * v) + 0.5


def _silu(v):
    return v * _sigmoid(v)


def _ada_kernel(c_ref, w_ref, b_ref, o_ref):
    o_ref[...] = jnp.dot(_silu(c_ref[...]).astype(BF16), w_ref[...].astype(BF16),
                         preferred_element_type=F32) + b_ref[...]


def _ada(c, w_ada, b_ada):
    bsz, d = c.shape
    n_out = w_ada.shape[1]
    return pl.pallas_call(
        _ada_kernel,
        grid=(n_out // d,),
        in_specs=[pl.BlockSpec((bsz, d), lambda j: (0, 0)),
                  pl.BlockSpec((d, d), lambda j: (0, j)),
                  pl.BlockSpec((1, d), lambda j: (0, j))],
        out_specs=pl.BlockSpec((bsz, d), lambda j: (0, j)),
        out_shape=jax.ShapeDtypeStruct((bsz, n_out), F32),
        name="ada",
    )(c, w_ada, b_ada.reshape(1, n_out))


def _attn_bias_table():
    L = SUB_BLOCK
    tabs = []
    for group, (window, dilation) in enumerate(DILATED_GROUPS):
        span = window // dilation
        heads = np.arange(group * HEADS_PER_GROUP, (group + 1) * HEADS_PER_GROUP, dtype=np.float64)
        slopes = 2.0 ** (-ALIBI_MAX_EXP * (heads + 1.0) / N_ATTN_HEADS)
        delta = (np.arange(L)[:, None] + L - np.arange(2 * L)[None, :]).astype(np.float64)
        valid = (delta >= 0) & (delta <= span)
        bias = np.where(valid[None], -slopes[:, None, None] * (delta * dilation)[None], -np.inf)
        tabs.append(bias.reshape(HEADS_PER_GROUP // 2, 2 * L, 2 * L))
    return np.stack(tabs).astype(np.float32)


def _proj_attn_kernel(x_ref, ada_ref, w_ref, b_ref, cw_ref, wpc_ref, bias_ref,
                      za_ref, yc_ref, o_ref, u_scr, t_scr, stage_scr, qkv_scr, res, tmp_scr, kt_scr):
    tm = x_ref.shape[0]
    S = qkv_scr.shape[0]
    L = SUB_BLOCK
    tile = pl.program_id(1)
    first_tile = tile == 0
    base = pl.multiple_of(tile * tm, tm)
    prev_base = pl.multiple_of(jnp.maximum(tile - 1, 0) * tm, tm)

    @pl.when(first_tile)
    def _():
        u_scr[0:HALO, :] = jnp.zeros((HALO, CONV_WIDTH), F32)

    @pl.when(jnp.logical_not(first_tile))
    def _():
        u_scr[0:HALO, :] = u_scr[tm:tm + HALO, :]

    shift = ada_ref[0:1, :]
    scale = ada_ref[1:2, :]
    h = (x_ref[...] * (1.0 + scale) + shift).astype(BF16)

    def proj(lo, width):
        return (jnp.dot(h, w_ref[:, lo:lo + width], preferred_element_type=F32)
                + b_ref[:, lo:lo + width])

    slabs = itertools.cycle(range(N_STAGE_SLABS))

    def project_qkv(g):
        d = DILATED_GROUPS[g][1]
        for which in range(3):
            c0 = which * ATTN_WIDTH + g * GROUP_WIDTH
            blk = proj(c0, GROUP_WIDTH)
            if d == 1:
                qkv_scr[pl.ds(base, tm), c0:c0 + GROUP_WIDTH] = blk.astype(BF16)
                continue
            n_t = tm // d
            for lo in range(0, GROUP_WIDTH, LANES):
                cols = slice(c0 + lo, c0 + lo + LANES)
                slab = next(slabs)
                stage_scr[slab] = blk[:, lo:lo + LANES]
                if d == REGROUP_STRIDE:
                    for r in range(d):
                        qkv_scr[pl.ds(base + r * n_t, n_t), cols] = (
                            stage_scr[slab, pl.ds(r, n_t, stride=d), :].astype(BF16))
                else:
                    assert d == REGROUP_STRIDE * REGROUP_STRIDE
                    n_1 = tm // REGROUP_STRIDE
                    slab2 = next(slabs)
                    for r1 in range(REGROUP_STRIDE):
                        stage_scr[slab2, r1 * n_1:(r1 + 1) * n_1, :] = (
                            stage_scr[slab, pl.ds(r1, n_1, stride=REGROUP_STRIDE), :])
                    for r1 in range(REGROUP_STRIDE):
                        for r2 in range(REGROUP_STRIDE):
                            r = r2 * REGROUP_STRIDE + r1
                            qkv_scr[pl.ds(base + r * n_t, n_t), cols] = stage_scr[
                                slab2, pl.ds(r1 * n_1 + r2, n_t, stride=REGROUP_STRIDE), :].astype(BF16)

    project_qkv(1)
    project_qkv(0)

    low_half = lax.broadcasted_iota(jnp.int32, (L, PAIR_WIDTH), 1) < HEAD_DIM
    no_prev = jnp.logical_and(lax.broadcasted_iota(jnp.int32, (2 * L, 2 * L), 1) < L, first_tile)

    def col(g, which, pair):
        return which * ATTN_WIDTH + g * GROUP_WIDTH + pair * PAIR_WIDTH

    kt_slots = itertools.cycle(range(kt_scr.shape[0]))

    def probabilities(g, pair, q, k, maybe_no_prev):
        q = q * 0.125
        zero = jnp.zeros_like(q)
        q2 = jnp.concatenate([jnp.where(low_half, q, zero), jnp.where(low_half, zero, q)], axis=0)
        slot = next(kt_slots)
        n_keys = k.shape[0]
        kt_scr[slot, :, :n_keys] = k.T
        sc = jnp.dot(q2, kt_scr[slot, :, :n_keys], preferred_element_type=F32)
        sc = sc + bias_ref[g, pair, :, 2 * L - k.shape[0]:]
        if maybe_no_prev:
            sc = jnp.where(no_prev, -jnp.inf, sc)
        m = jnp.max(sc, axis=-1, keepdims=True)
        return jnp.exp(sc - m).astype(BF16), m

    def weighted_values(p, m, v):
        v1 = jnp.concatenate([v, jnp.ones(v.shape, BF16)], axis=1)
        pv = jnp.dot(p, v1, preferred_element_type=F32)
        mb = jnp.broadcast_to(m, (2 * L, PAIR_WIDTH))
        return (jnp.where(low_half, pv[:L, :PAIR_WIDTH], pv[L:, :PAIR_WIDTH]),
                jnp.where(low_half, pv[:L, PAIR_WIDTH:], pv[L:, PAIR_WIDTH:]),
                jnp.where(low_half, mb[:L], mb[L:]))

    def merged(pair, rows, acc, den, m):
        m_old = res[2, pair, rows, :]
        m_new = jnp.maximum(m_old, m)
        a = jnp.exp(m_old - m_new)
        b = jnp.exp(m - m_new)
        return a * res[0, pair, rows, :] + b * acc, a * res[1, pair, rows, :] + b * den, m_new

    g1_d = DILATED_GROUPS[1][1]

    def g1_scores(r):
        out = []
        for pair in range(N_PAIRS):
            cur = pl.ds(base + r * L, L)
            prev = pl.ds(prev_base + r * L, L)
            k = jnp.concatenate([qkv_scr[prev, pl.ds(col(1, 1, pair), PAIR_WIDTH)],
                                 qkv_scr[cur, pl.ds(col(1, 1, pair), PAIR_WIDTH)]], axis=0)
            out.append(probabilities(1, pair, qkv_scr[cur, pl.ds(col(1, 0, pair), PAIR_WIDTH)], k, True))
        return out

    def g1_finish(r, parts):
        for pair, (p, m) in enumerate(parts):
            cur = pl.ds(base + r * L, L)
            prev = pl.ds(prev_base + r * L, L)
            v = jnp.concatenate([qkv_scr[prev, pl.ds(col(1, 2, pair), PAIR_WIDTH)],
                                 qkv_scr[cur, pl.ds(col(1, 2, pair), PAIR_WIDTH)]], axis=0)
            acc, den, mm = weighted_values(p, m, v)
            rows = pl.ds(base + r * L, L)
            res[0, pair, rows, :] = acc
            res[1, pair, rows, :] = den
            res[2, pair, rows, :] = mm

    def g0_rows(c):
        q0 = base + c * L
        k0 = jnp.maximum(q0 - L, 0) if c == 0 else q0 - L
        return pl.ds(pl.multiple_of(q0, L), L), pl.ds(pl.multiple_of(k0, L), L)

    def g0_scores(c):
        cur, prev = g0_rows(c)
        out = []
        for pair in range(N_PAIRS):
            k = jnp.concatenate([qkv_scr[prev, pl.ds(col(0, 1, pair), PAIR_WIDTH)],
                                 qkv_scr[cur, pl.ds(col(0, 1, pair), PAIR_WIDTH)]], axis=0)
            out.append(probabilities(0, pair, qkv_scr[cur, pl.ds(col(0, 0, pair), PAIR_WIDTH)], k, c == 0))
        return out

    def g0_finish(c, parts):
        cur, prev = g0_rows(c)
        for pair, (p, m) in enumerate(parts):
            v = jnp.concatenate([qkv_scr[prev, pl.ds(col(0, 2, pair), PAIR_WIDTH)],
                                 qkv_scr[cur, pl.ds(col(0, 2, pair), PAIR_WIDTH)]], axis=0)
            slab = c * N_PAIRS + pair
            for k, val in enumerate(weighted_values(p, m, v)):
                tmp_scr[slab, k] = val
            n_r = L // RES_STRIDE
            for r in range(RES_STRIDE):
                rows = pl.ds(pl.multiple_of(base + r * (tm // RES_STRIDE) + c * n_r, n_r), n_r)
                acc, den, mm = merged(pair, rows, *[tmp_scr[slab, k, pl.ds(r, n_r, stride=RES_STRIDE), :]
                                                    for k in range(3)])
                res[0, pair, rows, :] = acc
                res[1, pair, rows, :] = den
                res[2, pair, rows, :] = mm

    items = ([(g1_scores, g1_finish, (j,)) for j in range(tm // L)]
             + [(g0_scores, g0_finish, (j,)) for j in range(tm // L)])

    def start(item):
        scores_fn, finish_fn, args = item
        return finish_fn, [(a, scores_fn(a)) for a in args]

    def finish(pending):
        finish_fn, parts = pending
        for a, part in parts:
            finish_fn(a, part)

    in_flight = []
    pending_items = list(items)

    def rotate():
        in_flight.append(start(pending_items.pop(0)))
        if len(in_flight) > 2:
            finish(in_flight.pop(0))

    for c0 in range(0, CONV_WIDTH, CONV_CHUNK):
        cs = slice(c0, c0 + CONV_CHUNK)
        u = proj(OFF_GC + c0, CONV_CHUNK) * proj(OFF_UX + c0, CONV_CHUNK)
        u_scr[HALO:HALO + tm, cs] = u
        conv = cw_ref[0:1, cs] * u_scr[HALO - 2:HALO - 2 + tm, cs]
        conv = conv + cw_ref[1:2, cs] * u_scr[HALO - 1:HALO - 1 + tm, cs]
        conv = conv + cw_ref[2:3, cs] * u
        rotate()
        t = proj(OFF_GB + c0, CONV_CHUNK) * conv * _silu(proj(OFF_ZC + c0, CONV_CHUNK))
        t_scr[:, cs] = t.astype(BF16)
        rotate()
    finish(in_flight.pop(0))
    za_ref[...] = _silu(proj(OFF_Z_ATTN, GROUP_WIDTH)).astype(BF16)
    yc_ref[...] = jnp.dot(t_scr[...], wpc_ref[...], preferred_element_type=F32).astype(BF16)
    finish(in_flight.pop(0))
    project_qkv(2)

    g2_d = DILATED_GROUPS[2][1]
    n_t2 = tm // g2_d

    def g2_piece(which, r, pair):
        pieces = [qkv_scr[pl.ds(pl.multiple_of(t * tm + r * n_t2, n_t2), n_t2),
                          pl.ds(col(2, which, pair), PAIR_WIDTH)] for t in range(S // tm)]
        return jnp.concatenate(pieces, axis=0)

    @pl.when(tile == S // tm - 1)
    def _():
        def body(step, carry):
            work = [(u, pair) for u in range(RES_STRIDE) for pair in range(N_PAIRS)]
            parts = [probabilities(2, pair, g2_piece(0, step * RES_STRIDE + u, pair),
                                   g2_piece(1, step * RES_STRIDE + u, pair), False) for u, pair in work]
            for (u, pair), (p, m) in zip(work, parts):
                r = step * RES_STRIDE + u
                new = weighted_values(p, m, g2_piece(2, r, pair))
                outs = []
                for t in range(S // tm):
                    start_row = t * tm + u * (tm // RES_STRIDE) + step
                    rows = pl.ds(start_row, n_t2, stride=RES_STRIDE)
                    acc, den, _ = merged(pair, rows, *[x[t * n_t2:(t + 1) * n_t2] for x in new])
                    outs.append(acc / den)
                o_ref[pair, pl.ds(r, L, stride=g2_d), :] = jnp.concatenate(outs, axis=0)
            return carry
        lax.fori_loop(0, g2_d // RES_STRIDE, body, 0)


def _proj_attn(x, ada3, w_in, b_in, conv_w, w_pc):
    bsz, s, d = x.shape
    tm = ROW_TILE
    bias = jnp.asarray(_attn_bias_table())
    const = dict(pipeline_mode=pl.Buffered(1))
    row = lambda width: pl.BlockSpec((None, tm, width), lambda b, i: (b, i, 0))
    return pl.pallas_call(
        _proj_attn_kernel,
        grid=(bsz, s // tm),
        in_specs=[row(d),
                  pl.BlockSpec((None, 3, d), lambda b, i: (b, 0, 0)),
                  pl.BlockSpec((d, OFF_GA), lambda b, i: (0, 0), **const),
                  pl.BlockSpec((1, OFF_GA), lambda b, i: (0, 0), **const),
                  pl.BlockSpec((CONV_K, CONV_WIDTH), lambda b, i: (0, 0), **const),
                  pl.BlockSpec((CONV_WIDTH, d), lambda b, i: (0, 0), **const),
                  pl.BlockSpec(bias.shape, lambda b, i: (0, 0, 0, 0), **const)],
        out_specs=[row(GROUP_WIDTH), row(d),
                   pl.BlockSpec((None, N_PAIRS, s, PAIR_WIDTH), lambda b, i: (b, 0, 0, 0))],
        out_shape=[jax.ShapeDtypeStruct((bsz, s, GROUP_WIDTH), BF16),
                   jax.ShapeDtypeStruct((bsz, s, d), BF16),
                   jax.ShapeDtypeStruct((bsz, N_PAIRS, s, PAIR_WIDTH), F32)],
        scratch_shapes=[pltpu.VMEM((tm + HALO, CONV_WIDTH), F32),
                        pltpu.VMEM((tm, CONV_WIDTH), BF16),
                        pltpu.VMEM((N_STAGE_SLABS, tm, LANES), F32),
                        pltpu.VMEM((s, QKV_WIDTH), BF16),
                        pltpu.VMEM((3, N_PAIRS, s, PAIR_WIDTH), F32),
                        pltpu.VMEM((N_PAIRS * (tm // SUB_BLOCK), 3, SUB_BLOCK, PAIR_WIDTH), F32),
                        pltpu.VMEM((4, PAIR_WIDTH, 2 * SUB_BLOCK), BF16)],
        compiler_params=pltpu.CompilerParams(
            dimension_semantics=("arbitrary", "arbitrary"), vmem_limit_bytes=VMEM_LIMIT),
        name="proj_attn",
    )(x, ada3, w_in, b_in, conv_w, w_pc, bias)


def _out_kernel(x_ref, ada_ref, oa_ref, za_ref, yc_ref, wg_ref, bg_ref, wpa_ref, wout_ref, bout_ref,
                lng_ref, lnb_ref, out_ref, m_scr, *, alpha):
    shift = ada_ref[0:1, :]
    scale = ada_ref[1:2, :]
    gate = ada_ref[2:3, :]
    h = (x_ref[...] * (1.0 + scale) + shift).astype(BF16)
    o_attn = jnp.concatenate([oa_ref[pair] for pair in range(N_PAIRS)], axis=1)
    ta = (o_attn * za_ref[...].astype(F32)).astype(BF16)

    for c0 in range(0, D_MODEL, OUT_COL_CHUNK):
        cs = slice(c0, c0 + OUT_COL_CHUNK)
        gs = slice(D_MODEL + c0, D_MODEL + c0 + OUT_COL_CHUNK)
        g_a = _sigmoid(jnp.dot(h, wg_ref[:, cs], preferred_element_type=F32) + bg_ref[:, cs])
        g_b = _sigmoid(jnp.dot(h, wg_ref[:, gs], preferred_element_type=F32) + bg_ref[:, gs])
        y_attn = jnp.dot(ta, wpa_ref[:, cs], preferred_element_type=F32)
        m_scr[:, cs] = g_a.astype(BF16) * y_attn.astype(BF16) + g_b.astype(BF16) * yc_ref[:, cs]

    for r0 in range(0, x_ref.shape[0], OUT_CHUNK):
        rows = slice(r0, r0 + OUT_CHUNK)
        sub = gate * (jnp.dot(m_scr[rows, :], wout_ref[...], preferred_element_type=F32) + bout_ref[...])
        r = alpha * x_ref[rows, :] + sub
        mu = jnp.mean(r, axis=-1, keepdims=True)
        cen = r - mu
        var = jnp.mean(cen * cen, axis=-1, keepdims=True)
        out_ref[rows, :] = cen * lax.rsqrt(var + LN_EPS) * lng_ref[...] + lnb_ref[...]


def _out(x, ada3, o_attn, za, yc, w_gate, b_gate, w_pa, w_out, b_out, ln_g, ln_b, alpha):
    bsz, s, d = x.shape
    tm = OUT_ROW_TILE
    const = dict(pipeline_mode=pl.Buffered(1))
    row = lambda width: pl.BlockSpec((None, tm, width), lambda b, i: (b, i, 0))
    vec = pl.BlockSpec((1, d), lambda b, i: (0, 0), **const)
    return pl.pallas_call(
        functools.partial(_out_kernel, alpha=alpha),
        grid=(bsz, s // tm),
        in_specs=[row(d), pl.BlockSpec((None, 3, d), lambda b, i: (b, 0, 0)),
                  pl.BlockSpec((None, N_PAIRS, tm, PAIR_WIDTH), lambda b, i: (b, 0, i, 0)),
                  row(GROUP_WIDTH), row(d),
                  pl.BlockSpec((d, 2 * d), lambda b, i: (0, 0), **const),
                  pl.BlockSpec((1, 2 * d), lambda b, i: (0, 0), **const),
                  pl.BlockSpec((GROUP_WIDTH, d), lambda b, i: (0, 0), **const),
                  pl.BlockSpec((d, d), lambda b, i: (0, 0), **const),
                  vec, vec, vec],
        out_specs=row(d),
        out_shape=jax.ShapeDtypeStruct((bsz, s, d), F32),
        scratch_shapes=[pltpu.VMEM((tm, d), BF16)],
        compiler_params=pltpu.CompilerParams(
            dimension_semantics=("arbitrary", "arbitrary"), vmem_limit_bytes=VMEM_LIMIT),
        name="out",
    )(x, ada3, o_attn, za, yc, w_gate, b_gate, w_pa, w_out, b_out, ln_g, ln_b)


def kernel(x, c, w_ada, b_ada, w_in, b_in, conv_w, w_proj_attn, w_proj_conv, w_out, b_out, ln_g, ln_b):
    bsz, s, d = x.shape
    depth = w_in.shape[0]
    alpha = (2.0 * depth) ** 0.25
    for layer in range(depth):
        ada3 = _ada(c, w_ada[layer], b_ada[layer]).reshape(bsz, 3, d)
        b_in2 = b_in[layer].reshape(1, D_IN)
        za, yc, o_attn = _proj_attn(x, ada3, w_in[layer, :, :OFF_GA].astype(BF16), b_in2[:, :OFF_GA],
                                    conv_w[layer], w_proj_conv[layer].astype(BF16))
        x = _out(x, ada3, o_attn, za, yc,
                 w_in[layer, :, OFF_GA:].astype(BF16), b_in2[:, OFF_GA:],
                 w_proj_attn[layer].astype(BF16), w_out[layer].astype(BF16),
                 b_out[layer].reshape(1, d), ln_g[layer].reshape(1, d), ln_b[layer].reshape(1, d), alpha)
    return x
```

```python
import functools
import itertools

import numpy as np
import jax
import jax.numpy as jnp
from jax import lax
from jax.experimental import pallas as pl
from jax.experimental.pallas import tpu as pltpu

D_MODEL = 1024
HEAD_DIM = 64
HEADS_PER_GROUP = 4
DILATED_GROUPS = ((128, 1), (512, 4), (2048, 16))
N_GROUPS = len(DILATED_GROUPS)
N_ATTN_HEADS = N_GROUPS * HEADS_PER_GROUP
ATTN_WIDTH = N_ATTN_HEADS * HEAD_DIM
GROUP_WIDTH = HEADS_PER_GROUP * HEAD_DIM
QKV_WIDTH = 3 * ATTN_WIDTH
CONV_WIDTH = D_MODEL
CONV_K = 3
SUB_BLOCK = 128
ALIBI_MAX_EXP = 8.0
LN_EPS = 1e-5

OFF_Z_ATTN = QKV_WIDTH
OFF_UX = OFF_Z_ATTN + GROUP_WIDTH
OFF_GB = OFF_UX + CONV_WIDTH
OFF_GC = OFF_GB + CONV_WIDTH
OFF_ZC = OFF_GC + CONV_WIDTH
OFF_GA = OFF_ZC + CONV_WIDTH
D_IN = OFF_GA + 2 * D_MODEL

ROW_TILE = 512
OUT_ROW_TILE = 1024
OUT_CHUNK = 256
OUT_COL_CHUNK = 256
CONV_CHUNK = 256
LANES = 128
PAIR_WIDTH = 2 * HEAD_DIM
N_PAIRS = HEADS_PER_GROUP // 2
REGROUP_STRIDE = 4
RES_STRIDE = REGROUP_STRIDE
N_STAGE_SLABS = 6
HALO = 8
VMEM_LIMIT = 56 * 1024 * 1024

F32 = jnp.float32
BF16 = jnp.bfloat16

assert PAIR_WIDTH == LANES and ROW_TILE // DILATED_GROUPS[1][1] == SUB_BLOCK
assert DILATED_GROUPS[1][1] == RES_STRIDE and DILATED_GROUPS[2][1] == RES_STRIDE * RES_STRIDE
assert 2 * (CONV_WIDTH // CONV_CHUNK) == 2 * (ROW_TILE // SUB_BLOCK)


def _sigmoid(v):
    return 0.5 * jnp.tanh(0.5 * v) + 0.5


def _silu(v):
    return v * _sigmoid(v)


def _ada_kernel(c_ref, w_ref, b_ref, o_ref):
    o_ref[...] = jnp.dot(_silu(c_ref[...]).astype(BF16), w_ref[...].astype(BF16),
                         preferred_element_type=F32) + b_ref[...]


def _ada(c, w_ada, b_ada):
    bsz, d = c.shape
    n_out = w_ada.shape[1]
    return pl.pallas_call(
        _ada_kernel,
        grid=(n_out // d,),
        in_specs=[pl.BlockSpec((bsz, d), lambda j: (0, 0)),
                  pl.BlockSpec((d, d), lambda j: (0, j)),
                  pl.BlockSpec((1, d), lambda j: (0, j))],
        out_specs=pl.BlockSpec((bsz, d), lambda j: (0, j)),
        out_shape=jax.ShapeDtypeStruct((bsz, n_out), F32),
        name="ada",
    )(c, w_ada, b_ada.reshape(1, n_out))


def _attn_bias_table():
    L = SUB_BLOCK
    tabs = []
    for group, (window, dilation) in enumerate(DILATED_GROUPS):
        span = window // dilation
        heads = np.arange(group * HEADS_PER_GROUP, (group + 1) * HEADS_PER_GROUP, dtype=np.float64)
        slopes = 2.0 ** (-ALIBI_MAX_EXP * (heads + 1.0) / N_ATTN_HEADS)
        delta = (np.arange(L)[:, None] + L - np.arange(2 * L)[None, :]).astype(np.float64)
        valid = (delta >= 0) & (delta <= span)
        bias = np.where(valid[None], -slopes[:, None, None] * (delta * dilation)[None], -np.inf)
        tabs.append(bias.reshape(HEADS_PER_GROUP // 2, 2 * L, 2 * L))
    return np.stack(tabs).astype(np.float32)


def _proj_attn_kernel(x_ref, ada_ref, w_ref, b_ref, cw_ref, wpc_ref, bias_ref,
                      za_ref, yc_ref, o_ref, u_scr, t_scr, stage_scr, qkv_scr, res, tmp_scr, kt_scr):
    tm = x_ref.shape[0]
    S = qkv_scr.shape[0]
    L = SUB_BLOCK
    tile = pl.program_id(1)
    first_tile = tile == 0
    base = pl.multiple_of(tile * tm, tm)
    prev_base = pl.multiple_of(jnp.maximum(tile - 1, 0) * tm, tm)

    @pl.when(first_tile)
    def _():
        u_scr[0:HALO, :] = jnp.zeros((HALO, CONV_WIDTH), F32)

    @pl.when(jnp.logical_not(first_tile))
    def _():
        u_scr[0:HALO, :] = u_scr[tm:tm + HALO, :]

    shift = ada_ref[0:1, :]
    scale = ada_ref[1:2, :]
    h = (x_ref[...] * (1.0 + scale) + shift).astype(BF16)

    def proj(lo, width):
        return (jnp.dot(h, w_ref[:, lo:lo + width], preferred_element_type=F32)
                + b_ref[:, lo:lo + width])

    slabs = itertools.cycle(range(N_STAGE_SLABS))

    def project_qkv(g):
        d = DILATED_GROUPS[g][1]
        for which in range(3):
            c0 = which * ATTN_WIDTH + g * GROUP_WIDTH
            blk = proj(c0, GROUP_WIDTH)
            if d == 1:
                qkv_scr[pl.ds(base, tm), c0:c0 + GROUP_WIDTH] = blk.astype(BF16)
                continue
            n_t = tm // d
            for lo in range(0, GROUP_WIDTH, LANES):
                cols = slice(c0 + lo, c0 + lo + LANES)
                slab = next(slabs)
                stage_scr[slab] = blk[:, lo:lo + LANES]
                if d == REGROUP_STRIDE:
                    for r in range(d):
                        qkv_scr[pl.ds(base + r * n_t, n_t), cols] = (
                            stage_scr[slab, pl.ds(r, n_t, stride=d), :].astype(BF16))
                else:
                    assert d == REGROUP_STRIDE * REGROUP_STRIDE
                    n_1 = tm // REGROUP_STRIDE
                    slab2 = next(slabs)
                    for r1 in range(REGROUP_STRIDE):
                        stage_scr[slab2, r1 * n_1:(r1 + 1) * n_1, :] = (
                            stage_scr[slab, pl.ds(r1, n_1, stride=REGROUP_STRIDE), :])
                    for r1 in range(REGROUP_STRIDE):
                        for r2 in range(REGROUP_STRIDE):
                            r = r2 * REGROUP_STRIDE + r1
                            qkv_scr[pl.ds(base + r * n_t, n_t), cols] = stage_scr[
                                slab2, pl.ds(r1 * n_1 + r2, n_t, stride=REGROUP_STRIDE), :].astype(BF16)

    project_qkv(1)
    project_qkv(0)

    low_half = lax.broadcasted_iota(jnp.int32, (L, PAIR_WIDTH), 1) < HEAD_DIM
    no_prev = jnp.logical_and(lax.broadcasted_iota(jnp.int32, (2 * L, 2 * L), 1) < L, first_tile)

    def col(g, which, pair):
        return which * ATTN_WIDTH + g * GROUP_WIDTH + pair * PAIR_WIDTH

    kt_slots = itertools.cycle(range(kt_scr.shape[0]))

    def probabilities(g, pair, q, k, maybe_no_prev):
        q = q * 0.125
        zero = jnp.zeros_like(q)
        q2 = jnp.concatenate([jnp.where(low_half, q, zero), jnp.where(low_half, zero, q)], axis=0)
        slot = next(kt_slots)
        n_keys = k.shape[0]
        kt_scr[slot, :, :n_keys] = k.T
        sc = jnp.dot(q2, kt_scr[slot, :, :n_keys], preferred_element_type=F32)
        sc = sc + bias_ref[g, pair, :, 2 * L - k.shape[0]:]
        if maybe_no_prev:
            sc = jnp.where(no_prev, -jnp.inf, sc)
        m = jnp.max(sc, axis=-1, keepdims=True)
        return jnp.exp(sc - m).astype(BF16), m

    def weighted_values(p, m, v):
        v1 = jnp.concatenate([v, jnp.ones(v.shape, BF16)], axis=1)
        pv = jnp.dot(p, v1, preferred_element_type=F32)
        mb = jnp.broadcast_to(m, (2 * L, PAIR_WIDTH))
        return (jnp.where(low_half, pv[:L, :PAIR_WIDTH], pv[L:, :PAIR_WIDTH]),
                jnp.where(low_half, pv[:L, PAIR_WIDTH:], pv[L:, PAIR_WIDTH:]),
                jnp.where(low_half, mb[:L], mb[L:]))

    def merged(pair, rows, acc, den, m):
        m_old = res[2, pair, rows, :]
        m_new = jnp.maximum(m_old, m)
        a = jnp.exp(m_old - m_new)
        b = jnp.exp(m - m_new)
        return a * res[0, pair, rows, :] + b * acc, a * res[1, pair, rows, :] + b * den, m_new

    g1_d = DILATED_GROUPS[1][1]

    def g1_scores(r):
        out = []
        for pair in range(N_PAIRS):
            cur = pl.ds(base + r * L, L)
            prev = pl.ds(prev_base + r * L, L)
            k = jnp.concatenate([qkv_scr[prev, pl.ds(col(1, 1, pair), PAIR_WIDTH)],
                                 qkv_scr[cur, pl.ds(col(1, 1, pair), PAIR_WIDTH)]], axis=0)
            out.append(probabilities(1, pair, qkv_scr[cur, pl.ds(col(1, 0, pair), PAIR_WIDTH)], k, True))
        return out

    def g1_finish(r, parts):
        for pair, (p, m) in enumerate(parts):
            cur = pl.ds(base + r * L, L)
            prev = pl.ds(prev_base + r * L, L)
            v = jnp.concatenate([qkv_scr[prev, pl.ds(col(1, 2, pair), PAIR_WIDTH)],
                                 qkv_scr[cur, pl.ds(col(1, 2, pair), PAIR_WIDTH)]], axis=0)
            acc, den, mm = weighted_values(p, m, v)
            rows = pl.ds(base + r * L, L)
            res[0, pair, rows, :] = acc
            res[1, pair, rows, :] = den
            res[2, pair, rows, :] = mm

    def g0_rows(c):
        q0 = base + c * L
        k0 = jnp.maximum(q0 - L, 0) if c == 0 else q0 - L
        return pl.ds(pl.multiple_of(q0, L), L), pl.ds(pl.multiple_of(k0, L), L)

    def g0_scores(c):
        cur, prev = g0_rows(c)
        out = []
        for pair in range(N_PAIRS):
            k = jnp.concatenate([qkv_scr[prev, pl.ds(col(0, 1, pair), PAIR_WIDTH)],
                                 qkv_scr[cur, pl.ds(col(0, 1, pair), PAIR_WIDTH)]], axis=0)
            out.append(probabilities(0, pair, qkv_scr[cur, pl.ds(col(0, 0, pair), PAIR_WIDTH)], k, c == 0))
        return out

    def g0_finish(c, parts):
        cur, prev = g0_rows(c)
        for pair, (p, m) in enumerate(parts):
            v = jnp.concatenate([qkv_scr[prev, pl.ds(col(0, 2, pair), PAIR_WIDTH)],
                                 qkv_scr[cur, pl.ds(col(0, 2, pair), PAIR_WIDTH)]], axis=0)
            slab = c * N_PAIRS + pair
            for k, val in enumerate(weighted_values(p, m, v)):
                tmp_scr[slab, k] = val
            n_r = L // RES_STRIDE
            for r in range(RES_STRIDE):
                rows = pl.ds(pl.multiple_of(base + r * (tm // RES_STRIDE) + c * n_r, n_r), n_r)
                acc, den, mm = merged(pair, rows, *[tmp_scr[slab, k, pl.ds(r, n_r, stride=RES_STRIDE), :]
                                                    for k in range(3)])
                res[0, pair, rows, :] = acc
                res[1, pair, rows, :] = den
                res[2, pair, rows, :] = mm

    items = ([(g1_scores, g1_finish, (j,)) for j in range(tm // L)]
             + [(g0_scores, g0_finish, (j,)) for j in range(tm // L)])

    def start(item):
        scores_fn, finish_fn, args = item
        return finish_fn, [(a, scores_fn(a)) for a in args]

    def finish(pending):
        finish_fn, parts = pending
        for a, part in parts:
            finish_fn(a, part)

    in_flight = []
    pending_items = list(items)

    def rotate():
        in_flight.append(start(pending_items.pop(0)))
        if len(in_flight) > 2:
            finish(in_flight.pop(0))

    for c0 in range(0, CONV_WIDTH, CONV_CHUNK):
        cs = slice(c0, c0 + CONV_CHUNK)
        u = proj(OFF_GC + c0, CONV_CHUNK) * proj(OFF_UX + c0, CONV_CHUNK)
        u_scr[HALO:HALO + tm, cs] = u
        conv = cw_ref[0:1, cs] * u_scr[HALO - 2:HALO - 2 + tm, cs]
        conv = conv + cw_ref[1:2, cs] * u_scr[HALO - 1:HALO - 1 + tm, cs]
        conv = conv + cw_ref[2:3, cs] * u
        rotate()
        t = proj(OFF_GB + c0, CONV_CHUNK) * conv * _silu(proj(OFF_ZC + c0, CONV_CHUNK))
        t_scr[:, cs] = t.astype(BF16)
        rotate()
    finish(in_flight.pop(0))
    za_ref[...] = _silu(proj(OFF_Z_ATTN, GROUP_WIDTH)).astype(BF16)
    yc_ref[...] = jnp.dot(t_scr[...], wpc_ref[...], preferred_element_type=F32).astype(BF16)
    finish(in_flight.pop(0))
    project_qkv(2)

    g2_d = DILATED_GROUPS[2][1]
    n_t2 = tm // g2_d

    def g2_piece(which, r, pair):
        pieces = [qkv_scr[pl.ds(pl.multiple_of(t * tm + r * n_t2, n_t2), n_t2),
                          pl.ds(col(2, which, pair), PAIR_WIDTH)] for t in range(S // tm)]
        return jnp.concatenate(pieces, axis=0)

    @pl.when(tile == S // tm - 1)
    def _():
        def body(step, carry):
            work = [(u, pair) for u in range(RES_STRIDE) for pair in range(N_PAIRS)]
            parts = [probabilities(2, pair, g2_piece(0, step * RES_STRIDE + u, pair),
                                   g2_piece(1, step * RES_STRIDE + u, pair), False) for u, pair in work]
            for (u, pair), (p, m) in zip(work, parts):
                r = step * RES_STRIDE + u
                new = weighted_values(p, m, g2_piece(2, r, pair))
                outs = []
                for t in range(S // tm):
                    start_row = t * tm + u * (tm // RES_STRIDE) + step
                    rows = pl.ds(start_row, n_t2, stride=RES_STRIDE)
                    acc, den, _ = merged(pair, rows, *[x[t * n_t2:(t + 1) * n_t2] for x in new])
                    outs.append(acc / den)
                o_ref[pair, pl.ds(r, L, stride=g2_d), :] = jnp.concatenate(outs, axis=0)
            return carry
        lax.fori_loop(0, g2_d // RES_STRIDE, body, 0)


def _proj_attn(x, ada3, w_in, b_in, conv_w, w_pc):
    bsz, s, d = x.shape
    tm = ROW_TILE
    bias = jnp.asarray(_attn_bias_table())
    const = dict(pipeline_mode=pl.Buffered(1))
    row = lambda width: pl.BlockSpec((None, tm, width), lambda b, i: (b, i, 0))
    return pl.pallas_call(
        _proj_attn_kernel,
        grid=(bsz, s // tm),
        in_specs=[row(d),
                  pl.BlockSpec((None, 3, d), lambda b, i: (b, 0, 0)),
                  pl.BlockSpec((d, OFF_GA), lambda b, i: (0, 0), **const),
                  pl.BlockSpec((1, OFF_GA), lambda b, i: (0, 0), **const),
                  pl.BlockSpec((CONV_K, CONV_WIDTH), lambda b, i: (0, 0), **const),
                  pl.BlockSpec((CONV_WIDTH, d), lambda b, i: (0, 0), **const),
                  pl.BlockSpec(bias.shape, lambda b, i: (0, 0, 0, 0), **const)],
        out_specs=[row(GROUP_WIDTH), row(d),
                   pl.BlockSpec((None, N_PAIRS, s, PAIR_WIDTH), lambda b, i: (b, 0, 0, 0))],
        out_shape=[jax.ShapeDtypeStruct((bsz, s, GROUP_WIDTH), BF16),
                   jax.ShapeDtypeStruct((bsz, s, d), BF16),
                   jax.ShapeDtypeStruct((bsz, N_PAIRS, s, PAIR_WIDTH), F32)],
        scratch_shapes=[pltpu.VMEM((tm + HALO, CONV_WIDTH), F32),
                        pltpu.VMEM((tm, CONV_WIDTH), BF16),
                        pltpu.VMEM((N_STAGE_SLABS, tm, LANES), F32),
                        pltpu.VMEM((s, QKV_WIDTH), BF16),
                        pltpu.VMEM((3, N_PAIRS, s, PAIR_WIDTH), F32),
                        pltpu.VMEM((N_PAIRS * (tm // SUB_BLOCK), 3, SUB_BLOCK, PAIR_WIDTH), F32),
                        pltpu.VMEM((4, PAIR_WIDTH, 2 * SUB_BLOCK), BF16)],
        compiler_params=pltpu.CompilerParams(
            dimension_semantics=("arbitrary", "arbitrary"), vmem_limit_bytes=VMEM_LIMIT),
        name="proj_attn",
    )(x, ada3, w_in, b_in, conv_w, w_pc, bias)


def _out_kernel(x_ref, ada_ref, oa_ref, za_ref, yc_ref, wg_ref, bg_ref, wpa_ref, wout_ref, bout_ref,
                lng_ref, lnb_ref, out_ref, m_scr, *, alpha):
    shift = ada_ref[0:1, :]
    scale = ada_ref[1:2, :]
    gate = ada_ref[2:3, :] * (1.0 / alpha)
    eps = LN_EPS / (alpha * alpha)
    h = (x_ref[...] * (1.0 + scale) + shift).astype(BF16)
    o_attn = jnp.concatenate([oa_ref[pair] for pair in range(N_PAIRS)], axis=1)
    ta = (o_attn * za_ref[...].astype(F32)).astype(BF16)

    for c0 in range(0, D_MODEL, OUT_COL_CHUNK):
        cs = slice(c0, c0 + OUT_COL_CHUNK)
        gs = slice(D_MODEL + c0, D_MODEL + c0 + OUT_COL_CHUNK)
        g_a = _sigmoid(jnp.dot(h, wg_ref[:, cs], preferred_element_type=F32) + bg_ref[:, cs])
        g_b = _sigmoid(jnp.dot(h, wg_ref[:, gs], preferred_element_type=F32) + bg_ref[:, gs])
        y_attn = jnp.dot(ta, wpa_ref[:, cs], preferred_element_type=F32)
        m_scr[:, cs] = g_a.astype(BF16) * y_attn.astype(BF16) + g_b.astype(BF16) * yc_ref[:, cs]

    for r0 in range(0, x_ref.shape[0], OUT_CHUNK):
        rows = slice(r0, r0 + OUT_CHUNK)
        sub = gate * (jnp.dot(m_scr[rows, :], wout_ref[...], preferred_element_type=F32) + bout_ref[...])
        r = x_ref[rows, :] + sub
        mu = jnp.mean(r, axis=-1, keepdims=True)
        cen = r - mu
        var = jnp.mean(cen * cen, axis=-1, keepdims=True)
        out_ref[rows, :] = cen * lax.rsqrt(var + eps) * lng_ref[...] + lnb_ref[...]


def _out(x, ada3, o_attn, za, yc, w_gate, b_gate, w_pa, w_out, b_out, ln_g, ln_b, alpha):
    bsz, s, d = x.shape
    tm = OUT_ROW_TILE
    const = dict(pipeline_mode=pl.Buffered(1))
    row = lambda width: pl.BlockSpec((None, tm, width), lambda b, i: (b, i, 0))
    vec = pl.BlockSpec((1, d), lambda b, i: (0, 0), **const)
    return pl.pallas_call(
        functools.partial(_out_kernel, alpha=alpha),
        grid=(bsz, s // tm),
        in_specs=[row(d), pl.BlockSpec((None, 3, d), lambda b, i: (b, 0, 0)),
                  pl.BlockSpec((None, N_PAIRS, tm, PAIR_WIDTH), lambda b, i: (b, 0, i, 0)),
                  row(GROUP_WIDTH), row(d),
                  pl.BlockSpec((d, 2 * d), lambda b, i: (0, 0), **const),
                  pl.BlockSpec((1, 2 * d), lambda b, i: (0, 0), **const),
                  pl.BlockSpec((GROUP_WIDTH, d), lambda b, i: (0, 0), **const),
                  pl.BlockSpec((d, d), lambda b, i: (0, 0), **const),
                  vec, vec, vec],
        out_specs=row(d),
        out_shape=jax.ShapeDtypeStruct((bsz, s, d), F32),
        scratch_shapes=[pltpu.VMEM((tm, d), BF16)],
        compiler_params=pltpu.CompilerParams(
            dimension_semantics=("arbitrary", "arbitrary"), vmem_limit_bytes=VMEM_LIMIT),
        name="out",
    )(x, ada3, o_attn, za, yc, w_gate, b_gate, w_pa, w_out, b_out, ln_g, ln_b)


def kernel(x, c, w_ada, b_ada, w_in, b_in, conv_w, w_proj_attn, w_proj_conv, w_out, b_out, ln_g, ln_b):
    bsz, s, d = x.shape
    depth = w_in.shape[0]
    assert d == D_MODEL and s % OUT_ROW_TILE == 0 and w_in.shape[1:] == (D_MODEL, D_IN)
    assert s // DILATED_GROUPS[-1][1] == SUB_BLOCK
    alpha = (2.0 * depth) ** 0.25
    for layer in range(depth):
        ada3 = _ada(c, w_ada[layer], b_ada[layer]).reshape(bsz, 3, d)
        b_in2 = b_in[layer].reshape(1, D_IN)
        za, yc, o_attn = _proj_attn(x, ada3, w_in[layer, :, :OFF_GA].astype(BF16), b_in2[:, :OFF_GA],
                                    conv_w[layer], w_proj_conv[layer].astype(BF16))
        x = _out(x, ada3, o_attn, za, yc,
                 w_in[layer, :, OFF_GA:].astype(BF16), b_in2[:, OFF_GA:],
                 w_proj_attn[layer].astype(BF16), w_out[layer].astype(BF16),
                 b_out[layer].reshape(1, d), ln_g[layer].reshape(1, d), ln_b[layer].reshape(1, d), alpha)
    return x
```

```python
import functools
import itertools

import numpy as np
import jax
import jax.numpy as jnp
from jax import lax
from jax.experimental import pallas as pl
from jax.experimental.pallas import tpu as pltpu

D_MODEL = 1024
HEAD_DIM = 64
HEADS_PER_GROUP = 4
DILATED_GROUPS = ((128, 1), (512, 4), (2048, 16))
N_GROUPS = len(DILATED_GROUPS)
N_ATTN_HEADS = N_GROUPS * HEADS_PER_GROUP
ATTN_WIDTH = N_ATTN_HEADS * HEAD_DIM
GROUP_WIDTH = HEADS_PER_GROUP * HEAD_DIM
QKV_WIDTH = 3 * ATTN_WIDTH
CONV_WIDTH = D_MODEL
CONV_K = 3
SUB_BLOCK = 128
ALIBI_MAX_EXP = 8.0
LN_EPS = 1e-5

OFF_Z_ATTN = QKV_WIDTH
OFF_UX = OFF_Z_ATTN + GROUP_WIDTH
OFF_GB = OFF_UX + CONV_WIDTH
OFF_GC = OFF_GB + CONV_WIDTH
OFF_ZC = OFF_GC + CONV_WIDTH
OFF_GA = OFF_ZC + CONV_WIDTH
D_IN = OFF_GA + 2 * D_MODEL

ROW_TILE = 512
OUT_ROW_TILE = 2048
OUT_SLAB = 1024
OUT_CHUNK = 256
OUT_COL_CHUNK = 256
CONV_CHUNK = 256
LANES = 128
PAIR_WIDTH = 2 * HEAD_DIM
N_PAIRS = HEADS_PER_GROUP // 2
REGROUP_STRIDE = 4
RES_STRIDE = REGROUP_STRIDE
N_STAGE_SLABS = 6
HALO = 8
VMEM_LIMIT = 56 * 1024 * 1024

F32 = jnp.float32
BF16 = jnp.bfloat16

assert PAIR_WIDTH == LANES and ROW_TILE // DILATED_GROUPS[1][1] == SUB_BLOCK
assert DILATED_GROUPS[1][1] == RES_STRIDE and DILATED_GROUPS[2][1] == RES_STRIDE * RES_STRIDE
assert 2 * (CONV_WIDTH // CONV_CHUNK) == 2 * (ROW_TILE // SUB_BLOCK)


def _sigmoid(v):
    return 0.5 * jnp.tanh(0.5 * v) + 0.5


def _silu(v):
    return v * _sigmoid(v)


def _ada_kernel(c_ref, w_ref, b_ref, o_ref):
    o_ref[...] = jnp.dot(_silu(c_ref[...]).astype(BF16), w_ref[...].astype(BF16),
                         preferred_element_type=F32) + b_ref[...]


def _ada(c, w_ada, b_ada):
    bsz, d = c.shape
    n_out = w_ada.shape[1]
    return pl.pallas_call(
        _ada_kernel,
        grid=(n_out // d,),
        in_specs=[pl.BlockSpec((bsz, d), lambda j: (0, 0)),
                  pl.BlockSpec((d, d), lambda j: (0, j)),
                  pl.BlockSpec((1, d), lambda j: (0, j))],
        out_specs=pl.BlockSpec((bsz, d), lambda j: (0, j)),
        out_shape=jax.ShapeDtypeStruct((bsz, n_out), F32),
        name="ada",
    )(c, w_ada, b_ada.reshape(1, n_out))


def _attn_bias_table():
    L = SUB_BLOCK
    tabs = []
    for group, (window, dilation) in enumerate(DILATED_GROUPS):
        span = window // dilation
        heads = np.arange(group * HEADS_PER_GROUP, (group + 1) * HEADS_PER_GROUP, dtype=np.float64)
        slopes = 2.0 ** (-ALIBI_MAX_EXP * (heads + 1.0) / N_ATTN_HEADS)
        delta = (np.arange(L)[:, None] + L - np.arange(2 * L)[None, :]).astype(np.float64)
        valid = (delta >= 0) & (delta <= span)
        bias = np.where(valid[None], -slopes[:, None, None] * (delta * dilation)[None], -np.inf)
        tabs.append(bias.reshape(HEADS_PER_GROUP // 2, 2 * L, 2 * L))
    return np.stack(tabs).astype(np.float32)


def _proj_attn_kernel(x_ref, ada_ref, w_ref, b_ref, cw_ref, wpc_ref, bias_ref,
                      za_ref, yc_ref, o_ref, u_scr, t_scr, stage_scr, qkv_scr, res, tmp_scr, kt_scr):
    tm = x_ref.shape[0]
    S = qkv_scr.shape[0]
    L = SUB_BLOCK
    tile = pl.program_id(1)
    first_tile = tile == 0
    base = pl.multiple_of(tile * tm, tm)
    prev_base = pl.multiple_of(jnp.maximum(tile - 1, 0) * tm, tm)

    @pl.when(first_tile)
    def _():
        u_scr[0:HALO, :] = jnp.zeros((HALO, CONV_WIDTH), F32)

    @pl.when(jnp.logical_not(first_tile))
    def _():
        u_scr[0:HALO, :] = u_scr[tm:tm + HALO, :]

    shift = ada_ref[0:1, :]
    scale = ada_ref[1:2, :]
    h = (x_ref[...] * (1.0 + scale) + shift).astype(BF16)

    def proj(lo, width):
        return (jnp.dot(h, w_ref[:, lo:lo + width], preferred_element_type=F32)
                + b_ref[:, lo:lo + width])

    slabs = itertools.cycle(range(N_STAGE_SLABS))

    def project_qkv(g):
        d = DILATED_GROUPS[g][1]
        for which in range(3):
            c0 = which * ATTN_WIDTH + g * GROUP_WIDTH
            blk = proj(c0, GROUP_WIDTH)
            if d == 1:
                qkv_scr[pl.ds(base, tm), c0:c0 + GROUP_WIDTH] = blk.astype(BF16)
                continue
            n_t = tm // d
            for lo in range(0, GROUP_WIDTH, LANES):
                cols = slice(c0 + lo, c0 + lo + LANES)
                slab = next(slabs)
                stage_scr[slab] = blk[:, lo:lo + LANES]
                if d == REGROUP_STRIDE:
                    for r in range(d):
                        qkv_scr[pl.ds(base + r * n_t, n_t), cols] = (
                            stage_scr[slab, pl.ds(r, n_t, stride=d), :].astype(BF16))
                else:
                    assert d == REGROUP_STRIDE * REGROUP_STRIDE
                    n_1 = tm // REGROUP_STRIDE
                    slab2 = next(slabs)
                    for r1 in range(REGROUP_STRIDE):
                        stage_scr[slab2, r1 * n_1:(r1 + 1) * n_1, :] = (
                            stage_scr[slab, pl.ds(r1, n_1, stride=REGROUP_STRIDE), :])
                    for r1 in range(REGROUP_STRIDE):
                        for r2 in range(REGROUP_STRIDE):
                            r = r2 * REGROUP_STRIDE + r1
                            qkv_scr[pl.ds(base + r * n_t, n_t), cols] = stage_scr[
                                slab2, pl.ds(r1 * n_1 + r2, n_t, stride=REGROUP_STRIDE), :].astype(BF16)

    project_qkv(1)
    project_qkv(0)

    low_half = lax.broadcasted_iota(jnp.int32, (L, PAIR_WIDTH), 1) < HEAD_DIM
    no_prev = jnp.logical_and(lax.broadcasted_iota(jnp.int32, (2 * L, 2 * L), 1) < L, first_tile)

    def col(g, which, pair):
        return which * ATTN_WIDTH + g * GROUP_WIDTH + pair * PAIR_WIDTH

    kt_slots = itertools.cycle(range(kt_scr.shape[0]))

    def probabilities(g, pair, q, k, maybe_no_prev):
        q = q * 0.125
        zero = jnp.zeros_like(q)
        q2 = jnp.concatenate([jnp.where(low_half, q, zero), jnp.where(low_half, zero, q)], axis=0)
        slot = next(kt_slots)
        n_keys = k.shape[0]
        kt_scr[slot, :, :n_keys] = k.T
        sc = jnp.dot(q2, kt_scr[slot, :, :n_keys], preferred_element_type=F32)
        sc = sc + bias_ref[g, pair, :, 2 * L - k.shape[0]:]
        if maybe_no_prev:
            sc = jnp.where(no_prev, -jnp.inf, sc)
        m = jnp.max(sc, axis=-1, keepdims=True)
        return jnp.exp(sc - m).astype(BF16), m

    def weighted_values(p, m, v):
        v1 = jnp.concatenate([v, jnp.ones(v.shape, BF16)], axis=1)
        pv = jnp.dot(p, v1, preferred_element_type=F32)
        mb = jnp.broadcast_to(m, (2 * L, PAIR_WIDTH))
        return (jnp.where(low_half, pv[:L, :PAIR_WIDTH], pv[L:, :PAIR_WIDTH]),
                jnp.where(low_half, pv[:L, PAIR_WIDTH:], pv[L:, PAIR_WIDTH:]),
                jnp.where(low_half, mb[:L], mb[L:]))

    def merged(pair, rows, acc, den, m):
        m_old = res[2, pair, rows, :]
        m_new = jnp.maximum(m_old, m)
        a = jnp.exp(m_old - m_new)
        b = jnp.exp(m - m_new)
        return a * res[0, pair, rows, :] + b * acc, a * res[1, pair, rows, :] + b * den, m_new

    g1_d = DILATED_GROUPS[1][1]

    def g1_scores(r):
        out = []
        for pair in range(N_PAIRS):
            cur = pl.ds(base + r * L, L)
            prev = pl.ds(prev_base + r * L, L)
            k = jnp.concatenate([qkv_scr[prev, pl.ds(col(1, 1, pair), PAIR_WIDTH)],
                                 qkv_scr[cur, pl.ds(col(1, 1, pair), PAIR_WIDTH)]], axis=0)
            out.append(probabilities(1, pair, qkv_scr[cur, pl.ds(col(1, 0, pair), PAIR_WIDTH)], k, True))
        return out

    def g1_finish(r, parts):
        for pair, (p, m) in enumerate(parts):
            cur = pl.ds(base + r * L, L)
            prev = pl.ds(prev_base + r * L, L)
            v = jnp.concatenate([qkv_scr[prev, pl.ds(col(1, 2, pair), PAIR_WIDTH)],
                                 qkv_scr[cur, pl.ds(col(1, 2, pair), PAIR_WIDTH)]], axis=0)
            acc, den, mm = weighted_values(p, m, v)
            rows = pl.ds(base + r * L, L)
            res[0, pair, rows, :] = acc
            res[1, pair, rows, :] = den
            res[2, pair, rows, :] = mm

    def g0_rows(c):
        q0 = base + c * L
        k0 = jnp.maximum(q0 - L, 0) if c == 0 else q0 - L
        return pl.ds(pl.multiple_of(q0, L), L), pl.ds(pl.multiple_of(k0, L), L)

    def g0_scores(c):
        cur, prev = g0_rows(c)
        out = []
        for pair in range(N_PAIRS):
            k = jnp.concatenate([qkv_scr[prev, pl.ds(col(0, 1, pair), PAIR_WIDTH)],
                                 qkv_scr[cur, pl.ds(col(0, 1, pair), PAIR_WIDTH)]], axis=0)
            out.append(probabilities(0, pair, qkv_scr[cur, pl.ds(col(0, 0, pair), PAIR_WIDTH)], k, c == 0))
        return out

    def g0_finish(c, parts):
        cur, prev = g0_rows(c)
        for pair, (p, m) in enumerate(parts):
            v = jnp.concatenate([qkv_scr[prev, pl.ds(col(0, 2, pair), PAIR_WIDTH)],
                                 qkv_scr[cur, pl.ds(col(0, 2, pair), PAIR_WIDTH)]], axis=0)
            slab = c * N_PAIRS + pair
            for k, val in enumerate(weighted_values(p, m, v)):
                tmp_scr[slab, k] = val
            n_r = L // RES_STRIDE
            for r in range(RES_STRIDE):
                rows = pl.ds(pl.multiple_of(base + r * (tm // RES_STRIDE) + c * n_r, n_r), n_r)
                acc, den, mm = merged(pair, rows, *[tmp_scr[slab, k, pl.ds(r, n_r, stride=RES_STRIDE), :]
                                                    for k in range(3)])
                res[0, pair, rows, :] = acc
                res[1, pair, rows, :] = den
                res[2, pair, rows, :] = mm

    items = ([(g1_scores, g1_finish, (j,)) for j in range(tm // L)]
             + [(g0_scores, g0_finish, (j,)) for j in range(tm // L)])

    def start(item):
        scores_fn, finish_fn, args = item
        return finish_fn, [(a, scores_fn(a)) for a in args]

    def finish(pending):
        finish_fn, parts = pending
        for a, part in parts:
            finish_fn(a, part)

    in_flight = []
    pending_items = list(items)

    def rotate():
        in_flight.append(start(pending_items.pop(0)))
        if len(in_flight) > 2:
            finish(in_flight.pop(0))

    for c0 in range(0, CONV_WIDTH, CONV_CHUNK):
        cs = slice(c0, c0 + CONV_CHUNK)
        u = proj(OFF_GC + c0, CONV_CHUNK) * proj(OFF_UX + c0, CONV_CHUNK)
        u_scr[HALO:HALO + tm, cs] = u
        conv = cw_ref[0:1, cs] * u_scr[HALO - 2:HALO - 2 + tm, cs]
        conv = conv + cw_ref[1:2, cs] * u_scr[HALO - 1:HALO - 1 + tm, cs]
        conv = conv + cw_ref[2:3, cs] * u
        rotate()
        t = proj(OFF_GB + c0, CONV_CHUNK) * conv * _silu(proj(OFF_ZC + c0, CONV_CHUNK))
        t_scr[:, cs] = t.astype(BF16)
        rotate()
    finish(in_flight.pop(0))
    za_ref[...] = _silu(proj(OFF_Z_ATTN, GROUP_WIDTH)).astype(BF16)
    yc_ref[...] = jnp.dot(t_scr[...], wpc_ref[...], preferred_element_type=F32).astype(BF16)
    finish(in_flight.pop(0))
    project_qkv(2)

    g2_d = DILATED_GROUPS[2][1]
    n_t2 = tm // g2_d

    def g2_piece(which, r, pair):
        pieces = [qkv_scr[pl.ds(pl.multiple_of(t * tm + r * n_t2, n_t2), n_t2),
                          pl.ds(col(2, which, pair), PAIR_WIDTH)] for t in range(S // tm)]
        return jnp.concatenate(pieces, axis=0)

    @pl.when(tile == S // tm - 1)
    def _():
        def body(step, carry):
            work = [(u, pair) for u in range(RES_STRIDE) for pair in range(N_PAIRS)]
            parts = [probabilities(2, pair, g2_piece(0, step * RES_STRIDE + u, pair),
                                   g2_piece(1, step * RES_STRIDE + u, pair), False) for u, pair in work]
            for (u, pair), (p, m) in zip(work, parts):
                r = step * RES_STRIDE + u
                new = weighted_values(p, m, g2_piece(2, r, pair))
                outs = []
                for t in range(S // tm):
                    start_row = t * tm + u * (tm // RES_STRIDE) + step
                    rows = pl.ds(start_row, n_t2, stride=RES_STRIDE)
                    acc, den, _ = merged(pair, rows, *[x[t * n_t2:(t + 1) * n_t2] for x in new])
                    outs.append(acc / den)
                o_ref[pair, pl.ds(r, L, stride=g2_d), :] = jnp.concatenate(outs, axis=0)
            return carry
        lax.fori_loop(0, g2_d // RES_STRIDE, body, 0)


def _proj_attn(x, ada3, w_in, b_in, conv_w, w_pc):
    bsz, s, d = x.shape
    tm = ROW_TILE
    bias = jnp.asarray(_attn_bias_table())
    const = dict(pipeline_mode=pl.Buffered(1))
    row = lambda width: pl.BlockSpec((None, tm, width), lambda b, i: (b, i, 0))
    return pl.pallas_call(
        _proj_attn_kernel,
        grid=(bsz, s // tm),
        in_specs=[row(d),
                  pl.BlockSpec((None, 3, d), lambda b, i: (b, 0, 0)),
                  pl.BlockSpec((d, OFF_GA), lambda b, i: (0, 0), **const),
                  pl.BlockSpec((1, OFF_GA), lambda b, i: (0, 0), **const),
                  pl.BlockSpec((CONV_K, CONV_WIDTH), lambda b, i: (0, 0), **const),
                  pl.BlockSpec((CONV_WIDTH, d), lambda b, i: (0, 0), **const),
                  pl.BlockSpec(bias.shape, lambda b, i: (0, 0, 0, 0), **const)],
        out_specs=[row(GROUP_WIDTH), row(d),
                   pl.BlockSpec((None, N_PAIRS, s, PAIR_WIDTH), lambda b, i: (b, 0, 0, 0))],
        out_shape=[jax.ShapeDtypeStruct((bsz, s, GROUP_WIDTH), BF16),
                   jax.ShapeDtypeStruct((bsz, s, d), BF16),
                   jax.ShapeDtypeStruct((bsz, N_PAIRS, s, PAIR_WIDTH), F32)],
        scratch_shapes=[pltpu.VMEM((tm + HALO, CONV_WIDTH), F32),
                        pltpu.VMEM((tm, CONV_WIDTH), BF16),
                        pltpu.VMEM((N_STAGE_SLABS, tm, LANES), F32),
                        pltpu.VMEM((s, QKV_WIDTH), BF16),
                        pltpu.VMEM((3, N_PAIRS, s, PAIR_WIDTH), F32),
                        pltpu.VMEM((N_PAIRS * (tm // SUB_BLOCK), 3, SUB_BLOCK, PAIR_WIDTH), F32),
                        pltpu.VMEM((4, PAIR_WIDTH, 2 * SUB_BLOCK), BF16)],
        compiler_params=pltpu.CompilerParams(
            dimension_semantics=("arbitrary", "arbitrary"), vmem_limit_bytes=VMEM_LIMIT),
        name="proj_attn",
    )(x, ada3, w_in, b_in, conv_w, w_pc, bias)


def _out_kernel(x_ref, ada_ref, oa_ref, za_ref, yc_ref, wg_ref, bg_ref, wpa_ref, wout_ref, bout_ref,
                lng_ref, lnb_ref, out_ref, m_scr, *, alpha):
    shift = ada_ref[0:1, :]
    scale = ada_ref[1:2, :]
    gate = ada_ref[2:3, :] * (1.0 / alpha)
    eps = LN_EPS / (alpha * alpha)
    for s0 in range(0, x_ref.shape[0], OUT_SLAB):
        slab = slice(s0, s0 + OUT_SLAB)
        m_slot = m_scr
        h = (x_ref[slab, :] * (1.0 + scale) + shift).astype(BF16)
        o_attn = jnp.concatenate([oa_ref[pair, slab, :] for pair in range(N_PAIRS)], axis=1)
        ta = (o_attn * za_ref[slab, :].astype(F32)).astype(BF16)

        for c0 in range(0, D_MODEL, OUT_COL_CHUNK):
            cs = slice(c0, c0 + OUT_COL_CHUNK)
            gs = slice(D_MODEL + c0, D_MODEL + c0 + OUT_COL_CHUNK)
            g_a = _sigmoid(jnp.dot(h, wg_ref[:, cs], preferred_element_type=F32) + bg_ref[:, cs])
            g_b = _sigmoid(jnp.dot(h, wg_ref[:, gs], preferred_element_type=F32) + bg_ref[:, gs])
            y_attn = jnp.dot(ta, wpa_ref[:, cs], preferred_element_type=F32)
            m_slot[:, cs] = g_a.astype(BF16) * y_attn.astype(BF16) + g_b.astype(BF16) * yc_ref[slab, cs]

        for r0 in range(0, OUT_SLAB, OUT_CHUNK):
            rows = slice(s0 + r0, s0 + r0 + OUT_CHUNK)
            sub = gate * (jnp.dot(m_slot[r0:r0 + OUT_CHUNK, :], wout_ref[...],
                                  preferred_element_type=F32) + bout_ref[...])
            r = x_ref[rows, :] + sub
            mu = jnp.mean(r, axis=-1, keepdims=True)
            cen = r - mu
            var = jnp.mean(cen * cen, axis=-1, keepdims=True)
            out_ref[rows, :] = cen * lax.rsqrt(var + eps) * lng_ref[...] + lnb_ref[...]


def _out(x, ada3, o_attn, za, yc, w_gate, b_gate, w_pa, w_out, b_out, ln_g, ln_b, alpha):
    bsz, s, d = x.shape
    tm = OUT_ROW_TILE
    const = dict(pipeline_mode=pl.Buffered(1))
    row = lambda width: pl.BlockSpec((None, tm, width), lambda b, i: (b, i, 0))
    vec = pl.BlockSpec((1, d), lambda b, i: (0, 0), **const)
    return pl.pallas_call(
        functools.partial(_out_kernel, alpha=alpha),
        grid=(bsz, s // tm),
        in_specs=[row(d), pl.BlockSpec((None, 3, d), lambda b, i: (b, 0, 0)),
                  pl.BlockSpec((None, N_PAIRS, tm, PAIR_WIDTH), lambda b, i: (b, 0, i, 0)),
                  row(GROUP_WIDTH), row(d),
                  pl.BlockSpec((d, 2 * d), lambda b, i: (0, 0), **const),
                  pl.BlockSpec((1, 2 * d), lambda b, i: (0, 0), **const),
                  pl.BlockSpec((GROUP_WIDTH, d), lambda b, i: (0, 0), **const),
                  pl.BlockSpec((d, d), lambda b, i: (0, 0), **const),
                  vec, vec, vec],
        out_specs=row(d),
        out_shape=jax.ShapeDtypeStruct((bsz, s, d), F32),
        scratch_shapes=[pltpu.VMEM((OUT_SLAB, d), BF16)],
        compiler_params=pltpu.CompilerParams(
            dimension_semantics=("arbitrary", "arbitrary"), vmem_limit_bytes=62 * 1024 * 1024),
        name="out",
    )(x, ada3, o_attn, za, yc, w_gate, b_gate, w_pa, w_out, b_out, ln_g, ln_b)


def kernel(x, c, w_ada, b_ada, w_in, b_in, conv_w, w_proj_attn, w_proj_conv, w_out, b_out, ln_g, ln_b):
    bsz, s, d = x.shape
    depth = w_in.shape[0]
    assert d == D_MODEL and s % OUT_ROW_TILE == 0 and w_in.shape[1:] == (D_MODEL, D_IN)
    assert s // DILATED_GROUPS[-1][1] == SUB_BLOCK
    alpha = (2.0 * depth) ** 0.25
    for layer in range(depth):
        ada3 = _ada(c, w_ada[layer], b_ada[layer]).reshape(bsz, 3, d)
        b_in2 = b_in[layer].reshape(1, D_IN)
        za, yc, o_attn = _proj_attn(x, ada3, w_in[layer, :, :OFF_GA].astype(BF16), b_in2[:, :OFF_GA],
                                    conv_w[layer], w_proj_conv[layer].astype(BF16))
        x = _out(x, ada3, o_attn, za, yc,
                 w_in[layer, :, OFF_GA:].astype(BF16), b_in2[:, OFF_GA:],
                 w_proj_attn[layer].astype(BF16), w_out[layer].astype(BF16),
                 b_out[layer].reshape(1, d), ln_g[layer].reshape(1, d), ln_b[layer].reshape(1, d), alpha)
    return x
```

```python
import functools
import itertools

import numpy as np
import jax
import jax.numpy as jnp
from jax import lax
from jax.experimental import pallas as pl
from jax.experimental.pallas import tpu as pltpu

D_MODEL = 1024
HEAD_DIM = 64
HEADS_PER_GROUP = 4
DILATED_GROUPS = ((128, 1), (512, 4), (2048, 16))
N_GROUPS = len(DILATED_GROUPS)
N_ATTN_HEADS = N_GROUPS * HEADS_PER_GROUP
ATTN_WIDTH = N_ATTN_HEADS * HEAD_DIM
GROUP_WIDTH = HEADS_PER_GROUP * HEAD_DIM
QKV_WIDTH = 3 * ATTN_WIDTH
CONV_WIDTH = D_MODEL
CONV_K = 3
SUB_BLOCK = 128
ALIBI_MAX_EXP = 8.0
LN_EPS = 1e-5

OFF_Z_ATTN = QKV_WIDTH
OFF_UX = OFF_Z_ATTN + GROUP_WIDTH
OFF_GB = OFF_UX + CONV_WIDTH
OFF_GC = OFF_GB + CONV_WIDTH
OFF_ZC = OFF_GC + CONV_WIDTH
OFF_GA = OFF_ZC + CONV_WIDTH
D_IN = OFF_GA + 2 * D_MODEL

ROW_TILE = 512
OUT_ROW_TILE = 1024
OUT_CHUNK = 256
OUT_COL_CHUNK = 256
OUT_WG_BLOCK = 512
CONV_CHUNK = 256
LANES = 128
PAIR_WIDTH = 2 * HEAD_DIM
N_PAIRS = HEADS_PER_GROUP // 2
REGROUP_STRIDE = 4
RES_STRIDE = REGROUP_STRIDE
N_STAGE_SLABS = 6
HALO = 8
VMEM_LIMIT = 56 * 1024 * 1024

F32 = jnp.float32
BF16 = jnp.bfloat16

assert PAIR_WIDTH == LANES and ROW_TILE // DILATED_GROUPS[1][1] == SUB_BLOCK
assert DILATED_GROUPS[1][1] == RES_STRIDE and DILATED_GROUPS[2][1] == RES_STRIDE * RES_STRIDE
assert 2 * (CONV_WIDTH // CONV_CHUNK) == 2 * (ROW_TILE // SUB_BLOCK)
assert OFF_GA % OUT_WG_BLOCK == 0


def _sigmoid(v):
    return 0.5 * jnp.tanh(0.5 * v) + 0.5


def _silu(v):
    return v * _sigmoid(v)


def _ada_kernel(c_ref, w_ref, b_ref, o_ref):
    o_ref[...] = jnp.dot(_silu(c_ref[...]).astype(BF16), w_ref[...].astype(BF16),
                         preferred_element_type=F32) + b_ref[...]


def _ada(c, w_ada, b_ada):
    bsz, d = c.shape
    n_out = w_ada.shape[1]
    return pl.pallas_call(
        _ada_kernel,
        grid=(n_out // d,),
        in_specs=[pl.BlockSpec((bsz, d), lambda j: (0, 0)),
                  pl.BlockSpec((d, d), lambda j: (0, j)),
                  pl.BlockSpec((1, d), lambda j: (0, j))],
        out_specs=pl.BlockSpec((bsz, d), lambda j: (0, j)),
        out_shape=jax.ShapeDtypeStruct((bsz, n_out), F32),
        name="ada",
    )(c, w_ada, b_ada.reshape(1, n_out))


def _attn_bias_table():
    L = SUB_BLOCK
    tabs = []
    for group, (window, dilation) in enumerate(DILATED_GROUPS):
        span = window // dilation
        heads = np.arange(group * HEADS_PER_GROUP, (group + 1) * HEADS_PER_GROUP, dtype=np.float64)
        slopes = 2.0 ** (-ALIBI_MAX_EXP * (heads + 1.0) / N_ATTN_HEADS)
        delta = (np.arange(L)[:, None] + L - np.arange(2 * L)[None, :]).astype(np.float64)
        valid = (delta >= 0) & (delta <= span)
        bias = np.where(valid[None], -slopes[:, None, None] * (delta * dilation)[None], -np.inf)
        tabs.append(bias.reshape(HEADS_PER_GROUP // 2, 2 * L, 2 * L))
    return np.stack(tabs).astype(np.float32)


def _proj_attn_kernel(x_ref, ada_ref, w_ref, b_ref, cw_ref, wpc_ref, bias_ref,
                      za_ref, yc_ref, o_ref, u_scr, t_scr, stage_scr, qkv_scr, res, tmp_scr, kt_scr):
    tm = x_ref.shape[0]
    S = qkv_scr.shape[0]
    L = SUB_BLOCK
    tile = pl.program_id(1)
    first_tile = tile == 0
    base = pl.multiple_of(tile * tm, tm)
    prev_base = pl.multiple_of(jnp.maximum(tile - 1, 0) * tm, tm)

    @pl.when(first_tile)
    def _():
        u_scr[0:HALO, :] = jnp.zeros((HALO, CONV_WIDTH), F32)

    @pl.when(jnp.logical_not(first_tile))
    def _():
        u_scr[0:HALO, :] = u_scr[tm:tm + HALO, :]

    ada = ada_ref[pl.ds(pl.program_id(0), 1), :]
    shift = ada[:, 0:D_MODEL]
    scale = ada[:, D_MODEL:2 * D_MODEL]
    h = (x_ref[...] * (1.0 + scale) + shift).astype(BF16)

    def proj(lo, width):
        return (jnp.dot(h, w_ref[:, lo:lo + width], preferred_element_type=F32)
                + b_ref[:, lo:lo + width])

    slabs = itertools.cycle(range(N_STAGE_SLABS))

    def project_qkv(g):
        d = DILATED_GROUPS[g][1]
        for which in range(3):
            c0 = which * ATTN_WIDTH + g * GROUP_WIDTH
            blk = proj(c0, GROUP_WIDTH)
            if d == 1:
                qkv_scr[pl.ds(base, tm), c0:c0 + GROUP_WIDTH] = blk.astype(BF16)
                continue
            n_t = tm // d
            for lo in range(0, GROUP_WIDTH, LANES):
                cols = slice(c0 + lo, c0 + lo + LANES)
                slab = next(slabs)
                stage_scr[slab] = blk[:, lo:lo + LANES]
                if d == REGROUP_STRIDE:
                    for r in range(d):
                        qkv_scr[pl.ds(base + r * n_t, n_t), cols] = (
                            stage_scr[slab, pl.ds(r, n_t, stride=d), :].astype(BF16))
                else:
                    assert d == REGROUP_STRIDE * REGROUP_STRIDE
                    n_1 = tm // REGROUP_STRIDE
                    slab2 = next(slabs)
                    for r1 in range(REGROUP_STRIDE):
                        stage_scr[slab2, r1 * n_1:(r1 + 1) * n_1, :] = (
                            stage_scr[slab, pl.ds(r1, n_1, stride=REGROUP_STRIDE), :])
                    for r1 in range(REGROUP_STRIDE):
                        for r2 in range(REGROUP_STRIDE):
                            r = r2 * REGROUP_STRIDE + r1
                            qkv_scr[pl.ds(base + r * n_t, n_t), cols] = stage_scr[
                                slab2, pl.ds(r1 * n_1 + r2, n_t, stride=REGROUP_STRIDE), :].astype(BF16)

    project_qkv(1)
    project_qkv(0)

    low_half = lax.broadcasted_iota(jnp.int32, (L, PAIR_WIDTH), 1) < HEAD_DIM
    no_prev = jnp.logical_and(lax.broadcasted_iota(jnp.int32, (2 * L, 2 * L), 1) < L, first_tile)

    def col(g, which, pair):
        return which * ATTN_WIDTH + g * GROUP_WIDTH + pair * PAIR_WIDTH

    kt_slots = itertools.cycle(range(kt_scr.shape[0]))

    def probabilities(g, pair, q, k, maybe_no_prev):
        q = q * 0.125
        zero = jnp.zeros_like(q)
        q2 = jnp.concatenate([jnp.where(low_half, q, zero), jnp.where(low_half, zero, q)], axis=0)
        slot = next(kt_slots)
        n_keys = k.shape[0]
        kt_scr[slot, :, :n_keys] = k.T
        sc = jnp.dot(q2, kt_scr[slot, :, :n_keys], preferred_element_type=F32)
        sc = sc + bias_ref[g, pair, :, 2 * L - k.shape[0]:]
        if maybe_no_prev:
            sc = jnp.where(no_prev, -jnp.inf, sc)
        m = jnp.max(sc, axis=-1, keepdims=True)
        return jnp.exp(sc - m).astype(BF16), m

    def weighted_values(p, m, v):
        v1 = jnp.concatenate([v, jnp.ones(v.shape, BF16)], axis=1)
        pv = jnp.dot(p, v1, preferred_element_type=F32)
        mb = jnp.broadcast_to(m, (2 * L, PAIR_WIDTH))
        return (jnp.where(low_half, pv[:L, :PAIR_WIDTH], pv[L:, :PAIR_WIDTH]),
                jnp.where(low_half, pv[:L, PAIR_WIDTH:], pv[L:, PAIR_WIDTH:]),
                jnp.where(low_half, mb[:L], mb[L:]))

    def merged(pair, rows, acc, den, m):
        m_old = res[2, pair, rows, :]
        m_new = jnp.maximum(m_old, m)
        a = jnp.exp(m_old - m_new)
        b = jnp.exp(m - m_new)
        return a * res[0, pair, rows, :] + b * acc, a * res[1, pair, rows, :] + b * den, m_new

    g1_d = DILATED_GROUPS[1][1]

    def g1_scores(r):
        out = []
        for pair in range(N_PAIRS):
            cur = pl.ds(base + r * L, L)
            prev = pl.ds(prev_base + r * L, L)
            k = jnp.concatenate([qkv_scr[prev, pl.ds(col(1, 1, pair), PAIR_WIDTH)],
                                 qkv_scr[cur, pl.ds(col(1, 1, pair), PAIR_WIDTH)]], axis=0)
            out.append(probabilities(1, pair, qkv_scr[cur, pl.ds(col(1, 0, pair), PAIR_WIDTH)], k, True))
        return out

    def g1_finish(r, parts):
        for pair, (p, m) in enumerate(parts):
            cur = pl.ds(base + r * L, L)
            prev = pl.ds(prev_base + r * L, L)
            v = jnp.concatenate([qkv_scr[prev, pl.ds(col(1, 2, pair), PAIR_WIDTH)],
                                 qkv_scr[cur, pl.ds(col(1, 2, pair), PAIR_WIDTH)]], axis=0)
            acc, den, mm = weighted_values(p, m, v)
            rows = pl.ds(base + r * L, L)
            res[0, pair, rows, :] = acc
            res[1, pair, rows, :] = den
            res[2, pair, rows, :] = mm

    def g0_rows(c):
        q0 = base + c * L
        k0 = jnp.maximum(q0 - L, 0) if c == 0 else q0 - L
        return pl.ds(pl.multiple_of(q0, L), L), pl.ds(pl.multiple_of(k0, L), L)

    def g0_scores(c):
        cur, prev = g0_rows(c)
        out = []
        for pair in range(N_PAIRS):
            k = jnp.concatenate([qkv_scr[prev, pl.ds(col(0, 1, pair), PAIR_WIDTH)],
                                 qkv_scr[cur, pl.ds(col(0, 1, pair), PAIR_WIDTH)]], axis=0)
            out.append(probabilities(0, pair, qkv_scr[cur, pl.ds(col(0, 0, pair), PAIR_WIDTH)], k, c == 0))
        return out

    def g0_finish(c, parts):
        cur, prev = g0_rows(c)
        for pair, (p, m) in enumerate(parts):
            v = jnp.concatenate([qkv_scr[prev, pl.ds(col(0, 2, pair), PAIR_WIDTH)],
                                 qkv_scr[cur, pl.ds(col(0, 2, pair), PAIR_WIDTH)]], axis=0)
            slab = c * N_PAIRS + pair
            for k, val in enumerate(weighted_values(p, m, v)):
                tmp_scr[slab, k] = val
            n_r = L // RES_STRIDE
            for r in range(RES_STRIDE):
                rows = pl.ds(pl.multiple_of(base + r * (tm // RES_STRIDE) + c * n_r, n_r), n_r)
                acc, den, mm = merged(pair, rows, *[tmp_scr[slab, k, pl.ds(r, n_r, stride=RES_STRIDE), :]
                                                    for k in range(3)])
                res[0, pair, rows, :] = acc
                res[1, pair, rows, :] = den
                res[2, pair, rows, :] = mm

    items = ([(g1_scores, g1_finish, (j,)) for j in range(tm // L)]
             + [(g0_scores, g0_finish, (j,)) for j in range(tm // L)])

    def start(item):
        scores_fn, finish_fn, args = item
        return finish_fn, [(a, scores_fn(a)) for a in args]

    def finish(pending):
        finish_fn, parts = pending
        for a, part in parts:
            finish_fn(a, part)

    in_flight = []
    pending_items = list(items)

    def rotate():
        in_flight.append(start(pending_items.pop(0)))
        if len(in_flight) > 2:
            finish(in_flight.pop(0))

    for c0 in range(0, CONV_WIDTH, CONV_CHUNK):
        cs = slice(c0, c0 + CONV_CHUNK)
        u = proj(OFF_GC + c0, CONV_CHUNK) * proj(OFF_UX + c0, CONV_CHUNK)
        u_scr[HALO:HALO + tm, cs] = u
        conv = cw_ref[0:1, cs] * u_scr[HALO - 2:HALO - 2 + tm, cs]
        conv = conv + cw_ref[1:2, cs] * u_scr[HALO - 1:HALO - 1 + tm, cs]
        conv = conv + cw_ref[2:3, cs] * u
        rotate()
        t = proj(OFF_GB + c0, CONV_CHUNK) * conv * _silu(proj(OFF_ZC + c0, CONV_CHUNK))
        t_scr[:, cs] = t.astype(BF16)
        rotate()
    finish(in_flight.pop(0))
    za_ref[...] = _silu(proj(OFF_Z_ATTN, GROUP_WIDTH)).astype(BF16)
    yc_ref[...] = jnp.dot(t_scr[...], wpc_ref[...], preferred_element_type=F32).astype(BF16)
    finish(in_flight.pop(0))
    project_qkv(2)

    g2_d = DILATED_GROUPS[2][1]
    n_t2 = tm // g2_d

    def g2_piece(which, r, pair):
        pieces = [qkv_scr[pl.ds(pl.multiple_of(t * tm + r * n_t2, n_t2), n_t2),
                          pl.ds(col(2, which, pair), PAIR_WIDTH)] for t in range(S // tm)]
        return jnp.concatenate(pieces, axis=0)

    @pl.when(tile == S // tm - 1)
    def _():
        def body(step, carry):
            work = [(u, pair) for u in range(RES_STRIDE) for pair in range(N_PAIRS)]
            parts = [probabilities(2, pair, g2_piece(0, step * RES_STRIDE + u, pair),
                                   g2_piece(1, step * RES_STRIDE + u, pair), False) for u, pair in work]
            for (u, pair), (p, m) in zip(work, parts):
                r = step * RES_STRIDE + u
                new = weighted_values(p, m, g2_piece(2, r, pair))
                outs = []
                for t in range(S // tm):
                    start_row = t * tm + u * (tm // RES_STRIDE) + step
                    rows = pl.ds(start_row, n_t2, stride=RES_STRIDE)
                    acc, den, _ = merged(pair, rows, *[x[t * n_t2:(t + 1) * n_t2] for x in new])
                    outs.append(acc / den)
                o_ref[pair, pl.ds(r, L, stride=g2_d), :] = jnp.concatenate(outs, axis=0)
            return carry
        lax.fori_loop(0, g2_d // RES_STRIDE, body, 0)


def _proj_attn(x, ada, w_in, b_in, conv_w, w_pc):
    bsz, s, d = x.shape
    tm = ROW_TILE
    bias = jnp.asarray(_attn_bias_table())
    const = dict(pipeline_mode=pl.Buffered(1))
    row = lambda width: pl.BlockSpec((None, tm, width), lambda b, i: (b, i, 0))
    return pl.pallas_call(
        _proj_attn_kernel,
        grid=(bsz, s // tm),
        in_specs=[row(d),
                  pl.BlockSpec(ada.shape, lambda b, i: (0, 0), **const),
                  pl.BlockSpec((d, OFF_GA), lambda b, i: (0, 0), **const),
                  pl.BlockSpec((1, D_IN), lambda b, i: (0, 0), **const),
                  pl.BlockSpec((CONV_K, CONV_WIDTH), lambda b, i: (0, 0), **const),
                  pl.BlockSpec((CONV_WIDTH, d), lambda b, i: (0, 0), **const),
                  pl.BlockSpec(bias.shape, lambda b, i: (0, 0, 0, 0), **const)],
        out_specs=[row(GROUP_WIDTH), row(d),
                   pl.BlockSpec((None, N_PAIRS, s, PAIR_WIDTH), lambda b, i: (b, 0, 0, 0))],
        out_shape=[jax.ShapeDtypeStruct((bsz, s, GROUP_WIDTH), BF16),
                   jax.ShapeDtypeStruct((bsz, s, d), BF16),
                   jax.ShapeDtypeStruct((bsz, N_PAIRS, s, PAIR_WIDTH), F32)],
        scratch_shapes=[pltpu.VMEM((tm + HALO, CONV_WIDTH), F32),
                        pltpu.VMEM((tm, CONV_WIDTH), BF16),
                        pltpu.VMEM((N_STAGE_SLABS, tm, LANES), F32),
                        pltpu.VMEM((s, QKV_WIDTH), BF16),
                        pltpu.VMEM((3, N_PAIRS, s, PAIR_WIDTH), F32),
                        pltpu.VMEM((N_PAIRS * (tm // SUB_BLOCK), 3, SUB_BLOCK, PAIR_WIDTH), F32),
                        pltpu.VMEM((4, PAIR_WIDTH, 2 * SUB_BLOCK), BF16)],
        compiler_params=pltpu.CompilerParams(
            dimension_semantics=("arbitrary", "arbitrary"), vmem_limit_bytes=VMEM_LIMIT),
        name="proj_attn",
    )(x, ada, w_in, b_in, conv_w, w_pc, bias)


def _out_kernel(x_ref, ada_ref, oa_ref, za_ref, yc_ref, *rest, alpha):
    n_wg = 2 * D_MODEL // OUT_WG_BLOCK
    wg_f32 = rest[:n_wg]
    (b_ref, wpa_f32, wout_f32, bout_ref, lng_ref, lnb_ref,
     out_ref, m_scr, wg_ref, wpa_ref, wout_ref) = rest[n_wg:]

    @pl.when(jnp.logical_and(pl.program_id(0) == 0, pl.program_id(1) == 0))
    def _():
        for k, blk in enumerate(wg_f32):
            wg_ref[:, k * OUT_WG_BLOCK:(k + 1) * OUT_WG_BLOCK] = blk[...].astype(BF16)
        wpa_ref[...] = wpa_f32[...].astype(BF16)
        wout_ref[...] = wout_f32[...].astype(BF16)

    ada = ada_ref[pl.ds(pl.program_id(0), 1), :]
    shift = ada[:, 0:D_MODEL]
    scale = ada[:, D_MODEL:2 * D_MODEL]
    gate = ada[:, 2 * D_MODEL:] * (1.0 / alpha)
    eps = LN_EPS / (alpha * alpha)
    h = (x_ref[...] * (1.0 + scale) + shift).astype(BF16)
    o_attn = jnp.concatenate([oa_ref[pair] for pair in range(N_PAIRS)], axis=1)
    ta = (o_attn * za_ref[...].astype(F32)).astype(BF16)

    for c0 in range(0, D_MODEL, OUT_COL_CHUNK):
        cs = slice(c0, c0 + OUT_COL_CHUNK)
        gs = slice(D_MODEL + c0, D_MODEL + c0 + OUT_COL_CHUNK)
        g_a = _sigmoid(jnp.dot(h, wg_ref[:, cs], preferred_element_type=F32)
                       + b_ref[:, OFF_GA + c0:OFF_GA + c0 + OUT_COL_CHUNK])
        g_b = _sigmoid(jnp.dot(h, wg_ref[:, gs], preferred_element_type=F32)
                       + b_ref[:, OFF_GA + D_MODEL + c0:OFF_GA + D_MODEL + c0 + OUT_COL_CHUNK])
        y_attn = jnp.dot(ta, wpa_ref[:, cs], preferred_element_type=F32)
        m_scr[:, cs] = g_a.astype(BF16) * y_attn.astype(BF16) + g_b.astype(BF16) * yc_ref[:, cs]

    for r0 in range(0, x_ref.shape[0], OUT_CHUNK):
        rows = slice(r0, r0 + OUT_CHUNK)
        sub = gate * (jnp.dot(m_scr[rows, :], wout_ref[...], preferred_element_type=F32) + bout_ref[...])
        r = x_ref[rows, :] + sub
        mu = jnp.mean(r, axis=-1, keepdims=True)
        cen = r - mu
        var = jnp.mean(cen * cen, axis=-1, keepdims=True)
        out_ref[rows, :] = cen * lax.rsqrt(var + eps) * lng_ref[...] + lnb_ref[...]


def _out(x, ada, o_attn, za, yc, w_in, b_in, w_pa, w_out, b_out, ln_g, ln_b, alpha):
    bsz, s, d = x.shape
    tm = OUT_ROW_TILE
    const = dict(pipeline_mode=pl.Buffered(1))
    row = lambda width: pl.BlockSpec((None, tm, width), lambda b, i: (b, i, 0))
    vec = pl.BlockSpec((1, d), lambda b, i: (0, 0), **const)
    n_wg = 2 * d // OUT_WG_BLOCK
    wg_specs = [pl.BlockSpec((d, OUT_WG_BLOCK), lambda b, i, k=k: (0, OFF_GA // OUT_WG_BLOCK + k), **const)
                for k in range(n_wg)]
    return pl.pallas_call(
        functools.partial(_out_kernel, alpha=alpha),
        grid=(bsz, s // tm),
        in_specs=[row(d), pl.BlockSpec(ada.shape, lambda b, i: (0, 0), **const),
                  pl.BlockSpec((None, N_PAIRS, tm, PAIR_WIDTH), lambda b, i: (b, 0, i, 0)),
                  row(GROUP_WIDTH), row(d)]
                 + wg_specs
                 + [pl.BlockSpec((1, D_IN), lambda b, i: (0, 0), **const),
                    pl.BlockSpec((GROUP_WIDTH, d), lambda b, i: (0, 0), **const),
                    pl.BlockSpec((d, d), lambda b, i: (0, 0), **const),
                    vec, vec, vec],
        out_specs=row(d),
        out_shape=jax.ShapeDtypeStruct((bsz, s, d), F32),
        scratch_shapes=[pltpu.VMEM((tm, d), BF16),
                        pltpu.VMEM((d, 2 * d), BF16),
                        pltpu.VMEM((GROUP_WIDTH, d), BF16),
                        pltpu.VMEM((d, d), BF16)],
        compiler_params=pltpu.CompilerParams(
            dimension_semantics=("arbitrary", "arbitrary"), vmem_limit_bytes=VMEM_LIMIT),
        name="out",
    )(x, ada, o_attn, za, yc, *([w_in] * n_wg), b_in, w_pa, w_out, b_out, ln_g, ln_b)


def kernel(x, c, w_ada, b_ada, w_in, b_in, conv_w, w_proj_attn, w_proj_conv, w_out, b_out, ln_g, ln_b):
    bsz, s, d = x.shape
    depth = w_in.shape[0]
    assert d == D_MODEL and s % OUT_ROW_TILE == 0 and w_in.shape[1:] == (D_MODEL, D_IN)
    assert s // DILATED_GROUPS[-1][1] == SUB_BLOCK
    alpha = (2.0 * depth) ** 0.25
    for layer in range(depth):
        ada = _ada(c, w_ada[layer], b_ada[layer])
        b_in2 = b_in[layer].reshape(1, D_IN)
        za, yc, o_attn = _proj_attn(x, ada, w_in[layer, :, :OFF_GA].astype(BF16), b_in2,
                                    conv_w[layer], w_proj_conv[layer].astype(BF16))
        x = _out(x, ada, o_attn, za, yc, w_in[layer], b_in2, w_proj_attn[layer], w_out[layer],
                 b_out[layer].reshape(1, d), ln_g[layer].reshape(1, d), ln_b[layer].reshape(1, d), alpha)
    return x
```

```python
import functools
import itertools

import numpy as np
import jax
import jax.numpy as jnp
from jax import lax
from jax.experimental import pallas as pl
from jax.experimental.pallas import tpu as pltpu

D_MODEL = 1024
HEAD_DIM = 64
HEADS_PER_GROUP = 4
DILATED_GROUPS = ((128, 1), (512, 4), (2048, 16))
N_GROUPS = len(DILATED_GROUPS)
N_ATTN_HEADS = N_GROUPS * HEADS_PER_GROUP
ATTN_WIDTH = N_ATTN_HEADS * HEAD_DIM
GROUP_WIDTH = HEADS_PER_GROUP * HEAD_DIM
QKV_WIDTH = 3 * ATTN_WIDTH
CONV_WIDTH = D_MODEL
CONV_K = 3
SUB_BLOCK = 128
ALIBI_MAX_EXP = 8.0
LN_EPS = 1e-5

OFF_Z_ATTN = QKV_WIDTH
OFF_UX = OFF_Z_ATTN + GROUP_WIDTH
OFF_GB = OFF_UX + CONV_WIDTH
OFF_GC = OFF_GB + CONV_WIDTH
OFF_ZC = OFF_GC + CONV_WIDTH
OFF_GA = OFF_ZC + CONV_WIDTH
D_IN = OFF_GA + 2 * D_MODEL

ROW_TILE = 512
OUT_ROW_TILE = 1024
OUT_CHUNK = 256
OUT_COL_CHUNK = 256
OUT_WG_BLOCK = 512
W_CHUNK = 512
CONV_CHUNK = 256
LANES = 128
PAIR_WIDTH = 2 * HEAD_DIM
N_PAIRS = HEADS_PER_GROUP // 2
REGROUP_STRIDE = 4
RES_STRIDE = REGROUP_STRIDE
N_STAGE_SLABS = 6
HALO = 8
VMEM_LIMIT = 56 * 1024 * 1024
PROJ_VMEM_LIMIT = 60 * 1024 * 1024

F32 = jnp.float32
BF16 = jnp.bfloat16

assert PAIR_WIDTH == LANES and ROW_TILE // DILATED_GROUPS[1][1] == SUB_BLOCK
assert DILATED_GROUPS[1][1] == RES_STRIDE and DILATED_GROUPS[2][1] == RES_STRIDE * RES_STRIDE
assert 2 * (CONV_WIDTH // CONV_CHUNK) == 2 * (ROW_TILE // SUB_BLOCK)
assert OFF_GA % OUT_WG_BLOCK == 0 and OFF_GA % W_CHUNK == 0


def _sigmoid(v):
    return 0.5 * jnp.tanh(0.5 * v) + 0.5


def _silu(v):
    return v * _sigmoid(v)


def _ada_kernel(c_ref, w_ref, b_ref, o_ref):
    o_ref[...] = jnp.dot(_silu(c_ref[...]).astype(BF16), w_ref[...].astype(BF16),
                         preferred_element_type=F32) + b_ref[...]


def _ada(c, w_ada, b_ada):
    bsz, d = c.shape
    n_out = w_ada.shape[1]
    return pl.pallas_call(
        _ada_kernel,
        grid=(n_out // d,),
        in_specs=[pl.BlockSpec((bsz, d), lambda j: (0, 0)),
                  pl.BlockSpec((d, d), lambda j: (0, j)),
                  pl.BlockSpec((1, d), lambda j: (0, j))],
        out_specs=pl.BlockSpec((bsz, d), lambda j: (0, j)),
        out_shape=jax.ShapeDtypeStruct((bsz, n_out), F32),
        name="ada",
    )(c, w_ada, b_ada.reshape(1, n_out))


def _attn_bias_table():
    L = SUB_BLOCK
    tabs = []
    for group, (window, dilation) in enumerate(DILATED_GROUPS):
        span = window // dilation
        heads = np.arange(group * HEADS_PER_GROUP, (group + 1) * HEADS_PER_GROUP, dtype=np.float64)
        slopes = 2.0 ** (-ALIBI_MAX_EXP * (heads + 1.0) / N_ATTN_HEADS)
        delta = (np.arange(L)[:, None] + L - np.arange(2 * L)[None, :]).astype(np.float64)
        valid = (delta >= 0) & (delta <= span)
        bias = np.where(valid[None], -slopes[:, None, None] * (delta * dilation)[None], -np.inf)
        tabs.append(bias.reshape(HEADS_PER_GROUP // 2, 2 * L, 2 * L))
    return np.stack(tabs).astype(np.float32)


def _proj_attn_kernel(x_ref, ada_ref, w_hbm, b_ref, cw_ref, wpc_ref, bias_ref,
                      za_ref, yc_ref, o_ref, u_scr, t_scr, stage_scr, qkv_scr, res, tmp_scr, kt_scr,
                      w_ref, w_stage, w_sem):
    tm = x_ref.shape[0]
    S = qkv_scr.shape[0]
    L = SUB_BLOCK
    tile = pl.program_id(1)
    first_tile = tile == 0
    base = pl.multiple_of(tile * tm, tm)
    prev_base = pl.multiple_of(jnp.maximum(tile - 1, 0) * tm, tm)

    def weight_chunk(k):
        return pltpu.make_async_copy(w_hbm.at[:, pl.ds(k * W_CHUNK, W_CHUNK)],
                                     w_stage.at[k % 2], w_sem.at[k % 2])

    @pl.when(jnp.logical_and(pl.program_id(0) == 0, first_tile))
    def _():
        n_chunks = OFF_GA // W_CHUNK
        weight_chunk(0).start()
        for k in range(n_chunks):
            if k + 1 < n_chunks:
                weight_chunk(k + 1).start()
            weight_chunk(k).wait()
            w_ref[:, k * W_CHUNK:(k + 1) * W_CHUNK] = w_stage[k % 2].astype(BF16)

    @pl.when(first_tile)
    def _():
        u_scr[0:HALO, :] = jnp.zeros((HALO, CONV_WIDTH), F32)

    @pl.when(jnp.logical_not(first_tile))
    def _():
        u_scr[0:HALO, :] = u_scr[tm:tm + HALO, :]

    ada = ada_ref[pl.ds(pl.program_id(0), 1), :]
    shift = ada[:, 0:D_MODEL]
    scale = ada[:, D_MODEL:2 * D_MODEL]
    h = (x_ref[...] * (1.0 + scale) + shift).astype(BF16)

    def proj(lo, width):
        return (jnp.dot(h, w_ref[:, lo:lo + width], preferred_element_type=F32)
                + b_ref[:, lo:lo + width])

    slabs = itertools.cycle(range(N_STAGE_SLABS))

    def project_qkv(g):
        d = DILATED_GROUPS[g][1]
        for which in range(3):
            c0 = which * ATTN_WIDTH + g * GROUP_WIDTH
            blk = proj(c0, GROUP_WIDTH)
            if d == 1:
                qkv_scr[pl.ds(base, tm), c0:c0 + GROUP_WIDTH] = blk.astype(BF16)
                continue
            n_t = tm // d
            for lo in range(0, GROUP_WIDTH, LANES):
                cols = slice(c0 + lo, c0 + lo + LANES)
                slab = next(slabs)
                stage_scr[slab] = blk[:, lo:lo + LANES]
                if d == REGROUP_STRIDE:
                    for r in range(d):
                        qkv_scr[pl.ds(base + r * n_t, n_t), cols] = (
                            stage_scr[slab, pl.ds(r, n_t, stride=d), :].astype(BF16))
                else:
                    assert d == REGROUP_STRIDE * REGROUP_STRIDE
                    n_1 = tm // REGROUP_STRIDE
                    slab2 = next(slabs)
                    for r1 in range(REGROUP_STRIDE):
                        stage_scr[slab2, r1 * n_1:(r1 + 1) * n_1, :] = (
                            stage_scr[slab, pl.ds(r1, n_1, stride=REGROUP_STRIDE), :])
                    for r1 in range(REGROUP_STRIDE):
                        for r2 in range(REGROUP_STRIDE):
                            r = r2 * REGROUP_STRIDE + r1
                            qkv_scr[pl.ds(base + r * n_t, n_t), cols] = stage_scr[
                                slab2, pl.ds(r1 * n_1 + r2, n_t, stride=REGROUP_STRIDE), :].astype(BF16)

    project_qkv(1)
    project_qkv(0)

    low_half = lax.broadcasted_iota(jnp.int32, (L, PAIR_WIDTH), 1) < HEAD_DIM
    no_prev = jnp.logical_and(lax.broadcasted_iota(jnp.int32, (2 * L, 2 * L), 1) < L, first_tile)

    def col(g, which, pair):
        return which * ATTN_WIDTH + g * GROUP_WIDTH + pair * PAIR_WIDTH

    kt_slots = itertools.cycle(range(kt_scr.shape[0]))

    def probabilities(g, pair, q, k, maybe_no_prev):
        q = q * 0.125
        zero = jnp.zeros_like(q)
        q2 = jnp.concatenate([jnp.where(low_half, q, zero), jnp.where(low_half, zero, q)], axis=0)
        slot = next(kt_slots)
        n_keys = k.shape[0]
        kt_scr[slot, :, :n_keys] = k.T
        sc = jnp.dot(q2, kt_scr[slot, :, :n_keys], preferred_element_type=F32)
        sc = sc + bias_ref[g, pair, :, 2 * L - k.shape[0]:]
        if maybe_no_prev:
            sc = jnp.where(no_prev, -jnp.inf, sc)
        m = jnp.max(sc, axis=-1, keepdims=True)
        return jnp.exp(sc - m).astype(BF16), m

    def weighted_values(p, m, v):
        v1 = jnp.concatenate([v, jnp.ones(v.shape, BF16)], axis=1)
        pv = jnp.dot(p, v1, preferred_element_type=F32)
        mb = jnp.broadcast_to(m, (2 * L, PAIR_WIDTH))
        return (jnp.where(low_half, pv[:L, :PAIR_WIDTH], pv[L:, :PAIR_WIDTH]),
                jnp.where(low_half, pv[:L, PAIR_WIDTH:], pv[L:, PAIR_WIDTH:]),
                jnp.where(low_half, mb[:L], mb[L:]))

    def merged(pair, rows, acc, den, m):
        m_old = res[2, pair, rows, :]
        m_new = jnp.maximum(m_old, m)
        a = jnp.exp(m_old - m_new)
        b = jnp.exp(m - m_new)
        return a * res[0, pair, rows, :] + b * acc, a * res[1, pair, rows, :] + b * den, m_new

    g1_d = DILATED_GROUPS[1][1]

    def g1_scores(r):
        out = []
        for pair in range(N_PAIRS):
            cur = pl.ds(base + r * L, L)
            prev = pl.ds(prev_base + r * L, L)
            k = jnp.concatenate([qkv_scr[prev, pl.ds(col(1, 1, pair), PAIR_WIDTH)],
                                 qkv_scr[cur, pl.ds(col(1, 1, pair), PAIR_WIDTH)]], axis=0)
            out.append(probabilities(1, pair, qkv_scr[cur, pl.ds(col(1, 0, pair), PAIR_WIDTH)], k, True))
        return out

    def g1_finish(r, parts):
        for pair, (p, m) in enumerate(parts):
            cur = pl.ds(base + r * L, L)
            prev = pl.ds(prev_base + r * L, L)
            v = jnp.concatenate([qkv_scr[prev, pl.ds(col(1, 2, pair), PAIR_WIDTH)],
                                 qkv_scr[cur, pl.ds(col(1, 2, pair), PAIR_WIDTH)]], axis=0)
            acc, den, mm = weighted_values(p, m, v)
            rows = pl.ds(base + r * L, L)
            res[0, pair, rows, :] = acc
            res[1, pair, rows, :] = den
            res[2, pair, rows, :] = mm

    def g0_rows(c):
        q0 = base + c * L
        k0 = jnp.maximum(q0 - L, 0) if c == 0 else q0 - L
        return pl.ds(pl.multiple_of(q0, L), L), pl.ds(pl.multiple_of(k0, L), L)

    def g0_scores(c):
        cur, prev = g0_rows(c)
        out = []
        for pair in range(N_PAIRS):
            k = jnp.concatenate([qkv_scr[prev, pl.ds(col(0, 1, pair), PAIR_WIDTH)],
                                 qkv_scr[cur, pl.ds(col(0, 1, pair), PAIR_WIDTH)]], axis=0)
            out.append(probabilities(0, pair, qkv_scr[cur, pl.ds(col(0, 0, pair), PAIR_WIDTH)], k, c == 0))
        return out

    def g0_finish(c, parts):
        cur, prev = g0_rows(c)
        for pair, (p, m) in enumerate(parts):
            v = jnp.concatenate([qkv_scr[prev, pl.ds(col(0, 2, pair), PAIR_WIDTH)],
                                 qkv_scr[cur, pl.ds(col(0, 2, pair), PAIR_WIDTH)]], axis=0)
            slab = c * N_PAIRS + pair
            for k, val in enumerate(weighted_values(p, m, v)):
                tmp_scr[slab, k] = val
            n_r = L // RES_STRIDE
            for r in range(RES_STRIDE):
                rows = pl.ds(pl.multiple_of(base + r * (tm // RES_STRIDE) + c * n_r, n_r), n_r)
                acc, den, mm = merged(pair, rows, *[tmp_scr[slab, k, pl.ds(r, n_r, stride=RES_STRIDE), :]
                                                    for k in range(3)])
                res[0, pair, rows, :] = acc
                res[1, pair, rows, :] = den
                res[2, pair, rows, :] = mm

    items = ([(g1_scores, g1_finish, (j,)) for j in range(tm // L)]
             + [(g0_scores, g0_finish, (j,)) for j in range(tm // L)])

    def start(item):
        scores_fn, finish_fn, args = item
        return finish_fn, [(a, scores_fn(a)) for a in args]

    def finish(pending):
        finish_fn, parts = pending
        for a, part in parts:
            finish_fn(a, part)

    in_flight = []
    pending_items = list(items)

    def rotate():
        in_flight.append(start(pending_items.pop(0)))
        if len(in_flight) > 2:
            finish(in_flight.pop(0))

    for c0 in range(0, CONV_WIDTH, CONV_CHUNK):
        cs = slice(c0, c0 + CONV_CHUNK)
        u = proj(OFF_GC + c0, CONV_CHUNK) * proj(OFF_UX + c0, CONV_CHUNK)
        u_scr[HALO:HALO + tm, cs] = u
        conv = cw_ref[0:1, cs] * u_scr[HALO - 2:HALO - 2 + tm, cs]
        conv = conv + cw_ref[1:2, cs] * u_scr[HALO - 1:HALO - 1 + tm, cs]
        conv = conv + cw_ref[2:3, cs] * u
        rotate()
        t = proj(OFF_GB + c0, CONV_CHUNK) * conv * _silu(proj(OFF_ZC + c0, CONV_CHUNK))
        t_scr[:, cs] = t.astype(BF16)
        rotate()
    finish(in_flight.pop(0))
    za_ref[...] = _silu(proj(OFF_Z_ATTN, GROUP_WIDTH)).astype(BF16)
    yc_ref[...] = jnp.dot(t_scr[...], wpc_ref[...], preferred_element_type=F32).astype(BF16)
    finish(in_flight.pop(0))
    project_qkv(2)

    g2_d = DILATED_GROUPS[2][1]
    n_t2 = tm // g2_d

    def g2_piece(which, r, pair):
        pieces = [qkv_scr[pl.ds(pl.multiple_of(t * tm + r * n_t2, n_t2), n_t2),
                          pl.ds(col(2, which, pair), PAIR_WIDTH)] for t in range(S // tm)]
        return jnp.concatenate(pieces, axis=0)

    @pl.when(tile == S // tm - 1)
    def _():
        def body(step, carry):
            work = [(u, pair) for u in range(RES_STRIDE) for pair in range(N_PAIRS)]
            parts = [probabilities(2, pair, g2_piece(0, step * RES_STRIDE + u, pair),
                                   g2_piece(1, step * RES_STRIDE + u, pair), False) for u, pair in work]
            for (u, pair), (p, m) in zip(work, parts):
                r = step * RES_STRIDE + u
                new = weighted_values(p, m, g2_piece(2, r, pair))
                outs = []
                for t in range(S // tm):
                    start_row = t * tm + u * (tm // RES_STRIDE) + step
                    rows = pl.ds(start_row, n_t2, stride=RES_STRIDE)
                    acc, den, _ = merged(pair, rows, *[x[t * n_t2:(t + 1) * n_t2] for x in new])
                    outs.append(acc / den)
                o_ref[pair, pl.ds(r, L, stride=g2_d), :] = jnp.concatenate(outs, axis=0)
            return carry
        lax.fori_loop(0, g2_d // RES_STRIDE, body, 0)


def _proj_attn(x, ada, w_in, b_in, conv_w, w_pc):
    bsz, s, d = x.shape
    tm = ROW_TILE
    bias = jnp.asarray(_attn_bias_table())
    const = dict(pipeline_mode=pl.Buffered(1))
    row = lambda width: pl.BlockSpec((None, tm, width), lambda b, i: (b, i, 0))
    return pl.pallas_call(
        _proj_attn_kernel,
        grid=(bsz, s // tm),
        in_specs=[row(d),
                  pl.BlockSpec(ada.shape, lambda b, i: (0, 0), **const),
                  pl.BlockSpec(memory_space=pl.ANY),
                  pl.BlockSpec((1, D_IN), lambda b, i: (0, 0), **const),
                  pl.BlockSpec((CONV_K, CONV_WIDTH), lambda b, i: (0, 0), **const),
                  pl.BlockSpec((CONV_WIDTH, d), lambda b, i: (0, 0), **const),
                  pl.BlockSpec(bias.shape, lambda b, i: (0, 0, 0, 0), **const)],
        out_specs=[row(GROUP_WIDTH), row(d),
                   pl.BlockSpec((None, N_PAIRS, s, PAIR_WIDTH), lambda b, i: (b, 0, 0, 0))],
        out_shape=[jax.ShapeDtypeStruct((bsz, s, GROUP_WIDTH), BF16),
                   jax.ShapeDtypeStruct((bsz, s, d), BF16),
                   jax.ShapeDtypeStruct((bsz, N_PAIRS, s, PAIR_WIDTH), F32)],
        scratch_shapes=[pltpu.VMEM((tm + HALO, CONV_WIDTH), F32),
                        pltpu.VMEM((tm, CONV_WIDTH), BF16),
                        pltpu.VMEM((N_STAGE_SLABS, tm, LANES), F32),
                        pltpu.VMEM((s, QKV_WIDTH), BF16),
                        pltpu.VMEM((3, N_PAIRS, s, PAIR_WIDTH), F32),
                        pltpu.VMEM((N_PAIRS * (tm // SUB_BLOCK), 3, SUB_BLOCK, PAIR_WIDTH), F32),
                        pltpu.VMEM((4, PAIR_WIDTH, 2 * SUB_BLOCK), BF16),
                        pltpu.VMEM((d, OFF_GA), BF16),
                        pltpu.VMEM((2, d, W_CHUNK), F32),
                        pltpu.SemaphoreType.DMA((2,))],
        compiler_params=pltpu.CompilerParams(
            dimension_semantics=("arbitrary", "arbitrary"), vmem_limit_bytes=PROJ_VMEM_LIMIT),
        name="proj_attn",
    )(x, ada, w_in, b_in, conv_w, w_pc, bias)


def _out_kernel(x_ref, ada_ref, oa_ref, za_ref, yc_ref, *rest, alpha):
    n_wg = 2 * D_MODEL // OUT_WG_BLOCK
    wg_f32 = rest[:n_wg]
    (b_ref, wpa_f32, wout_f32, bout_ref, lng_ref, lnb_ref,
     out_ref, m_scr, wg_ref, wpa_ref, wout_ref) = rest[n_wg:]

    @pl.when(jnp.logical_and(pl.program_id(0) == 0, pl.program_id(1) == 0))
    def _():
        for k, blk in enumerate(wg_f32):
            wg_ref[:, k * OUT_WG_BLOCK:(k + 1) * OUT_WG_BLOCK] = blk[...].astype(BF16)
        wpa_ref[...] = wpa_f32[...].astype(BF16)
        wout_ref[...] = wout_f32[...].astype(BF16)

    ada = ada_ref[pl.ds(pl.program_id(0), 1), :]
    shift = ada[:, 0:D_MODEL]
    scale = ada[:, D_MODEL:2 * D_MODEL]
    gate = ada[:, 2 * D_MODEL:] * (1.0 / alpha)
    eps = LN_EPS / (alpha * alpha)
    h = (x_ref[...] * (1.0 + scale) + shift).astype(BF16)
    o_attn = jnp.concatenate([oa_ref[pair] for pair in range(N_PAIRS)], axis=1)
    ta = (o_attn * za_ref[...].astype(F32)).astype(BF16)

    for c0 in range(0, D_MODEL, OUT_COL_CHUNK):
        cs = slice(c0, c0 + OUT_COL_CHUNK)
        gs = slice(D_MODEL + c0, D_MODEL + c0 + OUT_COL_CHUNK)
        g_a = _sigmoid(jnp.dot(h, wg_ref[:, cs], preferred_element_type=F32)
                       + b_ref[:, OFF_GA + c0:OFF_GA + c0 + OUT_COL_CHUNK])
        g_b = _sigmoid(jnp.dot(h, wg_ref[:, gs], preferred_element_type=F32)
                       + b_ref[:, OFF_GA + D_MODEL + c0:OFF_GA + D_MODEL + c0 + OUT_COL_CHUNK])
        y_attn = jnp.dot(ta, wpa_ref[:, cs], preferred_element_type=F32)
        m_scr[:, cs] = g_a.astype(BF16) * y_attn.astype(BF16) + g_b.astype(BF16) * yc_ref[:, cs]

    for r0 in range(0, x_ref.shape[0], OUT_CHUNK):
        rows = slice(r0, r0 + OUT_CHUNK)
        sub = gate * (jnp.dot(m_scr[rows, :], wout_ref[...], preferred_element_type=F32) + bout_ref[...])
        r = x_ref[rows, :] + sub
        mu = jnp.mean(r, axis=-1, keepdims=True)
        cen = r - mu
        var = jnp.mean(cen * cen, axis=-1, keepdims=True)
        out_ref[rows, :] = cen * lax.rsqrt(var + eps) * lng_ref[...] + lnb_ref[...]


def _out(x, ada, o_attn, za, yc, w_in, b_in, w_pa, w_out, b_out, ln_g, ln_b, alpha):
    bsz, s, d = x.shape
    tm = OUT_ROW_TILE
    const = dict(pipeline_mode=pl.Buffered(1))
    row = lambda width: pl.BlockSpec((None, tm, width), lambda b, i: (b, i, 0))
    vec = pl.BlockSpec((1, d), lambda b, i: (0, 0), **const)
    n_wg = 2 * d // OUT_WG_BLOCK
    wg_specs = [pl.BlockSpec((d, OUT_WG_BLOCK), lambda b, i, k=k: (0, OFF_GA // OUT_WG_BLOCK + k), **const)
                for k in range(n_wg)]
    return pl.pallas_call(
        functools.partial(_out_kernel, alpha=alpha),
        grid=(bsz, s // tm),
        in_specs=[row(d), pl.BlockSpec(ada.shape, lambda b, i: (0, 0), **const),
                  pl.BlockSpec((None, N_PAIRS, tm, PAIR_WIDTH), lambda b, i: (b, 0, i, 0)),
                  row(GROUP_WIDTH), row(d)]
                 + wg_specs
                 + [pl.BlockSpec((1, D_IN), lambda b, i: (0, 0), **const),
                    pl.BlockSpec((GROUP_WIDTH, d), lambda b, i: (0, 0), **const),
                    pl.BlockSpec((d, d), lambda b, i: (0, 0), **const),
                    vec, vec, vec],
        out_specs=row(d),
        out_shape=jax.ShapeDtypeStruct((bsz, s, d), F32),
        scratch_shapes=[pltpu.VMEM((tm, d), BF16),
                        pltpu.VMEM((d, 2 * d), BF16),
                        pltpu.VMEM((GROUP_WIDTH, d), BF16),
                        pltpu.VMEM((d, d), BF16)],
        compiler_params=pltpu.CompilerParams(
            dimension_semantics=("arbitrary", "arbitrary"), vmem_limit_bytes=VMEM_LIMIT),
        name="out",
    )(x, ada, o_attn, za, yc, *([w_in] * n_wg), b_in, w_pa, w_out, b_out, ln_g, ln_b)


def kernel(x, c, w_ada, b_ada, w_in, b_in, conv_w, w_proj_attn, w_proj_conv, w_out, b_out, ln_g, ln_b):
    bsz, s, d = x.shape
    depth = w_in.shape[0]
    assert d == D_MODEL and s % OUT_ROW_TILE == 0 and w_in.shape[1:] == (D_MODEL, D_IN)
    assert s // DILATED_GROUPS[-1][1] == SUB_BLOCK
    alpha = (2.0 * depth) ** 0.25
    for layer in range(depth):
        ada = _ada(c, w_ada[layer], b_ada[layer])
        b_in2 = b_in[layer].reshape(1, D_IN)
        za, yc, o_attn = _proj_attn(x, ada, w_in[layer], b_in2,
                                    conv_w[layer], w_proj_conv[layer].astype(BF16))
        x = _out(x, ada, o_attn, za, yc, w_in[layer], b_in2, w_proj_attn[layer], w_out[layer],
                 b_out[layer].reshape(1, d), ln_g[layer].reshape(1, d), ln_b[layer].reshape(1, d), alpha)
    return x
```

```python
import functools
import itertools

import numpy as np
import jax
import jax.numpy as jnp
from jax import lax
from jax.experimental import pallas as pl
from jax.experimental.pallas import tpu as pltpu

D_MODEL = 1024
HEAD_DIM = 64
HEADS_PER_GROUP = 4
DILATED_GROUPS = ((128, 1), (512, 4), (2048, 16))
N_GROUPS = len(DILATED_GROUPS)
N_ATTN_HEADS = N_GROUPS * HEADS_PER_GROUP
ATTN_WIDTH = N_ATTN_HEADS * HEAD_DIM
GROUP_WIDTH = HEADS_PER_GROUP * HEAD_DIM
QKV_WIDTH = 3 * ATTN_WIDTH
CONV_WIDTH = D_MODEL
CONV_K = 3
SUB_BLOCK = 128
ALIBI_MAX_EXP = 8.0
LN_EPS = 1e-5

OFF_Z_ATTN = QKV_WIDTH
OFF_UX = OFF_Z_ATTN + GROUP_WIDTH
OFF_GB = OFF_UX + CONV_WIDTH
OFF_GC = OFF_GB + CONV_WIDTH
OFF_ZC = OFF_GC + CONV_WIDTH
OFF_GA = OFF_ZC + CONV_WIDTH
D_IN = OFF_GA + 2 * D_MODEL

ROW_TILE = 512
OUT_ROW_TILE = 1024
OUT_CHUNK = 256
OUT_COL_CHUNK = 256
OUT_WG_BLOCK = 512
W_CHUNK = 256
W_SLOTS = 4
CONV_CHUNK = 256
LANES = 128
PAIR_WIDTH = 2 * HEAD_DIM
N_PAIRS = HEADS_PER_GROUP // 2
REGROUP_STRIDE = 4
RES_STRIDE = REGROUP_STRIDE
N_STAGE_SLABS = 6
HALO = 8
VMEM_LIMIT = 56 * 1024 * 1024
PROJ_VMEM_LIMIT = 60 * 1024 * 1024

F32 = jnp.float32
BF16 = jnp.bfloat16

assert PAIR_WIDTH == LANES and ROW_TILE // DILATED_GROUPS[1][1] == SUB_BLOCK
assert DILATED_GROUPS[1][1] == RES_STRIDE and DILATED_GROUPS[2][1] == RES_STRIDE * RES_STRIDE
assert 2 * (CONV_WIDTH // CONV_CHUNK) == 2 * (ROW_TILE // SUB_BLOCK)
assert OFF_GA % OUT_WG_BLOCK == 0 and OFF_GA % W_CHUNK == 0 and OFF_GA // W_CHUNK >= W_SLOTS


def _sigmoid(v):
    return 0.5 * jnp.tanh(0.5 * v) + 0.5


def _silu(v):
    return v * _sigmoid(v)


def _ada_kernel(c_ref, w_ref, b_ref, o_ref):
    o_ref[...] = jnp.dot(_silu(c_ref[...]).astype(BF16), w_ref[...].astype(BF16),
                         preferred_element_type=F32) + b_ref[...]


def _ada(c, w_ada, b_ada):
    bsz, d = c.shape
    n_out = w_ada.shape[1]
    return pl.pallas_call(
        _ada_kernel,
        grid=(n_out // d,),
        in_specs=[pl.BlockSpec((bsz, d), lambda j: (0, 0)),
                  pl.BlockSpec((d, d), lambda j: (0, j)),
                  pl.BlockSpec((1, d), lambda j: (0, j))],
        out_specs=pl.BlockSpec((bsz, d), lambda j: (0, j)),
        out_shape=jax.ShapeDtypeStruct((bsz, n_out), F32),
        name="ada",
    )(c, w_ada, b_ada.reshape(1, n_out))


def _attn_bias_table():
    L = SUB_BLOCK
    tabs = []
    for group, (window, dilation) in enumerate(DILATED_GROUPS):
        span = window // dilation
        heads = np.arange(group * HEADS_PER_GROUP, (group + 1) * HEADS_PER_GROUP, dtype=np.float64)
        slopes = 2.0 ** (-ALIBI_MAX_EXP * (heads + 1.0) / N_ATTN_HEADS)
        delta = (np.arange(L)[:, None] + L - np.arange(2 * L)[None, :]).astype(np.float64)
        valid = (delta >= 0) & (delta <= span)
        bias = np.where(valid[None], -slopes[:, None, None] * (delta * dilation)[None], -np.inf)
        tabs.append(bias.reshape(HEADS_PER_GROUP // 2, 2 * L, 2 * L))
    return np.stack(tabs).astype(np.float32)


def _proj_attn_kernel(x_ref, ada_ref, w_hbm, b_ref, cw_ref, wpc_ref, bias_ref,
                      za_ref, yc_ref, o_ref, u_scr, t_scr, stage_scr, qkv_scr, res, tmp_scr, kt_scr,
                      w_ref, w_stage, w_sem):
    tm = x_ref.shape[0]
    S = qkv_scr.shape[0]
    L = SUB_BLOCK
    tile = pl.program_id(1)
    first_tile = tile == 0
    base = pl.multiple_of(tile * tm, tm)
    prev_base = pl.multiple_of(jnp.maximum(tile - 1, 0) * tm, tm)

    def weight_chunk(k):
        return pltpu.make_async_copy(w_hbm.at[:, pl.ds(k * W_CHUNK, W_CHUNK)],
                                     w_stage.at[k % W_SLOTS], w_sem.at[k % W_SLOTS])

    @pl.when(jnp.logical_and(pl.program_id(0) == 0, first_tile))
    def _():
        n_chunks = OFF_GA // W_CHUNK
        ahead = W_SLOTS - 1
        for k in range(ahead):
            weight_chunk(k).start()
        for k in range(n_chunks):
            if k + ahead < n_chunks:
                weight_chunk(k + ahead).start()
            weight_chunk(k).wait()
            w_ref[:, k * W_CHUNK:(k + 1) * W_CHUNK] = w_stage[k % W_SLOTS].astype(BF16)

    @pl.when(first_tile)
    def _():
        u_scr[0:HALO, :] = jnp.zeros((HALO, CONV_WIDTH), F32)

    @pl.when(jnp.logical_not(first_tile))
    def _():
        u_scr[0:HALO, :] = u_scr[tm:tm + HALO, :]

    ada = ada_ref[pl.ds(pl.program_id(0), 1), :]
    shift = ada[:, 0:D_MODEL]
    scale = ada[:, D_MODEL:2 * D_MODEL]
    h = (x_ref[...] * (1.0 + scale) + shift).astype(BF16)

    def proj(lo, width):
        return (jnp.dot(h, w_ref[:, lo:lo + width], preferred_element_type=F32)
                + b_ref[:, lo:lo + width])

    slabs = itertools.cycle(range(N_STAGE_SLABS))

    def project_qkv(g):
        d = DILATED_GROUPS[g][1]
        for which in range(3):
            c0 = which * ATTN_WIDTH + g * GROUP_WIDTH
            blk = proj(c0, GROUP_WIDTH)
            if d == 1:
                qkv_scr[pl.ds(base, tm), c0:c0 + GROUP_WIDTH] = blk.astype(BF16)
                continue
            n_t = tm // d
            for lo in range(0, GROUP_WIDTH, LANES):
                cols = slice(c0 + lo, c0 + lo + LANES)
                slab = next(slabs)
                stage_scr[slab] = blk[:, lo:lo + LANES]
                if d == REGROUP_STRIDE:
                    for r in range(d):
                        qkv_scr[pl.ds(base + r * n_t, n_t), cols] = (
                            stage_scr[slab, pl.ds(r, n_t, stride=d), :].astype(BF16))
                else:
                    assert d == REGROUP_STRIDE * REGROUP_STRIDE
                    n_1 = tm // REGROUP_STRIDE
                    slab2 = next(slabs)
                    for r1 in range(REGROUP_STRIDE):
                        stage_scr[slab2, r1 * n_1:(r1 + 1) * n_1, :] = (
                            stage_scr[slab, pl.ds(r1, n_1, stride=REGROUP_STRIDE), :])
                    for r1 in range(REGROUP_STRIDE):
                        for r2 in range(REGROUP_STRIDE):
                            r = r2 * REGROUP_STRIDE + r1
                            qkv_scr[pl.ds(base + r * n_t, n_t), cols] = stage_scr[
                                slab2, pl.ds(r1 * n_1 + r2, n_t, stride=REGROUP_STRIDE), :].astype(BF16)

    project_qkv(1)
    project_qkv(0)

    low_half = lax.broadcasted_iota(jnp.int32, (L, PAIR_WIDTH), 1) < HEAD_DIM
    no_prev = jnp.logical_and(lax.broadcasted_iota(jnp.int32, (2 * L, 2 * L), 1) < L, first_tile)

    def col(g, which, pair):
        return which * ATTN_WIDTH + g * GROUP_WIDTH + pair * PAIR_WIDTH

    kt_slots = itertools.cycle(range(kt_scr.shape[0]))

    def probabilities(g, pair, q, k, maybe_no_prev):
        q = q * 0.125
        zero = jnp.zeros_like(q)
        q2 = jnp.concatenate([jnp.where(low_half, q, zero), jnp.where(low_half, zero, q)], axis=0)
        slot = next(kt_slots)
        n_keys = k.shape[0]
        kt_scr[slot, :, :n_keys] = k.T
        sc = jnp.dot(q2, kt_scr[slot, :, :n_keys], preferred_element_type=F32)
        sc = sc + bias_ref[g, pair, :, 2 * L - k.shape[0]:]
        if maybe_no_prev:
            sc = jnp.where(no_prev, -jnp.inf, sc)
        m = jnp.max(sc, axis=-1, keepdims=True)
        return jnp.exp(sc - m).astype(BF16), m

    def weighted_values(p, m, v):
        v1 = jnp.concatenate([v, jnp.ones(v.shape, BF16)], axis=1)
        pv = jnp.dot(p, v1, preferred_element_type=F32)
        mb = jnp.broadcast_to(m, (2 * L, PAIR_WIDTH))
        return (jnp.where(low_half, pv[:L, :PAIR_WIDTH], pv[L:, :PAIR_WIDTH]),
                jnp.where(low_half, pv[:L, PAIR_WIDTH:], pv[L:, PAIR_WIDTH:]),
                jnp.where(low_half, mb[:L], mb[L:]))

    def merged(pair, rows, acc, den, m):
        m_old = res[2, pair, rows, :]
        m_new = jnp.maximum(m_old, m)
        a = jnp.exp(m_old - m_new)
        b = jnp.exp(m - m_new)
        return a * res[0, pair, rows, :] + b * acc, a * res[1, pair, rows, :] + b * den, m_new

    g1_d = DILATED_GROUPS[1][1]

    def g1_scores(r):
        out = []
        for pair in range(N_PAIRS):
            cur = pl.ds(base + r * L, L)
            prev = pl.ds(prev_base + r * L, L)
            k = jnp.concatenate([qkv_scr[prev, pl.ds(col(1, 1, pair), PAIR_WIDTH)],
                                 qkv_scr[cur, pl.ds(col(1, 1, pair), PAIR_WIDTH)]], axis=0)
            out.append(probabilities(1, pair, qkv_scr[cur, pl.ds(col(1, 0, pair), PAIR_WIDTH)], k, True))
        return out

    def g1_finish(r, parts):
        for pair, (p, m) in enumerate(parts):
            cur = pl.ds(base + r * L, L)
            prev = pl.ds(prev_base + r * L, L)
            v = jnp.concatenate([qkv_scr[prev, pl.ds(col(1, 2, pair), PAIR_WIDTH)],
                                 qkv_scr[cur, pl.ds(col(1, 2, pair), PAIR_WIDTH)]], axis=0)
            acc, den, mm = weighted_values(p, m, v)
            rows = pl.ds(base + r * L, L)
            res[0, pair, rows, :] = acc
            res[1, pair, rows, :] = den
            res[2, pair, rows, :] = mm

    def g0_rows(c):
        q0 = base + c * L
        k0 = jnp.maximum(q0 - L, 0) if c == 0 else q0 - L
        return pl.ds(pl.multiple_of(q0, L), L), pl.ds(pl.multiple_of(k0, L), L)

    def g0_scores(c):
        cur, prev = g0_rows(c)
        out = []
        for pair in range(N_PAIRS):
            k = jnp.concatenate([qkv_scr[prev, pl.ds(col(0, 1, pair), PAIR_WIDTH)],
                                 qkv_scr[cur, pl.ds(col(0, 1, pair), PAIR_WIDTH)]], axis=0)
            out.append(probabilities(0, pair, qkv_scr[cur, pl.ds(col(0, 0, pair), PAIR_WIDTH)], k, c == 0))
        return out

    def g0_finish(c, parts):
        cur, prev = g0_rows(c)
        for pair, (p, m) in enumerate(parts):
            v = jnp.concatenate([qkv_scr[prev, pl.ds(col(0, 2, pair), PAIR_WIDTH)],
                                 qkv_scr[cur, pl.ds(col(0, 2, pair), PAIR_WIDTH)]], axis=0)
            slab = c * N_PAIRS + pair
            for k, val in enumerate(weighted_values(p, m, v)):
                tmp_scr[slab, k] = val
            n_r = L // RES_STRIDE
            for r in range(RES_STRIDE):
                rows = pl.ds(pl.multiple_of(base + r * (tm // RES_STRIDE) + c * n_r, n_r), n_r)
                acc, den, mm = merged(pair, rows, *[tmp_scr[slab, k, pl.ds(r, n_r, stride=RES_STRIDE), :]
                                                    for k in range(3)])
                res[0, pair, rows, :] = acc
                res[1, pair, rows, :] = den
                res[2, pair, rows, :] = mm

    items = ([(g1_scores, g1_finish, (j,)) for j in range(tm // L)]
             + [(g0_scores, g0_finish, (j,)) for j in range(tm // L)])

    def start(item):
        scores_fn, finish_fn, args = item
        return finish_fn, [(a, scores_fn(a)) for a in args]

    def finish(pending):
        finish_fn, parts = pending
        for a, part in parts:
            finish_fn(a, part)

    in_flight = []
    pending_items = list(items)

    def rotate():
        in_flight.append(start(pending_items.pop(0)))
        if len(in_flight) > 2:
            finish(in_flight.pop(0))

    for c0 in range(0, CONV_WIDTH, CONV_CHUNK):
        cs = slice(c0, c0 + CONV_CHUNK)
        u = proj(OFF_GC + c0, CONV_CHUNK) * proj(OFF_UX + c0, CONV_CHUNK)
        u_scr[HALO:HALO + tm, cs] = u
        conv = cw_ref[0:1, cs] * u_scr[HALO - 2:HALO - 2 + tm, cs]
        conv = conv + cw_ref[1:2, cs] * u_scr[HALO - 1:HALO - 1 + tm, cs]
        conv = conv + cw_ref[2:3, cs] * u
        rotate()
        t = proj(OFF_GB + c0, CONV_CHUNK) * conv * _silu(proj(OFF_ZC + c0, CONV_CHUNK))
        t_scr[:, cs] = t.astype(BF16)
        rotate()
    finish(in_flight.pop(0))
    za_ref[...] = _silu(proj(OFF_Z_ATTN, GROUP_WIDTH)).astype(BF16)
    yc_ref[...] = jnp.dot(t_scr[...], wpc_ref[...], preferred_element_type=F32).astype(BF16)
    finish(in_flight.pop(0))
    project_qkv(2)

    g2_d = DILATED_GROUPS[2][1]
    n_t2 = tm // g2_d

    def g2_piece(which, r, pair):
        pieces = [qkv_scr[pl.ds(pl.multiple_of(t * tm + r * n_t2, n_t2), n_t2),
                          pl.ds(col(2, which, pair), PAIR_WIDTH)] for t in range(S // tm)]
        return jnp.concatenate(pieces, axis=0)

    @pl.when(tile == S // tm - 1)
    def _():
        def body(step, carry):
            work = [(u, pair) for u in range(RES_STRIDE) for pair in range(N_PAIRS)]
            parts = [probabilities(2, pair, g2_piece(0, step * RES_STRIDE + u, pair),
                                   g2_piece(1, step * RES_STRIDE + u, pair), False) for u, pair in work]
            for (u, pair), (p, m) in zip(work, parts):
                r = step * RES_STRIDE + u
                new = weighted_values(p, m, g2_piece(2, r, pair))
                outs = []
                for t in range(S // tm):
                    start_row = t * tm + u * (tm // RES_STRIDE) + step
                    rows = pl.ds(start_row, n_t2, stride=RES_STRIDE)
                    acc, den, _ = merged(pair, rows, *[x[t * n_t2:(t + 1) * n_t2] for x in new])
                    outs.append(acc / den)
                o_ref[pair, pl.ds(r, L, stride=g2_d), :] = jnp.concatenate(outs, axis=0)
            return carry
        lax.fori_loop(0, g2_d // RES_STRIDE, body, 0)


def _proj_attn(x, ada, w_in, b_in, conv_w, w_pc):
    bsz, s, d = x.shape
    tm = ROW_TILE
    bias = jnp.asarray(_attn_bias_table())
    const = dict(pipeline_mode=pl.Buffered(1))
    row = lambda width: pl.BlockSpec((None, tm, width), lambda b, i: (b, i, 0))
    return pl.pallas_call(
        _proj_attn_kernel,
        grid=(bsz, s // tm),
        in_specs=[row(d),
                  pl.BlockSpec(ada.shape, lambda b, i: (0, 0), **const),
                  pl.BlockSpec(memory_space=pl.ANY),
                  pl.BlockSpec((1, D_IN), lambda b, i: (0, 0), **const),
                  pl.BlockSpec((CONV_K, CONV_WIDTH), lambda b, i: (0, 0), **const),
                  pl.BlockSpec((CONV_WIDTH, d), lambda b, i: (0, 0), **const),
                  pl.BlockSpec(bias.shape, lambda b, i: (0, 0, 0, 0), **const)],
        out_specs=[row(GROUP_WIDTH), row(d),
                   pl.BlockSpec((None, N_PAIRS, s, PAIR_WIDTH), lambda b, i: (b, 0, 0, 0))],
        out_shape=[jax.ShapeDtypeStruct((bsz, s, GROUP_WIDTH), BF16),
                   jax.ShapeDtypeStruct((bsz, s, d), BF16),
                   jax.ShapeDtypeStruct((bsz, N_PAIRS, s, PAIR_WIDTH), F32)],
        scratch_shapes=[pltpu.VMEM((tm + HALO, CONV_WIDTH), F32),
                        pltpu.VMEM((tm, CONV_WIDTH), BF16),
                        pltpu.VMEM((N_STAGE_SLABS, tm, LANES), F32),
                        pltpu.VMEM((s, QKV_WIDTH), BF16),
                        pltpu.VMEM((3, N_PAIRS, s, PAIR_WIDTH), F32),
                        pltpu.VMEM((N_PAIRS * (tm // SUB_BLOCK), 3, SUB_BLOCK, PAIR_WIDTH), F32),
                        pltpu.VMEM((4, PAIR_WIDTH, 2 * SUB_BLOCK), BF16),
                        pltpu.VMEM((d, OFF_GA), BF16),
                        pltpu.VMEM((W_SLOTS, d, W_CHUNK), F32),
                        pltpu.SemaphoreType.DMA((W_SLOTS,))],
        compiler_params=pltpu.CompilerParams(
            dimension_semantics=("arbitrary", "arbitrary"), vmem_limit_bytes=PROJ_VMEM_LIMIT),
        name="proj_attn",
    )(x, ada, w_in, b_in, conv_w, w_pc, bias)


def _out_kernel(x_ref, ada_ref, oa_ref, za_ref, yc_ref, *rest, alpha):
    n_wg = 2 * D_MODEL // OUT_WG_BLOCK
    wg_f32 = rest[:n_wg]
    (b_ref, wpa_f32, wout_f32, bout_ref, lng_ref, lnb_ref,
     out_ref, m_scr, wg_ref, wpa_ref, wout_ref) = rest[n_wg:]

    @pl.when(jnp.logical_and(pl.program_id(0) == 0, pl.program_id(1) == 0))
    def _():
        for k, blk in enumerate(wg_f32):
            wg_ref[:, k * OUT_WG_BLOCK:(k + 1) * OUT_WG_BLOCK] = blk[...].astype(BF16)
        wpa_ref[...] = wpa_f32[...].astype(BF16)
        wout_ref[...] = wout_f32[...].astype(BF16)

    ada = ada_ref[pl.ds(pl.program_id(0), 1), :]
    shift = ada[:, 0:D_MODEL]
    scale = ada[:, D_MODEL:2 * D_MODEL]
    gate = ada[:, 2 * D_MODEL:] * (1.0 / alpha)
    eps = LN_EPS / (alpha * alpha)
    h = (x_ref[...] * (1.0 + scale) + shift).astype(BF16)
    o_attn = jnp.concatenate([oa_ref[pair] for pair in range(N_PAIRS)], axis=1)
    ta = (o_attn * za_ref[...].astype(F32)).astype(BF16)

    for c0 in range(0, D_MODEL, OUT_COL_CHUNK):
        cs = slice(c0, c0 + OUT_COL_CHUNK)
        gs = slice(D_MODEL + c0, D_MODEL + c0 + OUT_COL_CHUNK)
        g_a = _sigmoid(jnp.dot(h, wg_ref[:, cs], preferred_element_type=F32)
                       + b_ref[:, OFF_GA + c0:OFF_GA + c0 + OUT_COL_CHUNK])
        g_b = _sigmoid(jnp.dot(h, wg_ref[:, gs], preferred_element_type=F32)
                       + b_ref[:, OFF_GA + D_MODEL + c0:OFF_GA + D_MODEL + c0 + OUT_COL_CHUNK])
        y_attn = jnp.dot(ta, wpa_ref[:, cs], preferred_element_type=F32)
        m_scr[:, cs] = g_a.astype(BF16) * y_attn.astype(BF16) + g_b.astype(BF16) * yc_ref[:, cs]

    for r0 in range(0, x_ref.shape[0], OUT_CHUNK):
        rows = slice(r0, r0 + OUT_CHUNK)
        sub = gate * (jnp.dot(m_scr[rows, :], wout_ref[...], preferred_element_type=F32) + bout_ref[...])
        r = x_ref[rows, :] + sub
        mu = jnp.mean(r, axis=-1, keepdims=True)
        cen = r - mu
        var = jnp.mean(cen * cen, axis=-1, keepdims=True)
        out_ref[rows, :] = cen * lax.rsqrt(var + eps) * lng_ref[...] + lnb_ref[...]


def _out(x, ada, o_attn, za, yc, w_in, b_in, w_pa, w_out, b_out, ln_g, ln_b, alpha):
    bsz, s, d = x.shape
    tm = OUT_ROW_TILE
    const = dict(pipeline_mode=pl.Buffered(1))
    row = lambda width: pl.BlockSpec((None, tm, width), lambda b, i: (b, i, 0))
    vec = pl.BlockSpec((1, d), lambda b, i: (0, 0), **const)
    n_wg = 2 * d // OUT_WG_BLOCK
    wg_specs = [pl.BlockSpec((d, OUT_WG_BLOCK), lambda b, i, k=k: (0, OFF_GA // OUT_WG_BLOCK + k), **const)
                for k in range(n_wg)]
    return pl.pallas_call(
        functools.partial(_out_kernel, alpha=alpha),
        grid=(bsz, s // tm),
        in_specs=[row(d), pl.BlockSpec(ada.shape, lambda b, i: (0, 0), **const),
                  pl.BlockSpec((None, N_PAIRS, tm, PAIR_WIDTH), lambda b, i: (b, 0, i, 0)),
                  row(GROUP_WIDTH), row(d)]
                 + wg_specs
                 + [pl.BlockSpec((1, D_IN), lambda b, i: (0, 0), **const),
                    pl.BlockSpec((GROUP_WIDTH, d), lambda b, i: (0, 0), **const),
                    pl.BlockSpec((d, d), lambda b, i: (0, 0), **const),
                    vec, vec, vec],
        out_specs=row(d),
        out_shape=jax.ShapeDtypeStruct((bsz, s, d), F32),
        scratch_shapes=[pltpu.VMEM((tm, d), BF16),
                        pltpu.VMEM((d, 2 * d), BF16),
                        pltpu.VMEM((GROUP_WIDTH, d), BF16),
                        pltpu.VMEM((d, d), BF16)],
        compiler_params=pltpu.CompilerParams(
            dimension_semantics=("arbitrary", "arbitrary"), vmem_limit_bytes=VMEM_LIMIT),
        name="out",
    )(x, ada, o_attn, za, yc, *([w_in] * n_wg), b_in, w_pa, w_out, b_out, ln_g, ln_b)


def kernel(x, c, w_ada, b_ada, w_in, b_in, conv_w, w_proj_attn, w_proj_conv, w_out, b_out, ln_g, ln_b):
    bsz, s, d = x.shape
    depth = w_in.shape[0]
    assert d == D_MODEL and s % OUT_ROW_TILE == 0 and w_in.shape[1:] == (D_MODEL, D_IN)
    assert s // DILATED_GROUPS[-1][1] == SUB_BLOCK
    alpha = (2.0 * depth) ** 0.25
    for layer in range(depth):
        ada = _ada(c, w_ada[layer], b_ada[layer])
        b_in2 = b_in[layer].reshape(1, D_IN)
        za, yc, o_attn = _proj_attn(x, ada, w_in[layer], b_in2,
                                    conv_w[layer], w_proj_conv[layer].astype(BF16))
        x = _out(x, ada, o_attn, za, yc, w_in[layer], b_in2, w_proj_attn[layer], w_out[layer],
                 b_out[layer].reshape(1, d), ln_g[layer].reshape(1, d), ln_b[layer].reshape(1, d), alpha)
    return x
```

```python
import functools
import itertools

import numpy as np
import jax
import jax.numpy as jnp
from jax import lax
from jax.experimental import pallas as pl
from jax.experimental.pallas import tpu as pltpu

D_MODEL = 1024
HEAD_DIM = 64
HEADS_PER_GROUP = 4
DILATED_GROUPS = ((128, 1), (512, 4), (2048, 16))
N_GROUPS = len(DILATED_GROUPS)
N_ATTN_HEADS = N_GROUPS * HEADS_PER_GROUP
ATTN_WIDTH = N_ATTN_HEADS * HEAD_DIM
GROUP_WIDTH = HEADS_PER_GROUP * HEAD_DIM
QKV_WIDTH = 3 * ATTN_WIDTH
CONV_WIDTH = D_MODEL
CONV_K = 3
SUB_BLOCK = 128
ALIBI_MAX_EXP = 8.0
LN_EPS = 1e-5

OFF_Z_ATTN = QKV_WIDTH
OFF_UX = OFF_Z_ATTN + GROUP_WIDTH
OFF_GB = OFF_UX + CONV_WIDTH
OFF_GC = OFF_GB + CONV_WIDTH
OFF_ZC = OFF_GC + CONV_WIDTH
OFF_GA = OFF_ZC + CONV_WIDTH
D_IN = OFF_GA + 2 * D_MODEL

ROW_TILE = 512
OUT_ROW_TILE = 1024
OUT_CHUNK = 256
OUT_COL_CHUNK = 256
OUT_WG_BLOCK = 512
W_CHUNK = 32
W_SLOTS = 4
CONV_CHUNK = 256
LANES = 128
PAIR_WIDTH = 2 * HEAD_DIM
N_PAIRS = HEADS_PER_GROUP // 2
REGROUP_STRIDE = 4
RES_STRIDE = REGROUP_STRIDE
N_STAGE_SLABS = 6
HALO = 8
VMEM_LIMIT = 56 * 1024 * 1024
PROJ_VMEM_LIMIT = 60 * 1024 * 1024

F32 = jnp.float32
BF16 = jnp.bfloat16

assert PAIR_WIDTH == LANES and ROW_TILE // DILATED_GROUPS[1][1] == SUB_BLOCK
assert DILATED_GROUPS[1][1] == RES_STRIDE and DILATED_GROUPS[2][1] == RES_STRIDE * RES_STRIDE
assert 2 * (CONV_WIDTH // CONV_CHUNK) == 2 * (ROW_TILE // SUB_BLOCK)
assert OFF_GA % OUT_WG_BLOCK == 0 and D_MODEL % W_CHUNK == 0 and D_MODEL // W_CHUNK >= W_SLOTS


def _sigmoid(v):
    return 0.5 * jnp.tanh(0.5 * v) + 0.5


def _silu(v):
    return v * _sigmoid(v)


def _ada_kernel(c_ref, w_ref, b_ref, o_ref):
    o_ref[...] = jnp.dot(_silu(c_ref[...]).astype(BF16), w_ref[...].astype(BF16),
                         preferred_element_type=F32) + b_ref[...]


def _ada(c, w_ada, b_ada):
    bsz, d = c.shape
    n_out = w_ada.shape[1]
    return pl.pallas_call(
        _ada_kernel,
        grid=(n_out // d,),
        in_specs=[pl.BlockSpec((bsz, d), lambda j: (0, 0)),
                  pl.BlockSpec((d, d), lambda j: (0, j)),
                  pl.BlockSpec((1, d), lambda j: (0, j))],
        out_specs=pl.BlockSpec((bsz, d), lambda j: (0, j)),
        out_shape=jax.ShapeDtypeStruct((bsz, n_out), F32),
        name="ada",
    )(c, w_ada, b_ada.reshape(1, n_out))


def _attn_bias_table():
    L = SUB_BLOCK
    tabs = []
    for group, (window, dilation) in enumerate(DILATED_GROUPS):
        span = window // dilation
        heads = np.arange(group * HEADS_PER_GROUP, (group + 1) * HEADS_PER_GROUP, dtype=np.float64)
        slopes = 2.0 ** (-ALIBI_MAX_EXP * (heads + 1.0) / N_ATTN_HEADS)
        delta = (np.arange(L)[:, None] + L - np.arange(2 * L)[None, :]).astype(np.float64)
        valid = (delta >= 0) & (delta <= span)
        bias = np.where(valid[None], -slopes[:, None, None] * (delta * dilation)[None], -np.inf)
        tabs.append(bias.reshape(HEADS_PER_GROUP // 2, 2 * L, 2 * L))
    return np.stack(tabs).astype(np.float32)


def _proj_attn_kernel(x_ref, ada_ref, w_hbm, b_ref, cw_ref, wpc_ref, bias_ref,
                      za_ref, yc_ref, o_ref, u_scr, t_scr, stage_scr, qkv_scr, res, tmp_scr, kt_scr,
                      w_ref, w_stage, w_sem):
    tm = x_ref.shape[0]
    S = qkv_scr.shape[0]
    L = SUB_BLOCK
    tile = pl.program_id(1)
    first_tile = tile == 0
    base = pl.multiple_of(tile * tm, tm)
    prev_base = pl.multiple_of(jnp.maximum(tile - 1, 0) * tm, tm)

    def weight_chunk(k):
        return pltpu.make_async_copy(w_hbm.at[pl.ds(k * W_CHUNK, W_CHUNK), pl.ds(0, OFF_GA)],
                                     w_stage.at[k % W_SLOTS], w_sem.at[k % W_SLOTS])

    @pl.when(jnp.logical_and(pl.program_id(0) == 0, first_tile))
    def _():
        n_chunks = D_MODEL // W_CHUNK
        ahead = W_SLOTS - 1
        for k in range(ahead):
            weight_chunk(k).start()
        for k in range(n_chunks):
            if k + ahead < n_chunks:
                weight_chunk(k + ahead).start()
            weight_chunk(k).wait()
            w_ref[k * W_CHUNK:(k + 1) * W_CHUNK, :] = w_stage[k % W_SLOTS].astype(BF16)

    @pl.when(first_tile)
    def _():
        u_scr[0:HALO, :] = jnp.zeros((HALO, CONV_WIDTH), F32)

    @pl.when(jnp.logical_not(first_tile))
    def _():
        u_scr[0:HALO, :] = u_scr[tm:tm + HALO, :]

    ada = ada_ref[pl.ds(pl.program_id(0), 1), :]
    shift = ada[:, 0:D_MODEL]
    scale = ada[:, D_MODEL:2 * D_MODEL]
    h = (x_ref[...] * (1.0 + scale) + shift).astype(BF16)

    def proj(lo, width):
        return (jnp.dot(h, w_ref[:, lo:lo + width], preferred_element_type=F32)
                + b_ref[:, lo:lo + width])

    slabs = itertools.cycle(range(N_STAGE_SLABS))

    def project_qkv(g):
        d = DILATED_GROUPS[g][1]
        for which in range(3):
            c0 = which * ATTN_WIDTH + g * GROUP_WIDTH
            blk = proj(c0, GROUP_WIDTH)
            if d == 1:
                qkv_scr[pl.ds(base, tm), c0:c0 + GROUP_WIDTH] = blk.astype(BF16)
                continue
            n_t = tm // d
            for lo in range(0, GROUP_WIDTH, LANES):
                cols = slice(c0 + lo, c0 + lo + LANES)
                slab = next(slabs)
                stage_scr[slab] = blk[:, lo:lo + LANES]
                if d == REGROUP_STRIDE:
                    for r in range(d):
                        qkv_scr[pl.ds(base + r * n_t, n_t), cols] = (
                            stage_scr[slab, pl.ds(r, n_t, stride=d), :].astype(BF16))
                else:
                    assert d == REGROUP_STRIDE * REGROUP_STRIDE
                    n_1 = tm // REGROUP_STRIDE
                    slab2 = next(slabs)
                    for r1 in range(REGROUP_STRIDE):
                        stage_scr[slab2, r1 * n_1:(r1 + 1) * n_1, :] = (
                            stage_scr[slab, pl.ds(r1, n_1, stride=REGROUP_STRIDE), :])
                    for r1 in range(REGROUP_STRIDE):
                        for r2 in range(REGROUP_STRIDE):
                            r = r2 * REGROUP_STRIDE + r1
                            qkv_scr[pl.ds(base + r * n_t, n_t), cols] = stage_scr[
                                slab2, pl.ds(r1 * n_1 + r2, n_t, stride=REGROUP_STRIDE), :].astype(BF16)

    project_qkv(1)
    project_qkv(0)

    low_half = lax.broadcasted_iota(jnp.int32, (L, PAIR_WIDTH), 1) < HEAD_DIM
    no_prev = jnp.logical_and(lax.broadcasted_iota(jnp.int32, (2 * L, 2 * L), 1) < L, first_tile)

    def col(g, which, pair):
        return which * ATTN_WIDTH + g * GROUP_WIDTH + pair * PAIR_WIDTH

    kt_slots = itertools.cycle(range(kt_scr.shape[0]))

    def probabilities(g, pair, q, k, maybe_no_prev):
        q = q * 0.125
        zero = jnp.zeros_like(q)
        q2 = jnp.concatenate([jnp.where(low_half, q, zero), jnp.where(low_half, zero, q)], axis=0)
        slot = next(kt_slots)
        n_keys = k.shape[0]
        kt_scr[slot, :, :n_keys] = k.T
        sc = jnp.dot(q2, kt_scr[slot, :, :n_keys], preferred_element_type=F32)
        sc = sc + bias_ref[g, pair, :, 2 * L - k.shape[0]:]
        if maybe_no_prev:
            sc = jnp.where(no_prev, -jnp.inf, sc)
        m = jnp.max(sc, axis=-1, keepdims=True)
        return jnp.exp(sc - m).astype(BF16), m

    def weighted_values(p, m, v):
        v1 = jnp.concatenate([v, jnp.ones(v.shape, BF16)], axis=1)
        pv = jnp.dot(p, v1, preferred_element_type=F32)
        mb = jnp.broadcast_to(m, (2 * L, PAIR_WIDTH))
        return (jnp.where(low_half, pv[:L, :PAIR_WIDTH], pv[L:, :PAIR_WIDTH]),
                jnp.where(low_half, pv[:L, PAIR_WIDTH:], pv[L:, PAIR_WIDTH:]),
                jnp.where(low_half, mb[:L], mb[L:]))

    def merged(pair, rows, acc, den, m):
        m_old = res[2, pair, rows, :]
        m_new = jnp.maximum(m_old, m)
        a = jnp.exp(m_old - m_new)
        b = jnp.exp(m - m_new)
        return a * res[0, pair, rows, :] + b * acc, a * res[1, pair, rows, :] + b * den, m_new

    g1_d = DILATED_GROUPS[1][1]

    def g1_scores(r):
        out = []
        for pair in range(N_PAIRS):
            cur = pl.ds(base + r * L, L)
            prev = pl.ds(prev_base + r * L, L)
            k = jnp.concatenate([qkv_scr[prev, pl.ds(col(1, 1, pair), PAIR_WIDTH)],
                                 qkv_scr[cur, pl.ds(col(1, 1, pair), PAIR_WIDTH)]], axis=0)
            out.append(probabilities(1, pair, qkv_scr[cur, pl.ds(col(1, 0, pair), PAIR_WIDTH)], k, True))
        return out

    def g1_finish(r, parts):
        for pair, (p, m) in enumerate(parts):
            cur = pl.ds(base + r * L, L)
            prev = pl.ds(prev_base + r * L, L)
            v = jnp.concatenate([qkv_scr[prev, pl.ds(col(1, 2, pair), PAIR_WIDTH)],
                                 qkv_scr[cur, pl.ds(col(1, 2, pair), PAIR_WIDTH)]], axis=0)
            acc, den, mm = weighted_values(p, m, v)
            rows = pl.ds(base + r * L, L)
            res[0, pair, rows, :] = acc
            res[1, pair, rows, :] = den
            res[2, pair, rows, :] = mm

    def g0_rows(c):
        q0 = base + c * L
        k0 = jnp.maximum(q0 - L, 0) if c == 0 else q0 - L
        return pl.ds(pl.multiple_of(q0, L), L), pl.ds(pl.multiple_of(k0, L), L)

    def g0_scores(c):
        cur, prev = g0_rows(c)
        out = []
        for pair in range(N_PAIRS):
            k = jnp.concatenate([qkv_scr[prev, pl.ds(col(0, 1, pair), PAIR_WIDTH)],
                                 qkv_scr[cur, pl.ds(col(0, 1, pair), PAIR_WIDTH)]], axis=0)
            out.append(probabilities(0, pair, qkv_scr[cur, pl.ds(col(0, 0, pair), PAIR_WIDTH)], k, c == 0))
        return out

    def g0_finish(c, parts):
        cur, prev = g0_rows(c)
        for pair, (p, m) in enumerate(parts):
            v = jnp.concatenate([qkv_scr[prev, pl.ds(col(0, 2, pair), PAIR_WIDTH)],
                                 qkv_scr[cur, pl.ds(col(0, 2, pair), PAIR_WIDTH)]], axis=0)
            slab = c * N_PAIRS + pair
            for k, val in enumerate(weighted_values(p, m, v)):
                tmp_scr[slab, k] = val
            n_r = L // RES_STRIDE
            for r in range(RES_STRIDE):
                rows = pl.ds(pl.multiple_of(base + r * (tm // RES_STRIDE) + c * n_r, n_r), n_r)
                acc, den, mm = merged(pair, rows, *[tmp_scr[slab, k, pl.ds(r, n_r, stride=RES_STRIDE), :]
                                                    for k in range(3)])
                res[0, pair, rows, :] = acc
                res[1, pair, rows, :] = den
                res[2, pair, rows, :] = mm

    items = ([(g1_scores, g1_finish, (j,)) for j in range(tm // L)]
             + [(g0_scores, g0_finish, (j,)) for j in range(tm // L)])

    def start(item):
        scores_fn, finish_fn, args = item
        return finish_fn, [(a, scores_fn(a)) for a in args]

    def finish(pending):
        finish_fn, parts = pending
        for a, part in parts:
            finish_fn(a, part)

    in_flight = []
    pending_items = list(items)

    def rotate():
        in_flight.append(start(pending_items.pop(0)))
        if len(in_flight) > 2:
            finish(in_flight.pop(0))

    for c0 in range(0, CONV_WIDTH, CONV_CHUNK):
        cs = slice(c0, c0 + CONV_CHUNK)
        u = proj(OFF_GC + c0, CONV_CHUNK) * proj(OFF_UX + c0, CONV_CHUNK)
        u_scr[HALO:HALO + tm, cs] = u
        conv = cw_ref[0:1, cs] * u_scr[HALO - 2:HALO - 2 + tm, cs]
        conv = conv + cw_ref[1:2, cs] * u_scr[HALO - 1:HALO - 1 + tm, cs]
        conv = conv + cw_ref[2:3, cs] * u
        rotate()
        t = proj(OFF_GB + c0, CONV_CHUNK) * conv * _silu(proj(OFF_ZC + c0, CONV_CHUNK))
        t_scr[:, cs] = t.astype(BF16)
        rotate()
    finish(in_flight.pop(0))
    za_ref[...] = _silu(proj(OFF_Z_ATTN, GROUP_WIDTH)).astype(BF16)
    yc_ref[...] = jnp.dot(t_scr[...], wpc_ref[...], preferred_element_type=F32).astype(BF16)
    finish(in_flight.pop(0))
    project_qkv(2)

    g2_d = DILATED_GROUPS[2][1]
    n_t2 = tm // g2_d

    def g2_piece(which, r, pair):
        pieces = [qkv_scr[pl.ds(pl.multiple_of(t * tm + r * n_t2, n_t2), n_t2),
                          pl.ds(col(2, which, pair), PAIR_WIDTH)] for t in range(S // tm)]
        return jnp.concatenate(pieces, axis=0)

    @pl.when(tile == S // tm - 1)
    def _():
        def body(step, carry):
            work = [(u, pair) for u in range(RES_STRIDE) for pair in range(N_PAIRS)]
            parts = [probabilities(2, pair, g2_piece(0, step * RES_STRIDE + u, pair),
                                   g2_piece(1, step * RES_STRIDE + u, pair), False) for u, pair in work]
            for (u, pair), (p, m) in zip(work, parts):
                r = step * RES_STRIDE + u
                new = weighted_values(p, m, g2_piece(2, r, pair))
                outs = []
                for t in range(S // tm):
                    start_row = t * tm + u * (tm // RES_STRIDE) + step
                    rows = pl.ds(start_row, n_t2, stride=RES_STRIDE)
                    acc, den, _ = merged(pair, rows, *[x[t * n_t2:(t + 1) * n_t2] for x in new])
                    outs.append(acc / den)
                o_ref[pair, pl.ds(r, L, stride=g2_d), :] = jnp.concatenate(outs, axis=0)
            return carry
        lax.fori_loop(0, g2_d // RES_STRIDE, body, 0)


def _proj_attn(x, ada, w_in, b_in, conv_w, w_pc):
    bsz, s, d = x.shape
    tm = ROW_TILE
    bias = jnp.asarray(_attn_bias_table())
    const = dict(pipeline_mode=pl.Buffered(1))
    row = lambda width: pl.BlockSpec((None, tm, width), lambda b, i: (b, i, 0))
    return pl.pallas_call(
        _proj_attn_kernel,
        grid=(bsz, s // tm),
        in_specs=[row(d),
                  pl.BlockSpec(ada.shape, lambda b, i: (0, 0), **const),
                  pl.BlockSpec(memory_space=pl.ANY),
                  pl.BlockSpec((1, D_IN), lambda b, i: (0, 0), **const),
                  pl.BlockSpec((CONV_K, CONV_WIDTH), lambda b, i: (0, 0), **const),
                  pl.BlockSpec((CONV_WIDTH, d), lambda b, i: (0, 0), **const),
                  pl.BlockSpec(bias.shape, lambda b, i: (0, 0, 0, 0), **const)],
        out_specs=[row(GROUP_WIDTH), row(d),
                   pl.BlockSpec((None, N_PAIRS, s, PAIR_WIDTH), lambda b, i: (b, 0, 0, 0))],
        out_shape=[jax.ShapeDtypeStruct((bsz, s, GROUP_WIDTH), BF16),
                   jax.ShapeDtypeStruct((bsz, s, d), BF16),
                   jax.ShapeDtypeStruct((bsz, N_PAIRS, s, PAIR_WIDTH), F32)],
        scratch_shapes=[pltpu.VMEM((tm + HALO, CONV_WIDTH), F32),
                        pltpu.VMEM((tm, CONV_WIDTH), BF16),
                        pltpu.VMEM((N_STAGE_SLABS, tm, LANES), F32),
                        pltpu.VMEM((s, QKV_WIDTH), BF16),
                        pltpu.VMEM((3, N_PAIRS, s, PAIR_WIDTH), F32),
                        pltpu.VMEM((N_PAIRS * (tm // SUB_BLOCK), 3, SUB_BLOCK, PAIR_WIDTH), F32),
                        pltpu.VMEM((4, PAIR_WIDTH, 2 * SUB_BLOCK), BF16),
                        pltpu.VMEM((d, OFF_GA), BF16),
                        pltpu.VMEM((W_SLOTS, W_CHUNK, OFF_GA), F32),
                        pltpu.SemaphoreType.DMA((W_SLOTS,))],
        compiler_params=pltpu.CompilerParams(
            dimension_semantics=("arbitrary", "arbitrary"), vmem_limit_bytes=PROJ_VMEM_LIMIT),
        name="proj_attn",
    )(x, ada, w_in, b_in, conv_w, w_pc, bias)


def _out_kernel(x_ref, ada_ref, oa_ref, za_ref, yc_ref, *rest, alpha):
    n_wg = 2 * D_MODEL // OUT_WG_BLOCK
    wg_f32 = rest[:n_wg]
    (b_ref, wpa_f32, wout_f32, bout_ref, lng_ref, lnb_ref,
     out_ref, m_scr, wg_ref, wpa_ref, wout_ref) = rest[n_wg:]

    @pl.when(jnp.logical_and(pl.program_id(0) == 0, pl.program_id(1) == 0))
    def _():
        for k, blk in enumerate(wg_f32):
            wg_ref[:, k * OUT_WG_BLOCK:(k + 1) * OUT_WG_BLOCK] = blk[...].astype(BF16)
        wpa_ref[...] = wpa_f32[...].astype(BF16)
        wout_ref[...] = wout_f32[...].astype(BF16)

    ada = ada_ref[pl.ds(pl.program_id(0), 1), :]
    shift = ada[:, 0:D_MODEL]
    scale = ada[:, D_MODEL:2 * D_MODEL]
    gate = ada[:, 2 * D_MODEL:] * (1.0 / alpha)
    eps = LN_EPS / (alpha * alpha)
    h = (x_ref[...] * (1.0 + scale) + shift).astype(BF16)
    o_attn = jnp.concatenate([oa_ref[pair] for pair in range(N_PAIRS)], axis=1)
    ta = (o_attn * za_ref[...].astype(F32)).astype(BF16)

    for c0 in range(0, D_MODEL, OUT_COL_CHUNK):
        cs = slice(c0, c0 + OUT_COL_CHUNK)
        gs = slice(D_MODEL + c0, D_MODEL + c0 + OUT_COL_CHUNK)
        g_a = _sigmoid(jnp.dot(h, wg_ref[:, cs], preferred_element_type=F32)
                       + b_ref[:, OFF_GA + c0:OFF_GA + c0 + OUT_COL_CHUNK])
        g_b = _sigmoid(jnp.dot(h, wg_ref[:, gs], preferred_element_type=F32)
                       + b_ref[:, OFF_GA + D_MODEL + c0:OFF_GA + D_MODEL + c0 + OUT_COL_CHUNK])
        y_attn = jnp.dot(ta, wpa_ref[:, cs], preferred_element_type=F32)
        m_scr[:, cs] = g_a.astype(BF16) * y_attn.astype(BF16) + g_b.astype(BF16) * yc_ref[:, cs]

    for r0 in range(0, x_ref.shape[0], OUT_CHUNK):
        rows = slice(r0, r0 + OUT_CHUNK)
        sub = gate * (jnp.dot(m_scr[rows, :], wout_ref[...], preferred_element_type=F32) + bout_ref[...])
        r = x_ref[rows, :] + sub
        mu = jnp.mean(r, axis=-1, keepdims=True)
        cen = r - mu
        var = jnp.mean(cen * cen, axis=-1, keepdims=True)
        out_ref[rows, :] = cen * lax.rsqrt(var + eps) * lng_ref[...] + lnb_ref[...]


def _out(x, ada, o_attn, za, yc, w_in, b_in, w_pa, w_out, b_out, ln_g, ln_b, alpha):
    bsz, s, d = x.shape
    tm = OUT_ROW_TILE
    const = dict(pipeline_mode=pl.Buffered(1))
    row = lambda width: pl.BlockSpec((None, tm, width), lambda b, i: (b, i, 0))
    vec = pl.BlockSpec((1, d), lambda b, i: (0, 0), **const)
    n_wg = 2 * d // OUT_WG_BLOCK
    wg_specs = [pl.BlockSpec((d, OUT_WG_BLOCK), lambda b, i, k=k: (0, OFF_GA // OUT_WG_BLOCK + k), **const)
                for k in range(n_wg)]
    return pl.pallas_call(
        functools.partial(_out_kernel, alpha=alpha),
        grid=(bsz, s // tm),
        in_specs=[row(d), pl.BlockSpec(ada.shape, lambda b, i: (0, 0), **const),
                  pl.BlockSpec((None, N_PAIRS, tm, PAIR_WIDTH), lambda b, i: (b, 0, i, 0)),
                  row(GROUP_WIDTH), row(d)]
                 + wg_specs
                 + [pl.BlockSpec((1, D_IN), lambda b, i: (0, 0), **const),
                    pl.BlockSpec((GROUP_WIDTH, d), lambda b, i: (0, 0), **const),
                    pl.BlockSpec((d, d), lambda b, i: (0, 0), **const),
                    vec, vec, vec],
        out_specs=row(d),
        out_shape=jax.ShapeDtypeStruct((bsz, s, d), F32),
        scratch_shapes=[pltpu.VMEM((tm, d), BF16),
                        pltpu.VMEM((d, 2 * d), BF16),
                        pltpu.VMEM((GROUP_WIDTH, d), BF16),
                        pltpu.VMEM((d, d), BF16)],
        compiler_params=pltpu.CompilerParams(
            dimension_semantics=("arbitrary", "arbitrary"), vmem_limit_bytes=VMEM_LIMIT),
        name="out",
    )(x, ada, o_attn, za, yc, *([w_in] * n_wg), b_in, w_pa, w_out, b_out, ln_g, ln_b)


def kernel(x, c, w_ada, b_ada, w_in, b_in, conv_w, w_proj_attn, w_proj_conv, w_out, b_out, ln_g, ln_b):
    bsz, s, d = x.shape
    depth = w_in.shape[0]
    assert d == D_MODEL and s % OUT_ROW_TILE == 0 and w_in.shape[1:] == (D_MODEL, D_IN)
    assert s // DILATED_GROUPS[-1][1] == SUB_BLOCK
    alpha = (2.0 * depth) ** 0.25
    for layer in range(depth):
        ada = _ada(c, w_ada[layer], b_ada[layer])
        b_in2 = b_in[layer].reshape(1, D_IN)
        za, yc, o_attn = _proj_attn(x, ada, w_in[layer], b_in2,
                                    conv_w[layer], w_proj_conv[layer].astype(BF16))
        x = _out(x, ada, o_attn, za, yc, w_in[layer], b_in2, w_proj_attn[layer], w_out[layer],
                 b_out[layer].reshape(1, d), ln_g[layer].reshape(1, d), ln_b[layer].reshape(1, d), alpha)
    return x
```

```python
import functools
import itertools

import jax
import jax.numpy as jnp
from jax import lax
from jax.experimental import pallas as pl
from jax.experimental.pallas import tpu as pltpu

D_MODEL = 1024
HEAD_DIM = 64
HEADS_PER_GROUP = 4
DILATED_GROUPS = ((128, 1), (512, 4), (2048, 16))
N_GROUPS = len(DILATED_GROUPS)
N_ATTN_HEADS = N_GROUPS * HEADS_PER_GROUP
ATTN_WIDTH = N_ATTN_HEADS * HEAD_DIM
GROUP_WIDTH = HEADS_PER_GROUP * HEAD_DIM
QKV_WIDTH = 3 * ATTN_WIDTH
CONV_WIDTH = D_MODEL
CONV_K = 3
SUB_BLOCK = 128
ALIBI_MAX_EXP = 8.0
LN_EPS = 1e-5

OFF_Z_ATTN = QKV_WIDTH
OFF_UX = OFF_Z_ATTN + GROUP_WIDTH
OFF_GB = OFF_UX + CONV_WIDTH
OFF_GC = OFF_GB + CONV_WIDTH
OFF_ZC = OFF_GC + CONV_WIDTH
OFF_GA = OFF_ZC + CONV_WIDTH
D_IN = OFF_GA + 2 * D_MODEL

ROW_TILE = 512
OUT_ROW_TILE = 1024
OUT_CHUNK = 256
OUT_COL_CHUNK = 256
OUT_WG_BLOCK = 512
W_CHUNK = 32
W_SLOTS = 4
CONV_CHUNK = 256
LANES = 128
PAIR_WIDTH = 2 * HEAD_DIM
N_PAIRS = HEADS_PER_GROUP // 2
REGROUP_STRIDE = 4
RES_STRIDE = REGROUP_STRIDE
N_STAGE_SLABS = 6
HALO = 8
VMEM_LIMIT = 56 * 1024 * 1024
PROJ_VMEM_LIMIT = 60 * 1024 * 1024

F32 = jnp.float32
BF16 = jnp.bfloat16

assert PAIR_WIDTH == LANES and ROW_TILE // DILATED_GROUPS[1][1] == SUB_BLOCK
assert DILATED_GROUPS[1][1] == RES_STRIDE and DILATED_GROUPS[2][1] == RES_STRIDE * RES_STRIDE
assert 2 * (CONV_WIDTH // CONV_CHUNK) == 2 * (ROW_TILE // SUB_BLOCK)
assert OFF_GA % OUT_WG_BLOCK == 0 and D_MODEL % W_CHUNK == 0 and D_MODEL // W_CHUNK >= W_SLOTS
assert CONV_WIDTH % W_CHUNK == 0 and SUB_BLOCK & (SUB_BLOCK - 1) == 0


def _sigmoid(v):
    return 0.5 * jnp.tanh(0.5 * v) + 0.5


def _silu(v):
    return v * _sigmoid(v)


def _ada_kernel(c_ref, w_ref, b_ref, o_ref):
    o_ref[...] = jnp.dot(_silu(c_ref[...]).astype(BF16), w_ref[...].astype(BF16),
                         preferred_element_type=F32) + b_ref[...]


def _ada(c, w_ada, b_ada):
    bsz, d = c.shape
    n_out = w_ada.shape[1]
    return pl.pallas_call(
        _ada_kernel,
        grid=(n_out // d,),
        in_specs=[pl.BlockSpec((bsz, d), lambda j: (0, 0)),
                  pl.BlockSpec((d, d), lambda j: (0, j)),
                  pl.BlockSpec((1, d), lambda j: (0, j))],
        out_specs=pl.BlockSpec((bsz, d), lambda j: (0, j)),
        out_shape=jax.ShapeDtypeStruct((bsz, n_out), F32),
        name="ada",
    )(c, w_ada, b_ada.reshape(1, n_out))


def _proj_attn_kernel(x_ref, ada_ref, w_hbm, b_ref, cw_ref, wpc_hbm,
                      za_ref, yc_ref, o_ref, u_scr, t_scr, stage_scr, qkv_scr, res, tmp_scr, kt_scr,
                      w_ref, wpc_ref, w_stage, w_sem, bias_ref):
    tm = x_ref.shape[0]
    S = qkv_scr.shape[0]
    L = SUB_BLOCK
    tile = pl.program_id(1)
    first_tile = tile == 0
    base = pl.multiple_of(tile * tm, tm)
    prev_base = pl.multiple_of(jnp.maximum(tile - 1, 0) * tm, tm)

    n_in = D_MODEL // W_CHUNK
    n_chunks = n_in + CONV_WIDTH // W_CHUNK

    def weight_job(k):
        src, dst, k0 = (w_hbm, w_ref, k) if k < n_in else (wpc_hbm, wpc_ref, k - n_in)
        width = dst.shape[1]
        rows = pl.ds(k0 * W_CHUNK, W_CHUNK)
        slot = k % W_SLOTS
        copy = pltpu.make_async_copy(src.at[rows, pl.ds(0, width)],
                                     w_stage.at[slot, :, pl.ds(0, width)], w_sem.at[slot])
        return copy, dst, rows, slot, width

    @pl.when(jnp.logical_and(pl.program_id(0) == 0, first_tile))
    def _():
        ahead = W_SLOTS - 1
        for k in range(ahead):
            weight_job(k)[0].start()
        for k in range(n_chunks):
            if k + ahead < n_chunks:
                weight_job(k + ahead)[0].start()
            copy, dst, rows, slot, width = weight_job(k)
            copy.wait()
            dst[rows, :] = w_stage[slot, :, 0:width].astype(BF16)

        L2 = 2 * SUB_BLOCK
        row = lax.broadcasted_iota(jnp.int32, (L2, L2), 0)
        delta = (row & (SUB_BLOCK - 1)) + SUB_BLOCK - lax.broadcasted_iota(jnp.int32, (L2, L2), 1)
        for g, (window, d) in enumerate(DILATED_GROUPS):
            valid = jnp.logical_and(delta >= 0, delta <= window // d)
            dist = (delta * d).astype(F32)
            for pair in range(N_PAIRS):
                slopes = [2.0 ** (-ALIBI_MAX_EXP * (g * HEADS_PER_GROUP + 2 * pair + hh + 1.0) / N_ATTN_HEADS)
                          for hh in range(2)]
                slope = jnp.where(row >= SUB_BLOCK, slopes[1], slopes[0])
                bias_ref[g, pair] = jnp.where(valid, -slope * dist, -jnp.inf)

    @pl.when(first_tile)
    def _():
        u_scr[0:HALO, :] = jnp.zeros((HALO, CONV_WIDTH), F32)

    @pl.when(jnp.logical_not(first_tile))
    def _():
        u_scr[0:HALO, :] = u_scr[tm:tm + HALO, :]

    ada = ada_ref[pl.ds(pl.program_id(0), 1), :]
    shift = ada[:, 0:D_MODEL]
    scale = ada[:, D_MODEL:2 * D_MODEL]
    h = (x_ref[...] * (1.0 + scale) + shift).astype(BF16)

    def proj(lo, width):
        return (jnp.dot(h, w_ref[:, lo:lo + width], preferred_element_type=F32)
                + b_ref[:, lo:lo + width])

    slabs = itertools.cycle(range(N_STAGE_SLABS))

    def project_qkv(g):
        d = DILATED_GROUPS[g][1]
        for which in range(3):
            c0 = which * ATTN_WIDTH + g * GROUP_WIDTH
            blk = proj(c0, GROUP_WIDTH)
            if d == 1:
                qkv_scr[pl.ds(base, tm), c0:c0 + GROUP_WIDTH] = blk.astype(BF16)
                continue
            n_t = tm // d
            for lo in range(0, GROUP_WIDTH, LANES):
                cols = slice(c0 + lo, c0 + lo + LANES)
                slab = next(slabs)
                stage_scr[slab] = blk[:, lo:lo + LANES]
                if d == REGROUP_STRIDE:
                    for r in range(d):
                        qkv_scr[pl.ds(base + r * n_t, n_t), cols] = (
                            stage_scr[slab, pl.ds(r, n_t, stride=d), :].astype(BF16))
                else:
                    assert d == REGROUP_STRIDE * REGROUP_STRIDE
                    n_1 = tm // REGROUP_STRIDE
                    slab2 = next(slabs)
                    for r1 in range(REGROUP_STRIDE):
                        stage_scr[slab2, r1 * n_1:(r1 + 1) * n_1, :] = (
                            stage_scr[slab, pl.ds(r1, n_1, stride=REGROUP_STRIDE), :])
                    for r1 in range(REGROUP_STRIDE):
                        for r2 in range(REGROUP_STRIDE):
                            r = r2 * REGROUP_STRIDE + r1
                            qkv_scr[pl.ds(base + r * n_t, n_t), cols] = stage_scr[
                                slab2, pl.ds(r1 * n_1 + r2, n_t, stride=REGROUP_STRIDE), :].astype(BF16)

    project_qkv(1)
    project_qkv(0)

    low_half = lax.broadcasted_iota(jnp.int32, (L, PAIR_WIDTH), 1) < HEAD_DIM
    no_prev = jnp.logical_and(lax.broadcasted_iota(jnp.int32, (2 * L, 2 * L), 1) < L, first_tile)

    def col(g, which, pair):
        return which * ATTN_WIDTH + g * GROUP_WIDTH + pair * PAIR_WIDTH

    kt_slots = itertools.cycle(range(kt_scr.shape[0]))

    def probabilities(g, pair, q, k, maybe_no_prev):
        q = q * 0.125
        zero = jnp.zeros_like(q)
        q2 = jnp.concatenate([jnp.where(low_half, q, zero), jnp.where(low_half, zero, q)], axis=0)
        slot = next(kt_slots)
        n_keys = k.shape[0]
        kt_scr[slot, :, :n_keys] = k.T
        sc = jnp.dot(q2, kt_scr[slot, :, :n_keys], preferred_element_type=F32)
        sc = sc + bias_ref[g, pair, :, 2 * L - k.shape[0]:]
        if maybe_no_prev:
            sc = jnp.where(no_prev, -jnp.inf, sc)
        m = jnp.max(sc, axis=-1, keepdims=True)
        return jnp.exp(sc - m).astype(BF16), m

    def weighted_values(p, m, v):
        v1 = jnp.concatenate([v, jnp.ones(v.shape, BF16)], axis=1)
        pv = jnp.dot(p, v1, preferred_element_type=F32)
        mb = jnp.broadcast_to(m, (2 * L, PAIR_WIDTH))
        return (jnp.where(low_half, pv[:L, :PAIR_WIDTH], pv[L:, :PAIR_WIDTH]),
                jnp.where(low_half, pv[:L, PAIR_WIDTH:], pv[L:, PAIR_WIDTH:]),
                jnp.where(low_half, mb[:L], mb[L:]))

    def merged(pair, rows, acc, den, m):
        m_old = res[2, pair, rows, :]
        m_new = jnp.maximum(m_old, m)
        a = jnp.exp(m_old - m_new)
        b = jnp.exp(m - m_new)
        return a * res[0, pair, rows, :] + b * acc, a * res[1, pair, rows, :] + b * den, m_new

    g1_d = DILATED_GROUPS[1][1]

    def g1_scores(r):
        out = []
        for pair in range(N_PAIRS):
            cur = pl.ds(base + r * L, L)
            prev = pl.ds(prev_base + r * L, L)
            k = jnp.concatenate([qkv_scr[prev, pl.ds(col(1, 1, pair), PAIR_WIDTH)],
                                 qkv_scr[cur, pl.ds(col(1, 1, pair), PAIR_WIDTH)]], axis=0)
            out.append(probabilities(1, pair, qkv_scr[cur, pl.ds(col(1, 0, pair), PAIR_WIDTH)], k, True))
        return out

    def g1_finish(r, parts):
        for pair, (p, m) in enumerate(parts):
            cur = pl.ds(base + r * L, L)
            prev = pl.ds(prev_base + r * L, L)
            v = jnp.concatenate([qkv_scr[prev, pl.ds(col(1, 2, pair), PAIR_WIDTH)],
                                 qkv_scr[cur, pl.ds(col(1, 2, pair), PAIR_WIDTH)]], axis=0)
            acc, den, mm = weighted_values(p, m, v)
            rows = pl.ds(base + r * L, L)
            res[0, pair, rows, :] = acc
            res[1, pair, rows, :] = den
            res[2, pair, rows, :] = mm

    def g0_rows(c):
        q0 = base + c * L
        k0 = jnp.maximum(q0 - L, 0) if c == 0 else q0 - L
        return pl.ds(pl.multiple_of(q0, L), L), pl.ds(pl.multiple_of(k0, L), L)

    def g0_scores(c):
        cur, prev = g0_rows(c)
        out = []
        for pair in range(N_PAIRS):
            k = jnp.concatenate([qkv_scr[prev, pl.ds(col(0, 1, pair), PAIR_WIDTH)],
                                 qkv_scr[cur, pl.ds(col(0, 1, pair), PAIR_WIDTH)]], axis=0)
            out.append(probabilities(0, pair, qkv_scr[cur, pl.ds(col(0, 0, pair), PAIR_WIDTH)], k, c == 0))
        return out

    def g0_finish(c, parts):
        cur, prev = g0_rows(c)
        for pair, (p, m) in enumerate(parts):
            v = jnp.concatenate([qkv_scr[prev, pl.ds(col(0, 2, pair), PAIR_WIDTH)],
                                 qkv_scr[cur, pl.ds(col(0, 2, pair), PAIR_WIDTH)]], axis=0)
            slab = c * N_PAIRS + pair
            for k, val in enumerate(weighted_values(p, m, v)):
                tmp_scr[slab, k] = val
            n_r = L // RES_STRIDE
            for r in range(RES_STRIDE):
                rows = pl.ds(pl.multiple_of(base + r * (tm // RES_STRIDE) + c * n_r, n_r), n_r)
                acc, den, mm = merged(pair, rows, *[tmp_scr[slab, k, pl.ds(r, n_r, stride=RES_STRIDE), :]
                                                    for k in range(3)])
                res[0, pair, rows, :] = acc
                res[1, pair, rows, :] = den
                res[2, pair, rows, :] = mm

    items = ([(g1_scores, g1_finish, (j,)) for j in range(tm // L)]
             + [(g0_scores, g0_finish, (j,)) for j in range(tm // L)])

    def start(item):
        scores_fn, finish_fn, args = item
        return finish_fn, [(a, scores_fn(a)) for a in args]

    def finish(pending):
        finish_fn, parts = pending
        for a, part in parts:
            finish_fn(a, part)

    in_flight = []
    pending_items = list(items)

    def rotate():
        in_flight.append(start(pending_items.pop(0)))
        if len(in_flight) > 2:
            finish(in_flight.pop(0))

    for c0 in range(0, CONV_WIDTH, CONV_CHUNK):
        cs = slice(c0, c0 + CONV_CHUNK)
        u = proj(OFF_GC + c0, CONV_CHUNK) * proj(OFF_UX + c0, CONV_CHUNK)
        u_scr[HALO:HALO + tm, cs] = u
        conv = cw_ref[0:1, cs] * u_scr[HALO - 2:HALO - 2 + tm, cs]
        conv = conv + cw_ref[1:2, cs] * u_scr[HALO - 1:HALO - 1 + tm, cs]
        conv = conv + cw_ref[2:3, cs] * u
        rotate()
        t = proj(OFF_GB + c0, CONV_CHUNK) * conv * _silu(proj(OFF_ZC + c0, CONV_CHUNK))
        t_scr[:, cs] = t.astype(BF16)
        rotate()
    finish(in_flight.pop(0))
    za_ref[...] = _silu(proj(OFF_Z_ATTN, GROUP_WIDTH)).astype(BF16)
    yc_ref[...] = jnp.dot(t_scr[...], wpc_ref[...], preferred_element_type=F32).astype(BF16)
    finish(in_flight.pop(0))
    project_qkv(2)

    g2_d = DILATED_GROUPS[2][1]
    n_t2 = tm // g2_d

    def g2_piece(which, r, pair):
        pieces = [qkv_scr[pl.ds(pl.multiple_of(t * tm + r * n_t2, n_t2), n_t2),
                          pl.ds(col(2, which, pair), PAIR_WIDTH)] for t in range(S // tm)]
        return jnp.concatenate(pieces, axis=0)

    @pl.when(tile == S // tm - 1)
    def _():
        def body(step, carry):
            work = [(u, pair) for u in range(RES_STRIDE) for pair in range(N_PAIRS)]
            parts = [probabilities(2, pair, g2_piece(0, step * RES_STRIDE + u, pair),
                                   g2_piece(1, step * RES_STRIDE + u, pair), False) for u, pair in work]
            for (u, pair), (p, m) in zip(work, parts):
                r = step * RES_STRIDE + u
                new = weighted_values(p, m, g2_piece(2, r, pair))
                outs = []
                for t in range(S // tm):
                    start_row = t * tm + u * (tm // RES_STRIDE) + step
                    rows = pl.ds(start_row, n_t2, stride=RES_STRIDE)
                    acc, den, _ = merged(pair, rows, *[x[t * n_t2:(t + 1) * n_t2] for x in new])
                    outs.append(acc / den)
                o_ref[pair, pl.ds(r, L, stride=g2_d), :] = jnp.concatenate(outs, axis=0)
            return carry
        lax.fori_loop(0, g2_d // RES_STRIDE, body, 0)


def _proj_attn(x, ada, w_in, b_in, conv_w, w_pc):
    bsz, s, d = x.shape
    tm = ROW_TILE
    const = dict(pipeline_mode=pl.Buffered(1))
    row = lambda width: pl.BlockSpec((None, tm, width), lambda b, i: (b, i, 0))
    return pl.pallas_call(
        _proj_attn_kernel,
        grid=(bsz, s // tm),
        in_specs=[row(d),
                  pl.BlockSpec(ada.shape, lambda b, i: (0, 0), **const),
                  pl.BlockSpec(memory_space=pl.ANY),
                  pl.BlockSpec((1, D_IN), lambda b, i: (0, 0), **const),
                  pl.BlockSpec((CONV_K, CONV_WIDTH), lambda b, i: (0, 0), **const),
                  pl.BlockSpec(memory_space=pl.ANY)],
        out_specs=[row(GROUP_WIDTH), row(d),
                   pl.BlockSpec((None, N_PAIRS, s, PAIR_WIDTH), lambda b, i: (b, 0, 0, 0))],
        out_shape=[jax.ShapeDtypeStruct((bsz, s, GROUP_WIDTH), BF16),
                   jax.ShapeDtypeStruct((bsz, s, d), BF16),
                   jax.ShapeDtypeStruct((bsz, N_PAIRS, s, PAIR_WIDTH), F32)],
        scratch_shapes=[pltpu.VMEM((tm + HALO, CONV_WIDTH), F32),
                        pltpu.VMEM((tm, CONV_WIDTH), BF16),
                        pltpu.VMEM((N_STAGE_SLABS, tm, LANES), F32),
                        pltpu.VMEM((s, QKV_WIDTH), BF16),
                        pltpu.VMEM((3, N_PAIRS, s, PAIR_WIDTH), F32),
                        pltpu.VMEM((N_PAIRS * (tm // SUB_BLOCK), 3, SUB_BLOCK, PAIR_WIDTH), F32),
                        pltpu.VMEM((4, PAIR_WIDTH, 2 * SUB_BLOCK), BF16),
                        pltpu.VMEM((d, OFF_GA), BF16),
                        pltpu.VMEM((CONV_WIDTH, d), BF16),
                        pltpu.VMEM((W_SLOTS, W_CHUNK, OFF_GA), F32),
                        pltpu.SemaphoreType.DMA((W_SLOTS,)),
                        pltpu.VMEM((N_GROUPS, N_PAIRS, 2 * SUB_BLOCK, 2 * SUB_BLOCK), F32)],
        compiler_params=pltpu.CompilerParams(
            dimension_semantics=("arbitrary", "arbitrary"), vmem_limit_bytes=PROJ_VMEM_LIMIT),
        name="proj_attn",
    )(x, ada, w_in, b_in, conv_w, w_pc)


def _out_kernel(x_ref, ada_ref, oa_ref, za_ref, yc_ref, *rest, alpha):
    n_wg = 2 * D_MODEL // OUT_WG_BLOCK
    wg_f32 = rest[:n_wg]
    (b_ref, wpa_f32, wout_f32, bout_ref, lng_ref, lnb_ref,
     out_ref, m_scr, wg_ref, wpa_ref, wout_ref) = rest[n_wg:]

    @pl.when(jnp.logical_and(pl.program_id(0) == 0, pl.program_id(1) == 0))
    def _():
        for k, blk in enumerate(wg_f32):
            wg_ref[:, k * OUT_WG_BLOCK:(k + 1) * OUT_WG_BLOCK] = blk[...].astype(BF16)
        wpa_ref[...] = wpa_f32[...].astype(BF16)
        wout_ref[...] = wout_f32[...].astype(BF16)

    ada = ada_ref[pl.ds(pl.program_id(0), 1), :]
    shift = ada[:, 0:D_MODEL]
    scale = ada[:, D_MODEL:2 * D_MODEL]
    gate = ada[:, 2 * D_MODEL:] * (1.0 / alpha)
    eps = LN_EPS / (alpha * alpha)
    h = (x_ref[...] * (1.0 + scale) + shift).astype(BF16)
    o_attn = jnp.concatenate([oa_ref[pair] for pair in range(N_PAIRS)], axis=1)
    ta = (o_attn * za_ref[...].astype(F32)).astype(BF16)

    for c0 in range(0, D_MODEL, OUT_COL_CHUNK):
        cs = slice(c0, c0 + OUT_COL_CHUNK)
        gs = slice(D_MODEL + c0, D_MODEL + c0 + OUT_COL_CHUNK)
        g_a = _sigmoid(jnp.dot(h, wg_ref[:, cs], preferred_element_type=F32)
                       + b_ref[:, OFF_GA + c0:OFF_GA + c0 + OUT_COL_CHUNK])
        g_b = _sigmoid(jnp.dot(h, wg_ref[:, gs], preferred_element_type=F32)
                       + b_ref[:, OFF_GA + D_MODEL + c0:OFF_GA + D_MODEL + c0 + OUT_COL_CHUNK])
        y_attn = jnp.dot(ta, wpa_ref[:, cs], preferred_element_type=F32)
        m_scr[:, cs] = g_a.astype(BF16) * y_attn.astype(BF16) + g_b.astype(BF16) * yc_ref[:, cs]

    for r0 in range(0, x_ref.shape[0], OUT_CHUNK):
        rows = slice(r0, r0 + OUT_CHUNK)
        sub = gate * (jnp.dot(m_scr[rows, :], wout_ref[...], preferred_element_type=F32) + bout_ref[...])
        r = x_ref[rows, :] + sub
        mu = jnp.mean(r, axis=-1, keepdims=True)
        cen = r - mu
        var = jnp.mean(cen * cen, axis=-1, keepdims=True)
        out_ref[rows, :] = cen * lax.rsqrt(var + eps) * lng_ref[...] + lnb_ref[...]


def _out(x, ada, o_attn, za, yc, w_in, b_in, w_pa, w_out, b_out, ln_g, ln_b, alpha):
    bsz, s, d = x.shape
    tm = OUT_ROW_TILE
    const = dict(pipeline_mode=pl.Buffered(1))
    row = lambda width: pl.BlockSpec((None, tm, width), lambda b, i: (b, i, 0))
    vec = pl.BlockSpec((1, d), lambda b, i: (0, 0), **const)
    n_wg = 2 * d // OUT_WG_BLOCK
    wg_specs = [pl.BlockSpec((d, OUT_WG_BLOCK), lambda b, i, k=k: (0, OFF_GA // OUT_WG_BLOCK + k), **const)
                for k in range(n_wg)]
    return pl.pallas_call(
        functools.partial(_out_kernel, alpha=alpha),
        grid=(bsz, s // tm),
        in_specs=[row(d), pl.BlockSpec(ada.shape, lambda b, i: (0, 0), **const),
                  pl.BlockSpec((None, N_PAIRS, tm, PAIR_WIDTH), lambda b, i: (b, 0, i, 0)),
                  row(GROUP_WIDTH), row(d)]
                 + wg_specs
                 + [pl.BlockSpec((1, D_IN), lambda b, i: (0, 0), **const),
                    pl.BlockSpec((GROUP_WIDTH, d), lambda b, i: (0, 0), **const),
                    pl.BlockSpec((d, d), lambda b, i: (0, 0), **const),
                    vec, vec, vec],
        out_specs=row(d),
        out_shape=jax.ShapeDtypeStruct((bsz, s, d), F32),
        scratch_shapes=[pltpu.VMEM((tm, d), BF16),
                        pltpu.VMEM((d, 2 * d), BF16),
                        pltpu.VMEM((GROUP_WIDTH, d), BF16),
                        pltpu.VMEM((d, d), BF16)],
        compiler_params=pltpu.CompilerParams(
            dimension_semantics=("arbitrary", "arbitrary"), vmem_limit_bytes=VMEM_LIMIT),
        name="out",
    )(x, ada, o_attn, za, yc, *([w_in] * n_wg), b_in, w_pa, w_out, b_out, ln_g, ln_b)


def kernel(x, c, w_ada, b_ada, w_in, b_in, conv_w, w_proj_attn, w_proj_conv, w_out, b_out, ln_g, ln_b):
    bsz, s, d = x.shape
    depth = w_in.shape[0]
    assert d == D_MODEL and s % OUT_ROW_TILE == 0 and w_in.shape[1:] == (D_MODEL, D_IN)
    assert s // DILATED_GROUPS[-1][1] == SUB_BLOCK
    alpha = (2.0 * depth) ** 0.25
    for layer in range(depth):
        ada = _ada(c, w_ada[layer], b_ada[layer])
        b_in2 = b_in[layer].reshape(1, D_IN)
        za, yc, o_attn = _proj_attn(x, ada, w_in[layer], b_in2,
                                    conv_w[layer], w_proj_conv[layer])
        x = _out(x, ada, o_attn, za, yc, w_in[layer], b_in2, w_proj_attn[layer], w_out[layer],
                 b_out[layer].reshape(1, d), ln_g[layer].reshape(1, d), ln_b[layer].reshape(1, d), alpha)
    return x
```

```python
import functools
import itertools

import numpy as np
import jax
import jax.numpy as jnp
from jax import lax
from jax.experimental import pallas as pl
from jax.experimental.pallas import tpu as pltpu

D_MODEL = 1024
HEAD_DIM = 64
HEADS_PER_GROUP = 4
DILATED_GROUPS = ((128, 1), (512, 4), (2048, 16))
N_GROUPS = len(DILATED_GROUPS)
N_ATTN_HEADS = N_GROUPS * HEADS_PER_GROUP
ATTN_WIDTH = N_ATTN_HEADS * HEAD_DIM
GROUP_WIDTH = HEADS_PER_GROUP * HEAD_DIM
QKV_WIDTH = 3 * ATTN_WIDTH
CONV_WIDTH = D_MODEL
CONV_K = 3
SUB_BLOCK = 128
ALIBI_MAX_EXP = 8.0
LN_EPS = 1e-5

OFF_Z_ATTN = QKV_WIDTH
OFF_UX = OFF_Z_ATTN + GROUP_WIDTH
OFF_GB = OFF_UX + CONV_WIDTH
OFF_GC = OFF_GB + CONV_WIDTH
OFF_ZC = OFF_GC + CONV_WIDTH
OFF_GA = OFF_ZC + CONV_WIDTH
D_IN = OFF_GA + 2 * D_MODEL

ROW_TILE = 512
OUT_ROW_TILE = 1024
OUT_CHUNK = 256
OUT_COL_CHUNK = 256
OUT_WG_BLOCK = 512
W_CHUNK = 32
W_SLOTS = 4
CONV_CHUNK = 256
LANES = 128
PAIR_WIDTH = 2 * HEAD_DIM
N_PAIRS = HEADS_PER_GROUP // 2
REGROUP_STRIDE = 4
RES_STRIDE = REGROUP_STRIDE
N_STAGE_SLABS = 6
HALO = 8
VMEM_LIMIT = 56 * 1024 * 1024
PROJ_VMEM_LIMIT = 60 * 1024 * 1024

F32 = jnp.float32
BF16 = jnp.bfloat16

assert PAIR_WIDTH == LANES and ROW_TILE // DILATED_GROUPS[1][1] == SUB_BLOCK
assert DILATED_GROUPS[1][1] == RES_STRIDE and DILATED_GROUPS[2][1] == RES_STRIDE * RES_STRIDE
assert 2 * (CONV_WIDTH // CONV_CHUNK) == 2 * (ROW_TILE // SUB_BLOCK)
assert OFF_GA % OUT_WG_BLOCK == 0 and D_MODEL % W_CHUNK == 0 and D_MODEL // W_CHUNK >= W_SLOTS


def _sigmoid(v):
    return 0.5 * jnp.tanh(0.5 * v) + 0.5


def _silu(v):
    return v * _sigmoid(v)


def _ada_kernel(c_ref, w_ref, b_ref, o_ref):
    o_ref[...] = jnp.dot(_silu(c_ref[...]).astype(BF16), w_ref[...].astype(BF16),
                         preferred_element_type=F32) + b_ref[...]


def _ada(c, w_ada, b_ada):
    bsz, d = c.shape
    n_out = w_ada.shape[1]
    return pl.pallas_call(
        _ada_kernel,
        grid=(n_out // d,),
        in_specs=[pl.BlockSpec((bsz, d), lambda j: (0, 0)),
                  pl.BlockSpec((d, d), lambda j: (0, j)),
                  pl.BlockSpec((1, d), lambda j: (0, j))],
        out_specs=pl.BlockSpec((bsz, d), lambda j: (0, j)),
        out_shape=jax.ShapeDtypeStruct((bsz, n_out), F32),
        name="ada",
    )(c, w_ada, b_ada.reshape(1, n_out))


def _attn_bias_table():
    L = SUB_BLOCK
    tabs = []
    for group, (window, dilation) in enumerate(DILATED_GROUPS):
        span = window // dilation
        heads = np.arange(group * HEADS_PER_GROUP, (group + 1) * HEADS_PER_GROUP, dtype=np.float64)
        slopes = 2.0 ** (-ALIBI_MAX_EXP * (heads + 1.0) / N_ATTN_HEADS)
        delta = (np.arange(L)[:, None] + L - np.arange(2 * L)[None, :]).astype(np.float64)
        valid = (delta >= 0) & (delta <= span)
        bias = np.where(valid[None], -slopes[:, None, None] * (delta * dilation)[None], -np.inf)
        tabs.append(bias.reshape(HEADS_PER_GROUP // 2, 2 * L, 2 * L))
    return np.stack(tabs).astype(np.float32)


def _proj_attn_kernel(x_ref, ada_ref, w_hbm, b_ref, cw_ref, wpc_ref, bias_ref,
                      za_ref, yc_ref, o_ref, u_scr, t_scr, stage_scr, qkv_scr, res, tmp_scr, kt_scr,
                      w_ref, w_stage, w_sem):
    tm = x_ref.shape[0]
    S = qkv_scr.shape[0]
    L = SUB_BLOCK
    tile = pl.program_id(1)
    first_tile = tile == 0
    base = pl.multiple_of(tile * tm, tm)
    prev_base = pl.multiple_of(jnp.maximum(tile - 1, 0) * tm, tm)

    def weight_chunk(k):
        return pltpu.make_async_copy(w_hbm.at[pl.ds(k * W_CHUNK, W_CHUNK), pl.ds(0, OFF_GA)],
                                     w_stage.at[k % W_SLOTS], w_sem.at[k % W_SLOTS])

    @pl.when(jnp.logical_and(pl.program_id(0) == 0, first_tile))
    def _():
        n_chunks = D_MODEL // W_CHUNK
        ahead = W_SLOTS - 1
        for k in range(ahead):
            weight_chunk(k).start(priority=k % 2)
        for k in range(n_chunks):
            if k + ahead < n_chunks:
                weight_chunk(k + ahead).start(priority=(k + ahead) % 2)
            weight_chunk(k).wait()
            w_ref[k * W_CHUNK:(k + 1) * W_CHUNK, :] = w_stage[k % W_SLOTS].astype(BF16)

    @pl.when(first_tile)
    def _():
        u_scr[0:HALO, :] = jnp.zeros((HALO, CONV_WIDTH), F32)

    @pl.when(jnp.logical_not(first_tile))
    def _():
        u_scr[0:HALO, :] = u_scr[tm:tm + HALO, :]

    ada = ada_ref[pl.ds(pl.program_id(0), 1), :]
    shift = ada[:, 0:D_MODEL]
    scale = ada[:, D_MODEL:2 * D_MODEL]
    h = (x_ref[...] * (1.0 + scale) + shift).astype(BF16)

    def proj(lo, width):
        return (jnp.dot(h, w_ref[:, lo:lo + width], preferred_element_type=F32)
                + b_ref[:, lo:lo + width])

    slabs = itertools.cycle(range(N_STAGE_SLABS))

    def project_qkv(g):
        d = DILATED_GROUPS[g][1]
        for which in range(3):
            c0 = which * ATTN_WIDTH + g * GROUP_WIDTH
            blk = proj(c0, GROUP_WIDTH)
            if d == 1:
                qkv_scr[pl.ds(base, tm), c0:c0 + GROUP_WIDTH] = blk.astype(BF16)
                continue
            n_t = tm // d
            for lo in range(0, GROUP_WIDTH, LANES):
                cols = slice(c0 + lo, c0 + lo + LANES)
                slab = next(slabs)
                stage_scr[slab] = blk[:, lo:lo + LANES]
                if d == REGROUP_STRIDE:
                    for r in range(d):
                        qkv_scr[pl.ds(base + r * n_t, n_t), cols] = (
                            stage_scr[slab, pl.ds(r, n_t, stride=d), :].astype(BF16))
                else:
                    assert d == REGROUP_STRIDE * REGROUP_STRIDE
                    n_1 = tm // REGROUP_STRIDE
                    slab2 = next(slabs)
                    for r1 in range(REGROUP_STRIDE):
                        stage_scr[slab2, r1 * n_1:(r1 + 1) * n_1, :] = (
                            stage_scr[slab, pl.ds(r1, n_1, stride=REGROUP_STRIDE), :])
                    for r1 in range(REGROUP_STRIDE):
                        for r2 in range(REGROUP_STRIDE):
                            r = r2 * REGROUP_STRIDE + r1
                            qkv_scr[pl.ds(base + r * n_t, n_t), cols] = stage_scr[
                                slab2, pl.ds(r1 * n_1 + r2, n_t, stride=REGROUP_STRIDE), :].astype(BF16)

    project_qkv(1)
    project_qkv(0)

    low_half = lax.broadcasted_iota(jnp.int32, (L, PAIR_WIDTH), 1) < HEAD_DIM
    no_prev = jnp.logical_and(lax.broadcasted_iota(jnp.int32, (2 * L, 2 * L), 1) < L, first_tile)

    def col(g, which, pair):
        return which * ATTN_WIDTH + g * GROUP_WIDTH + pair * PAIR_WIDTH

    kt_slots = itertools.cycle(range(kt_scr.shape[0]))

    def probabilities(g, pair, q, k, maybe_no_prev):
        q = q * 0.125
        zero = jnp.zeros_like(q)
        q2 = jnp.concatenate([jnp.where(low_half, q, zero), jnp.where(low_half, zero, q)], axis=0)
        slot = next(kt_slots)
        n_keys = k.shape[0]
        kt_scr[slot, :, :n_keys] = k.T
        sc = jnp.dot(q2, kt_scr[slot, :, :n_keys], preferred_element_type=F32)
        sc = sc + bias_ref[g, pair, :, 2 * L - k.shape[0]:]
        if maybe_no_prev:
            sc = jnp.where(no_prev, -jnp.inf, sc)
        m = jnp.max(sc, axis=-1, keepdims=True)
        return jnp.exp(sc - m).astype(BF16), m

    def weighted_values(p, m, v):
        v1 = jnp.concatenate([v, jnp.ones(v.shape, BF16)], axis=1)
        pv = jnp.dot(p, v1, preferred_element_type=F32)
        mb = jnp.broadcast_to(m, (2 * L, PAIR_WIDTH))
        return (jnp.where(low_half, pv[:L, :PAIR_WIDTH], pv[L:, :PAIR_WIDTH]),
                jnp.where(low_half, pv[:L, PAIR_WIDTH:], pv[L:, PAIR_WIDTH:]),
                jnp.where(low_half, mb[:L], mb[L:]))

    def merged(pair, rows, acc, den, m):
        m_old = res[2, pair, rows, :]
        m_new = jnp.maximum(m_old, m)
        a = jnp.exp(m_old - m_new)
        b = jnp.exp(m - m_new)
        return a * res[0, pair, rows, :] + b * acc, a * res[1, pair, rows, :] + b * den, m_new

    g1_d = DILATED_GROUPS[1][1]

    def g1_scores(r):
        out = []
        for pair in range(N_PAIRS):
            cur = pl.ds(base + r * L, L)
            prev = pl.ds(prev_base + r * L, L)
            k = jnp.concatenate([qkv_scr[prev, pl.ds(col(1, 1, pair), PAIR_WIDTH)],
                                 qkv_scr[cur, pl.ds(col(1, 1, pair), PAIR_WIDTH)]], axis=0)
            out.append(probabilities(1, pair, qkv_scr[cur, pl.ds(col(1, 0, pair), PAIR_WIDTH)], k, True))
        return out

    def g1_finish(r, parts):
        for pair, (p, m) in enumerate(parts):
            cur = pl.ds(base + r * L, L)
            prev = pl.ds(prev_base + r * L, L)
            v = jnp.concatenate([qkv_scr[prev, pl.ds(col(1, 2, pair), PAIR_WIDTH)],
                                 qkv_scr[cur, pl.ds(col(1, 2, pair), PAIR_WIDTH)]], axis=0)
            acc, den, mm = weighted_values(p, m, v)
            rows = pl.ds(base + r * L, L)
            res[0, pair, rows, :] = acc
            res[1, pair, rows, :] = den
            res[2, pair, rows, :] = mm

    def g0_rows(c):
        q0 = base + c * L
        k0 = jnp.maximum(q0 - L, 0) if c == 0 else q0 - L
        return pl.ds(pl.multiple_of(q0, L), L), pl.ds(pl.multiple_of(k0, L), L)

    def g0_scores(c):
        cur, prev = g0_rows(c)
        out = []
        for pair in range(N_PAIRS):
            k = jnp.concatenate([qkv_scr[prev, pl.ds(col(0, 1, pair), PAIR_WIDTH)],
                                 qkv_scr[cur, pl.ds(col(0, 1, pair), PAIR_WIDTH)]], axis=0)
            out.append(probabilities(0, pair, qkv_scr[cur, pl.ds(col(0, 0, pair), PAIR_WIDTH)], k, c == 0))
        return out

    def g0_finish(c, parts):
        cur, prev = g0_rows(c)
        for pair, (p, m) in enumerate(parts):
            v = jnp.concatenate([qkv_scr[prev, pl.ds(col(0, 2, pair), PAIR_WIDTH)],
                                 qkv_scr[cur, pl.ds(col(0, 2, pair), PAIR_WIDTH)]], axis=0)
            slab = c * N_PAIRS + pair
            for k, val in enumerate(weighted_values(p, m, v)):
                tmp_scr[slab, k] = val
            n_r = L // RES_STRIDE
            for r in range(RES_STRIDE):
                rows = pl.ds(pl.multiple_of(base + r * (tm // RES_STRIDE) + c * n_r, n_r), n_r)
                acc, den, mm = merged(pair, rows, *[tmp_scr[slab, k, pl.ds(r, n_r, stride=RES_STRIDE), :]
                                                    for k in range(3)])
                res[0, pair, rows, :] = acc
                res[1, pair, rows, :] = den
                res[2, pair, rows, :] = mm

    items = ([(g1_scores, g1_finish, (j,)) for j in range(tm // L)]
             + [(g0_scores, g0_finish, (j,)) for j in range(tm // L)])

    def start(item):
        scores_fn, finish_fn, args = item
        return finish_fn, [(a, scores_fn(a)) for a in args]

    def finish(pending):
        finish_fn, parts = pending
        for a, part in parts:
            finish_fn(a, part)

    in_flight = []
    pending_items = list(items)

    def rotate():
        in_flight.append(start(pending_items.pop(0)))
        if len(in_flight) > 2:
            finish(in_flight.pop(0))

    for c0 in range(0, CONV_WIDTH, CONV_CHUNK):
        cs = slice(c0, c0 + CONV_CHUNK)
        u = proj(OFF_GC + c0, CONV_CHUNK) * proj(OFF_UX + c0, CONV_CHUNK)
        u_scr[HALO:HALO + tm, cs] = u
        conv = cw_ref[0:1, cs] * u_scr[HALO - 2:HALO - 2 + tm, cs]
        conv = conv + cw_ref[1:2, cs] * u_scr[HALO - 1:HALO - 1 + tm, cs]
        conv = conv + cw_ref[2:3, cs] * u
        rotate()
        t = proj(OFF_GB + c0, CONV_CHUNK) * conv * _silu(proj(OFF_ZC + c0, CONV_CHUNK))
        t_scr[:, cs] = t.astype(BF16)
        rotate()
    finish(in_flight.pop(0))
    za_ref[...] = _silu(proj(OFF_Z_ATTN, GROUP_WIDTH)).astype(BF16)
    yc_ref[...] = jnp.dot(t_scr[...], wpc_ref[...], preferred_element_type=F32).astype(BF16)
    finish(in_flight.pop(0))
    project_qkv(2)

    g2_d = DILATED_GROUPS[2][1]
    n_t2 = tm // g2_d

    def g2_piece(which, r, pair):
        pieces = [qkv_scr[pl.ds(pl.multiple_of(t * tm + r * n_t2, n_t2), n_t2),
                          pl.ds(col(2, which, pair), PAIR_WIDTH)] for t in range(S // tm)]
        return jnp.concatenate(pieces, axis=0)

    @pl.when(tile == S // tm - 1)
    def _():
        def body(step, carry):
            work = [(u, pair) for u in range(RES_STRIDE) for pair in range(N_PAIRS)]
            parts = [probabilities(2, pair, g2_piece(0, step * RES_STRIDE + u, pair),
                                   g2_piece(1, step * RES_STRIDE + u, pair), False) for u, pair in work]
            for (u, pair), (p, m) in zip(work, parts):
                r = step * RES_STRIDE + u
                new = weighted_values(p, m, g2_piece(2, r, pair))
                outs = []
                for t in range(S // tm):
                    start_row = t * tm + u * (tm // RES_STRIDE) + step
                    rows = pl.ds(start_row, n_t2, stride=RES_STRIDE)
                    acc, den, _ = merged(pair, rows, *[x[t * n_t2:(t + 1) * n_t2] for x in new])
                    outs.append(acc / den)
                o_ref[pair, pl.ds(r, L, stride=g2_d), :] = jnp.concatenate(outs, axis=0)
            return carry
        lax.fori_loop(0, g2_d // RES_STRIDE, body, 0)


def _proj_attn(x, ada, w_in, b_in, conv_w, w_pc):
    bsz, s, d = x.shape
    tm = ROW_TILE
    bias = jnp.asarray(_attn_bias_table())
    const = dict(pipeline_mode=pl.Buffered(1))
    row = lambda width: pl.BlockSpec((None, tm, width), lambda b, i: (b, i, 0))
    return pl.pallas_call(
        _proj_attn_kernel,
        grid=(bsz, s // tm),
        in_specs=[row(d),
                  pl.BlockSpec(ada.shape, lambda b, i: (0, 0), **const),
                  pl.BlockSpec(memory_space=pl.ANY),
                  pl.BlockSpec((1, D_IN), lambda b, i: (0, 0), **const),
                  pl.BlockSpec((CONV_K, CONV_WIDTH), lambda b, i: (0, 0), **const),
                  pl.BlockSpec((CONV_WIDTH, d), lambda b, i: (0, 0), **const),
                  pl.BlockSpec(bias.shape, lambda b, i: (0, 0, 0, 0), **const)],
        out_specs=[row(GROUP_WIDTH), row(d),
                   pl.BlockSpec((None, N_PAIRS, s, PAIR_WIDTH), lambda b, i: (b, 0, 0, 0))],
        out_shape=[jax.ShapeDtypeStruct((bsz, s, GROUP_WIDTH), BF16),
                   jax.ShapeDtypeStruct((bsz, s, d), BF16),
                   jax.ShapeDtypeStruct((bsz, N_PAIRS, s, PAIR_WIDTH), F32)],
        scratch_shapes=[pltpu.VMEM((tm + HALO, CONV_WIDTH), F32),
                        pltpu.VMEM((tm, CONV_WIDTH), BF16),
                        pltpu.VMEM((N_STAGE_SLABS, tm, LANES), F32),
                        pltpu.VMEM((s, QKV_WIDTH), BF16),
                        pltpu.VMEM((3, N_PAIRS, s, PAIR_WIDTH), F32),
                        pltpu.VMEM((N_PAIRS * (tm // SUB_BLOCK), 3, SUB_BLOCK, PAIR_WIDTH), F32),
                        pltpu.VMEM((4, PAIR_WIDTH, 2 * SUB_BLOCK), BF16),
                        pltpu.VMEM((d, OFF_GA), BF16),
                        pltpu.VMEM((W_SLOTS, W_CHUNK, OFF_GA), F32),
                        pltpu.SemaphoreType.DMA((W_SLOTS,))],
        compiler_params=pltpu.CompilerParams(
            dimension_semantics=("arbitrary", "arbitrary"), vmem_limit_bytes=PROJ_VMEM_LIMIT),
        name="proj_attn",
    )(x, ada, w_in, b_in, conv_w, w_pc, bias)


def _out_kernel(x_ref, ada_ref, oa_ref, za_ref, yc_ref, *rest, alpha):
    n_wg = 2 * D_MODEL // OUT_WG_BLOCK
    wg_f32 = rest[:n_wg]
    (b_ref, wpa_f32, wout_f32, bout_ref, lng_ref, lnb_ref,
     out_ref, m_scr, wg_ref, wpa_ref, wout_ref) = rest[n_wg:]

    @pl.when(jnp.logical_and(pl.program_id(0) == 0, pl.program_id(1) == 0))
    def _():
        for k, blk in enumerate(wg_f32):
            wg_ref[:, k * OUT_WG_BLOCK:(k + 1) * OUT_WG_BLOCK] = blk[...].astype(BF16)
        wpa_ref[...] = wpa_f32[...].astype(BF16)
        wout_ref[...] = wout_f32[...].astype(BF16)

    ada = ada_ref[pl.ds(pl.program_id(0), 1), :]
    shift = ada[:, 0:D_MODEL]
    scale = ada[:, D_MODEL:2 * D_MODEL]
    gate = ada[:, 2 * D_MODEL:] * (1.0 / alpha)
    eps = LN_EPS / (alpha * alpha)
    h = (x_ref[...] * (1.0 + scale) + shift).astype(BF16)
    o_attn = jnp.concatenate([oa_ref[pair] for pair in range(N_PAIRS)], axis=1)
    ta = (o_attn * za_ref[...].astype(F32)).astype(BF16)

    for c0 in range(0, D_MODEL, OUT_COL_CHUNK):
        cs = slice(c0, c0 + OUT_COL_CHUNK)
        gs = slice(D_MODEL + c0, D_MODEL + c0 + OUT_COL_CHUNK)
        g_a = _sigmoid(jnp.dot(h, wg_ref[:, cs], preferred_element_type=F32)
                       + b_ref[:, OFF_GA + c0:OFF_GA + c0 + OUT_COL_CHUNK])
        g_b = _sigmoid(jnp.dot(h, wg_ref[:, gs], preferred_element_type=F32)
                       + b_ref[:, OFF_GA + D_MODEL + c0:OFF_GA + D_MODEL + c0 + OUT_COL_CHUNK])
        y_attn = jnp.dot(ta, wpa_ref[:, cs], preferred_element_type=F32)
        m_scr[:, cs] = g_a.astype(BF16) * y_attn.astype(BF16) + g_b.astype(BF16) * yc_ref[:, cs]

    for r0 in range(0, x_ref.shape[0], OUT_CHUNK):
        rows = slice(r0, r0 + OUT_CHUNK)
        sub = gate * (jnp.dot(m_scr[rows, :], wout_ref[...], preferred_element_type=F32) + bout_ref[...])
        r = x_ref[rows, :] + sub
        mu = jnp.mean(r, axis=-1, keepdims=True)
        cen = r - mu
        var = jnp.mean(cen * cen, axis=-1, keepdims=True)
        out_ref[rows, :] = cen * lax.rsqrt(var + eps) * lng_ref[...] + lnb_ref[...]


def _out(x, ada, o_attn, za, yc, w_in, b_in, w_pa, w_out, b_out, ln_g, ln_b, alpha):
    bsz, s, d = x.shape
    tm = OUT_ROW_TILE
    const = dict(pipeline_mode=pl.Buffered(1))
    row = lambda width: pl.BlockSpec((None, tm, width), lambda b, i: (b, i, 0))
    vec = pl.BlockSpec((1, d), lambda b, i: (0, 0), **const)
    n_wg = 2 * d // OUT_WG_BLOCK
    wg_specs = [pl.BlockSpec((d, OUT_WG_BLOCK), lambda b, i, k=k: (0, OFF_GA // OUT_WG_BLOCK + k), **const)
                for k in range(n_wg)]
    return pl.pallas_call(
        functools.partial(_out_kernel, alpha=alpha),
        grid=(bsz, s // tm),
        in_specs=[row(d), pl.BlockSpec(ada.shape, lambda b, i: (0, 0), **const),
                  pl.BlockSpec((None, N_PAIRS, tm, PAIR_WIDTH), lambda b, i: (b, 0, i, 0)),
                  row(GROUP_WIDTH), row(d)]
                 + wg_specs
                 + [pl.BlockSpec((1, D_IN), lambda b, i: (0, 0), **const),
                    pl.BlockSpec((GROUP_WIDTH, d), lambda b, i: (0, 0), **const),
                    pl.BlockSpec((d, d), lambda b, i: (0, 0), **const),
                    vec, vec, vec],
        out_specs=row(d),
        out_shape=jax.ShapeDtypeStruct((bsz, s, d), F32),
        scratch_shapes=[pltpu.VMEM((tm, d), BF16),
                        pltpu.VMEM((d, 2 * d), BF16),
                        pltpu.VMEM((GROUP_WIDTH, d), BF16),
                        pltpu.VMEM((d, d), BF16)],
        compiler_params=pltpu.CompilerParams(
            dimension_semantics=("arbitrary", "arbitrary"), vmem_limit_bytes=VMEM_LIMIT),
        name="out",
    )(x, ada, o_attn, za, yc, *([w_in] * n_wg), b_in, w_pa, w_out, b_out, ln_g, ln_b)


def kernel(x, c, w_ada, b_ada, w_in, b_in, conv_w, w_proj_attn, w_proj_conv, w_out, b_out, ln_g, ln_b):
    bsz, s, d = x.shape
    depth = w_in.shape[0]
    assert d == D_MODEL and s % OUT_ROW_TILE == 0 and w_in.shape[1:] == (D_MODEL, D_IN)
    assert s // DILATED_GROUPS[-1][1] == SUB_BLOCK
    alpha = (2.0 * depth) ** 0.25
    for layer in range(depth):
        ada = _ada(c, w_ada[layer], b_ada[layer])
        b_in2 = b_in[layer].reshape(1, D_IN)
        za, yc, o_attn = _proj_attn(x, ada, w_in[layer], b_in2,
                                    conv_w[layer], w_proj_conv[layer].astype(BF16))
        x = _out(x, ada, o_attn, za, yc, w_in[layer], b_in2, w_proj_attn[layer], w_out[layer],
                 b_out[layer].reshape(1, d), ln_g[layer].reshape(1, d), ln_b[layer].reshape(1, d), alpha)
    return x
```

```python
import functools
import itertools

import numpy as np
import jax
import jax.numpy as jnp
from jax import lax
from jax.experimental import pallas as pl
from jax.experimental.pallas import tpu as pltpu

D_MODEL = 1024
HEAD_DIM = 64
HEADS_PER_GROUP = 4
DILATED_GROUPS = ((128, 1), (512, 4), (2048, 16))
N_GROUPS = len(DILATED_GROUPS)
N_ATTN_HEADS = N_GROUPS * HEADS_PER_GROUP
ATTN_WIDTH = N_ATTN_HEADS * HEAD_DIM
GROUP_WIDTH = HEADS_PER_GROUP * HEAD_DIM
QKV_WIDTH = 3 * ATTN_WIDTH
CONV_WIDTH = D_MODEL
CONV_K = 3
SUB_BLOCK = 128
ALIBI_MAX_EXP = 8.0
LN_EPS = 1e-5

OFF_Z_ATTN = QKV_WIDTH
OFF_UX = OFF_Z_ATTN + GROUP_WIDTH
OFF_GB = OFF_UX + CONV_WIDTH
OFF_GC = OFF_GB + CONV_WIDTH
OFF_ZC = OFF_GC + CONV_WIDTH
OFF_GA = OFF_ZC + CONV_WIDTH
D_IN = OFF_GA + 2 * D_MODEL

ROW_TILE = 512
OUT_ROW_TILE = 1024
OUT_CHUNK = 256
OUT_COL_CHUNK = 256
OUT_WG_BLOCK = 512
W_CHUNK = 32
W_SLOTS = 4
CONV_CHUNK = 256
LANES = 128
PAIR_WIDTH = 2 * HEAD_DIM
N_PAIRS = HEADS_PER_GROUP // 2
REGROUP_STRIDE = 4
RES_STRIDE = REGROUP_STRIDE
N_STAGE_SLABS = 6
HALO = 8
VMEM_LIMIT = 56 * 1024 * 1024
PROJ_VMEM_LIMIT = 60 * 1024 * 1024

F32 = jnp.float32
BF16 = jnp.bfloat16

CONV_COLUMN_MOVES = [(off + c0, OFF_UX + 4 * c0 + j * CONV_CHUNK)
                     for c0 in range(0, CONV_WIDTH, CONV_CHUNK)
                     for j, off in enumerate((OFF_GC, OFF_UX, OFF_GB, OFF_ZC))]

assert PAIR_WIDTH == LANES and ROW_TILE // DILATED_GROUPS[1][1] == SUB_BLOCK
assert DILATED_GROUPS[1][1] == RES_STRIDE and DILATED_GROUPS[2][1] == RES_STRIDE * RES_STRIDE
assert 2 * (CONV_WIDTH // CONV_CHUNK) == 2 * (ROW_TILE // SUB_BLOCK)
assert OFF_GA % OUT_WG_BLOCK == 0 and D_MODEL % W_CHUNK == 0 and D_MODEL // W_CHUNK >= W_SLOTS


def _sigmoid(v):
    return 0.5 * jnp.tanh(0.5 * v) + 0.5


def _silu(v):
    return v * _sigmoid(v)


def _ada_kernel(c_ref, w_ref, b_ref, o_ref):
    o_ref[...] = jnp.dot(_silu(c_ref[...]).astype(BF16), w_ref[...].astype(BF16),
                         preferred_element_type=F32) + b_ref[...]


def _ada(c, w_ada, b_ada):
    bsz, d = c.shape
    n_out = w_ada.shape[1]
    return pl.pallas_call(
        _ada_kernel,
        grid=(n_out // d,),
        in_specs=[pl.BlockSpec((bsz, d), lambda j: (0, 0)),
                  pl.BlockSpec((d, d), lambda j: (0, j)),
                  pl.BlockSpec((1, d), lambda j: (0, j))],
        out_specs=pl.BlockSpec((bsz, d), lambda j: (0, j)),
        out_shape=jax.ShapeDtypeStruct((bsz, n_out), F32),
        name="ada",
    )(c, w_ada, b_ada.reshape(1, n_out))


def _attn_bias_table():
    L = SUB_BLOCK
    tabs = []
    for group, (window, dilation) in enumerate(DILATED_GROUPS):
        span = window // dilation
        heads = np.arange(group * HEADS_PER_GROUP, (group + 1) * HEADS_PER_GROUP, dtype=np.float64)
        slopes = 2.0 ** (-ALIBI_MAX_EXP * (heads + 1.0) / N_ATTN_HEADS)
        delta = (np.arange(L)[:, None] + L - np.arange(2 * L)[None, :]).astype(np.float64)
        valid = (delta >= 0) & (delta <= span)
        bias = np.where(valid[None], -slopes[:, None, None] * (delta * dilation)[None], -np.inf)
        tabs.append(bias.reshape(HEADS_PER_GROUP // 2, 2 * L, 2 * L))
    return np.stack(tabs).astype(np.float32)


def _proj_attn_kernel(x_ref, ada_ref, w_hbm, b_ref, cw_ref, wpc_ref, bias_ref,
                      za_ref, yc_ref, o_ref, u_scr, t_scr, stage_scr, qkv_scr, res, tmp_scr, kt_scr,
                      w_ref, w_stage, w_sem):
    tm = x_ref.shape[0]
    S = qkv_scr.shape[0]
    L = SUB_BLOCK
    tile = pl.program_id(1)
    first_tile = tile == 0
    base = pl.multiple_of(tile * tm, tm)
    prev_base = pl.multiple_of(jnp.maximum(tile - 1, 0) * tm, tm)

    def weight_chunk(k):
        return pltpu.make_async_copy(w_hbm.at[pl.ds(k * W_CHUNK, W_CHUNK), pl.ds(0, OFF_GA)],
                                     w_stage.at[k % W_SLOTS], w_sem.at[k % W_SLOTS])

    @pl.when(jnp.logical_and(pl.program_id(0) == 0, first_tile))
    def _():
        n_chunks = D_MODEL // W_CHUNK
        ahead = W_SLOTS - 1
        for k in range(ahead):
            weight_chunk(k).start()
        for k in range(n_chunks):
            if k + ahead < n_chunks:
                weight_chunk(k + ahead).start()
            weight_chunk(k).wait()
            rows = slice(k * W_CHUNK, (k + 1) * W_CHUNK)
            w_ref[rows, 0:OFF_UX] = w_stage[k % W_SLOTS, :, 0:OFF_UX].astype(BF16)
            for src_off, dst_off in CONV_COLUMN_MOVES:
                w_ref[rows, dst_off:dst_off + CONV_CHUNK] = (
                    w_stage[k % W_SLOTS, :, src_off:src_off + CONV_CHUNK].astype(BF16))

    @pl.when(first_tile)
    def _():
        u_scr[0:HALO, :] = jnp.zeros((HALO, CONV_WIDTH), F32)

    @pl.when(jnp.logical_not(first_tile))
    def _():
        u_scr[0:HALO, :] = u_scr[tm:tm + HALO, :]

    ada = ada_ref[pl.ds(pl.program_id(0), 1), :]
    shift = ada[:, 0:D_MODEL]
    scale = ada[:, D_MODEL:2 * D_MODEL]
    h = (x_ref[...] * (1.0 + scale) + shift).astype(BF16)

    def proj(lo, width):
        return (jnp.dot(h, w_ref[:, lo:lo + width], preferred_element_type=F32)
                + b_ref[:, lo:lo + width])

    slabs = itertools.cycle(range(N_STAGE_SLABS))

    def project_qkv(g):
        d = DILATED_GROUPS[g][1]
        for which in range(3):
            c0 = which * ATTN_WIDTH + g * GROUP_WIDTH
            blk = proj(c0, GROUP_WIDTH)
            if d == 1:
                qkv_scr[pl.ds(base, tm), c0:c0 + GROUP_WIDTH] = blk.astype(BF16)
                continue
            n_t = tm // d
            for lo in range(0, GROUP_WIDTH, LANES):
                cols = slice(c0 + lo, c0 + lo + LANES)
                slab = next(slabs)
                stage_scr[slab] = blk[:, lo:lo + LANES]
                if d == REGROUP_STRIDE:
                    for r in range(d):
                        qkv_scr[pl.ds(base + r * n_t, n_t), cols] = (
                            stage_scr[slab, pl.ds(r, n_t, stride=d), :].astype(BF16))
                else:
                    assert d == REGROUP_STRIDE * REGROUP_STRIDE
                    n_1 = tm // REGROUP_STRIDE
                    slab2 = next(slabs)
                    for r1 in range(REGROUP_STRIDE):
                        stage_scr[slab2, r1 * n_1:(r1 + 1) * n_1, :] = (
                            stage_scr[slab, pl.ds(r1, n_1, stride=REGROUP_STRIDE), :])
                    for r1 in range(REGROUP_STRIDE):
                        for r2 in range(REGROUP_STRIDE):
                            r = r2 * REGROUP_STRIDE + r1
                            qkv_scr[pl.ds(base + r * n_t, n_t), cols] = stage_scr[
                                slab2, pl.ds(r1 * n_1 + r2, n_t, stride=REGROUP_STRIDE), :].astype(BF16)

    project_qkv(1)
    project_qkv(0)

    low_half = lax.broadcasted_iota(jnp.int32, (L, PAIR_WIDTH), 1) < HEAD_DIM
    no_prev = jnp.logical_and(lax.broadcasted_iota(jnp.int32, (2 * L, 2 * L), 1) < L, first_tile)

    def col(g, which, pair):
        return which * ATTN_WIDTH + g * GROUP_WIDTH + pair * PAIR_WIDTH

    kt_slots = itertools.cycle(range(kt_scr.shape[0]))

    def probabilities(g, pair, q, k, maybe_no_prev):
        q = q * 0.125
        zero = jnp.zeros_like(q)
        q2 = jnp.concatenate([jnp.where(low_half, q, zero), jnp.where(low_half, zero, q)], axis=0)
        slot = next(kt_slots)
        n_keys = k.shape[0]
        kt_scr[slot, :, :n_keys] = k.T
        sc = jnp.dot(q2, kt_scr[slot, :, :n_keys], preferred_element_type=F32)
        sc = sc + bias_ref[g, pair, :, 2 * L - k.shape[0]:]
        if maybe_no_prev:
            sc = jnp.where(no_prev, -jnp.inf, sc)
        m = jnp.max(sc, axis=-1, keepdims=True)
        return jnp.exp(sc - m).astype(BF16), m

    def weighted_values(p, m, v):
        v1 = jnp.concatenate([v, jnp.ones(v.shape, BF16)], axis=1)
        pv = jnp.dot(p, v1, preferred_element_type=F32)
        mb = jnp.broadcast_to(m, (2 * L, PAIR_WIDTH))
        return (jnp.where(low_half, pv[:L, :PAIR_WIDTH], pv[L:, :PAIR_WIDTH]),
                jnp.where(low_half, pv[:L, PAIR_WIDTH:], pv[L:, PAIR_WIDTH:]),
                jnp.where(low_half, mb[:L], mb[L:]))

    def merged(pair, rows, acc, den, m):
        m_old = res[2, pair, rows, :]
        m_new = jnp.maximum(m_old, m)
        a = jnp.exp(m_old - m_new)
        b = jnp.exp(m - m_new)
        return a * res[0, pair, rows, :] + b * acc, a * res[1, pair, rows, :] + b * den, m_new

    g1_d = DILATED_GROUPS[1][1]

    def g1_scores(r):
        out = []
        for pair in range(N_PAIRS):
            cur = pl.ds(base + r * L, L)
            prev = pl.ds(prev_base + r * L, L)
            k = jnp.concatenate([qkv_scr[prev, pl.ds(col(1, 1, pair), PAIR_WIDTH)],
                                 qkv_scr[cur, pl.ds(col(1, 1, pair), PAIR_WIDTH)]], axis=0)
            out.append(probabilities(1, pair, qkv_scr[cur, pl.ds(col(1, 0, pair), PAIR_WIDTH)], k, True))
        return out

    def g1_finish(r, parts):
        for pair, (p, m) in enumerate(parts):
            cur = pl.ds(base + r * L, L)
            prev = pl.ds(prev_base + r * L, L)
            v = jnp.concatenate([qkv_scr[prev, pl.ds(col(1, 2, pair), PAIR_WIDTH)],
                                 qkv_scr[cur, pl.ds(col(1, 2, pair), PAIR_WIDTH)]], axis=0)
            acc, den, mm = weighted_values(p, m, v)
            rows = pl.ds(base + r * L, L)
            res[0, pair, rows, :] = acc
            res[1, pair, rows, :] = den
            res[2, pair, rows, :] = mm

    def g0_rows(c):
        q0 = base + c * L
        k0 = jnp.maximum(q0 - L, 0) if c == 0 else q0 - L
        return pl.ds(pl.multiple_of(q0, L), L), pl.ds(pl.multiple_of(k0, L), L)

    def g0_scores(c):
        cur, prev = g0_rows(c)
        out = []
        for pair in range(N_PAIRS):
            k = jnp.concatenate([qkv_scr[prev, pl.ds(col(0, 1, pair), PAIR_WIDTH)],
                                 qkv_scr[cur, pl.ds(col(0, 1, pair), PAIR_WIDTH)]], axis=0)
            out.append(probabilities(0, pair, qkv_scr[cur, pl.ds(col(0, 0, pair), PAIR_WIDTH)], k, c == 0))
        return out

    def g0_finish(c, parts):
        cur, prev = g0_rows(c)
        for pair, (p, m) in enumerate(parts):
            v = jnp.concatenate([qkv_scr[prev, pl.ds(col(0, 2, pair), PAIR_WIDTH)],
                                 qkv_scr[cur, pl.ds(col(0, 2, pair), PAIR_WIDTH)]], axis=0)
            slab = c * N_PAIRS + pair
            for k, val in enumerate(weighted_values(p, m, v)):
                tmp_scr[slab, k] = val
            n_r = L // RES_STRIDE
            for r in range(RES_STRIDE):
                rows = pl.ds(pl.multiple_of(base + r * (tm // RES_STRIDE) + c * n_r, n_r), n_r)
                acc, den, mm = merged(pair, rows, *[tmp_scr[slab, k, pl.ds(r, n_r, stride=RES_STRIDE), :]
                                                    for k in range(3)])
                res[0, pair, rows, :] = acc
                res[1, pair, rows, :] = den
                res[2, pair, rows, :] = mm

    items = ([(g1_scores, g1_finish, (j,)) for j in range(tm // L)]
             + [(g0_scores, g0_finish, (j,)) for j in range(tm // L)])

    def start(item):
        scores_fn, finish_fn, args = item
        return finish_fn, [(a, scores_fn(a)) for a in args]

    def finish(pending):
        finish_fn, parts = pending
        for a, part in parts:
            finish_fn(a, part)

    in_flight = []
    pending_items = list(items)

    def rotate():
        in_flight.append(start(pending_items.pop(0)))
        if len(in_flight) > 2:
            finish(in_flight.pop(0))

    for c0 in range(0, CONV_WIDTH, CONV_CHUNK):
        cs = slice(c0, c0 + CONV_CHUNK)
        lo = OFF_UX + 4 * c0
        gu = (jnp.dot(h, w_ref[:, lo:lo + 2 * CONV_CHUNK], preferred_element_type=F32)
              + jnp.concatenate([b_ref[:, OFF_GC + c0:OFF_GC + c0 + CONV_CHUNK],
                                 b_ref[:, OFF_UX + c0:OFF_UX + c0 + CONV_CHUNK]], axis=1))
        u = gu[:, :CONV_CHUNK] * gu[:, CONV_CHUNK:]
        u_scr[HALO:HALO + tm, cs] = u
        conv = cw_ref[0:1, cs] * u_scr[HALO - 2:HALO - 2 + tm, cs]
        conv = conv + cw_ref[1:2, cs] * u_scr[HALO - 1:HALO - 1 + tm, cs]
        conv = conv + cw_ref[2:3, cs] * u
        rotate()
        gz = (jnp.dot(h, w_ref[:, lo + 2 * CONV_CHUNK:lo + 4 * CONV_CHUNK], preferred_element_type=F32)
              + jnp.concatenate([b_ref[:, OFF_GB + c0:OFF_GB + c0 + CONV_CHUNK],
                                 b_ref[:, OFF_ZC + c0:OFF_ZC + c0 + CONV_CHUNK]], axis=1))
        t = gz[:, :CONV_CHUNK] * conv * _silu(gz[:, CONV_CHUNK:])
        t_scr[:, cs] = t.astype(BF16)
        rotate()
    finish(in_flight.pop(0))
    za_ref[...] = _silu(proj(OFF_Z_ATTN, GROUP_WIDTH)).astype(BF16)
    yc_ref[...] = jnp.dot(t_scr[...], wpc_ref[...], preferred_element_type=F32).astype(BF16)
    finish(in_flight.pop(0))
    project_qkv(2)

    g2_d = DILATED_GROUPS[2][1]
    n_t2 = tm // g2_d

    def g2_piece(which, r, pair):
        pieces = [qkv_scr[pl.ds(pl.multiple_of(t * tm + r * n_t2, n_t2), n_t2),
                          pl.ds(col(2, which, pair), PAIR_WIDTH)] for t in range(S // tm)]
        return jnp.concatenate(pieces, axis=0)

    @pl.when(tile == S // tm - 1)
    def _():
        def body(step, carry):
            work = [(u, pair) for u in range(RES_STRIDE) for pair in range(N_PAIRS)]
            parts = [probabilities(2, pair, g2_piece(0, step * RES_STRIDE + u, pair),
                                   g2_piece(1, step * RES_STRIDE + u, pair), False) for u, pair in work]
            for (u, pair), (p, m) in zip(work, parts):
                r = step * RES_STRIDE + u
                new = weighted_values(p, m, g2_piece(2, r, pair))
                outs = []
                for t in range(S // tm):
                    start_row = t * tm + u * (tm // RES_STRIDE) + step
                    rows = pl.ds(start_row, n_t2, stride=RES_STRIDE)
                    acc, den, _ = merged(pair, rows, *[x[t * n_t2:(t + 1) * n_t2] for x in new])
                    outs.append(acc / den)
                o_ref[pair, pl.ds(r, L, stride=g2_d), :] = jnp.concatenate(outs, axis=0)
            return carry
        lax.fori_loop(0, g2_d // RES_STRIDE, body, 0)


def _proj_attn(x, ada, w_in, b_in, conv_w, w_pc):
    bsz, s, d = x.shape
    tm = ROW_TILE
    bias = jnp.asarray(_attn_bias_table())
    const = dict(pipeline_mode=pl.Buffered(1))
    row = lambda width: pl.BlockSpec((None, tm, width), lambda b, i: (b, i, 0))
    return pl.pallas_call(
        _proj_attn_kernel,
        grid=(bsz, s // tm),
        in_specs=[row(d),
                  pl.BlockSpec(ada.shape, lambda b, i: (0, 0), **const),
                  pl.BlockSpec(memory_space=pl.ANY),
                  pl.BlockSpec((1, D_IN), lambda b, i: (0, 0), **const),
                  pl.BlockSpec((CONV_K, CONV_WIDTH), lambda b, i: (0, 0), **const),
                  pl.BlockSpec((CONV_WIDTH, d), lambda b, i: (0, 0), **const),
                  pl.BlockSpec(bias.shape, lambda b, i: (0, 0, 0, 0), **const)],
        out_specs=[row(GROUP_WIDTH), row(d),
                   pl.BlockSpec((None, N_PAIRS, s, PAIR_WIDTH), lambda b, i: (b, 0, 0, 0))],
        out_shape=[jax.ShapeDtypeStruct((bsz, s, GROUP_WIDTH), BF16),
                   jax.ShapeDtypeStruct((bsz, s, d), BF16),
                   jax.ShapeDtypeStruct((bsz, N_PAIRS, s, PAIR_WIDTH), F32)],
        scratch_shapes=[pltpu.VMEM((tm + HALO, CONV_WIDTH), F32),
                        pltpu.VMEM((tm, CONV_WIDTH), BF16),
                        pltpu.VMEM((N_STAGE_SLABS, tm, LANES), F32),
                        pltpu.VMEM((s, QKV_WIDTH), BF16),
                        pltpu.VMEM((3, N_PAIRS, s, PAIR_WIDTH), F32),
                        pltpu.VMEM((N_PAIRS * (tm // SUB_BLOCK), 3, SUB_BLOCK, PAIR_WIDTH), F32),
                        pltpu.VMEM((4, PAIR_WIDTH, 2 * SUB_BLOCK), BF16),
                        pltpu.VMEM((d, OFF_GA), BF16),
                        pltpu.VMEM((W_SLOTS, W_CHUNK, OFF_GA), F32),
                        pltpu.SemaphoreType.DMA((W_SLOTS,))],
        compiler_params=pltpu.CompilerParams(
            dimension_semantics=("arbitrary", "arbitrary"), vmem_limit_bytes=PROJ_VMEM_LIMIT),
        name="proj_attn",
    )(x, ada, w_in, b_in, conv_w, w_pc, bias)


def _out_kernel(x_ref, ada_ref, oa_ref, za_ref, yc_ref, *rest, alpha):
    n_wg = 2 * D_MODEL // OUT_WG_BLOCK
    wg_f32 = rest[:n_wg]
    (b_ref, wpa_f32, wout_f32, bout_ref, lng_ref, lnb_ref,
     out_ref, m_scr, wg_ref, wpa_ref, wout_ref) = rest[n_wg:]

    @pl.when(jnp.logical_and(pl.program_id(0) == 0, pl.program_id(1) == 0))
    def _():
        for k, blk in enumerate(wg_f32):
            wg_ref[:, k * OUT_WG_BLOCK:(k + 1) * OUT_WG_BLOCK] = blk[...].astype(BF16)
        wpa_ref[...] = wpa_f32[...].astype(BF16)
        wout_ref[...] = wout_f32[...].astype(BF16)

    ada = ada_ref[pl.ds(pl.program_id(0), 1), :]
    shift = ada[:, 0:D_MODEL]
    scale = ada[:, D_MODEL:2 * D_MODEL]
    gate = ada[:, 2 * D_MODEL:] * (1.0 / alpha)
    eps = LN_EPS / (alpha * alpha)
    h = (x_ref[...] * (1.0 + scale) + shift).astype(BF16)
    o_attn = jnp.concatenate([oa_ref[pair] for pair in range(N_PAIRS)], axis=1)
    ta = (o_attn * za_ref[...].astype(F32)).astype(BF16)

    for c0 in range(0, D_MODEL, OUT_COL_CHUNK):
        cs = slice(c0, c0 + OUT_COL_CHUNK)
        gs = slice(D_MODEL + c0, D_MODEL + c0 + OUT_COL_CHUNK)
        g_a = _sigmoid(jnp.dot(h, wg_ref[:, cs], preferred_element_type=F32)
                       + b_ref[:, OFF_GA + c0:OFF_GA + c0 + OUT_COL_CHUNK])
        g_b = _sigmoid(jnp.dot(h, wg_ref[:, gs], preferred_element_type=F32)
                       + b_ref[:, OFF_GA + D_MODEL + c0:OFF_GA + D_MODEL + c0 + OUT_COL_CHUNK])
        y_attn = jnp.dot(ta, wpa_ref[:, cs], preferred_element_type=F32)
        m_scr[:, cs] = g_a.astype(BF16) * y_attn.astype(BF16) + g_b.astype(BF16) * yc_ref[:, cs]

    for r0 in range(0, x_ref.shape[0], OUT_CHUNK):
        rows = slice(r0, r0 + OUT_CHUNK)
        sub = gate * (jnp.dot(m_scr[rows, :], wout_ref[...], preferred_element_type=F32) + bout_ref[...])
        r = x_ref[rows, :] + sub
        mu = jnp.mean(r, axis=-1, keepdims=True)
        cen = r - mu
        var = jnp.mean(cen * cen, axis=-1, keepdims=True)
        out_ref[rows, :] = cen * lax.rsqrt(var + eps) * lng_ref[...] + lnb_ref[...]


def _out(x, ada, o_attn, za, yc, w_in, b_in, w_pa, w_out, b_out, ln_g, ln_b, alpha):
    bsz, s, d = x.shape
    tm = OUT_ROW_TILE
    const = dict(pipeline_mode=pl.Buffered(1))
    row = lambda width: pl.BlockSpec((None, tm, width), lambda b, i: (b, i, 0))
    vec = pl.BlockSpec((1, d), lambda b, i: (0, 0), **const)
    n_wg = 2 * d // OUT_WG_BLOCK
    wg_specs = [pl.BlockSpec((d, OUT_WG_BLOCK), lambda b, i, k=k: (0, OFF_GA // OUT_WG_BLOCK + k), **const)
                for k in range(n_wg)]
    return pl.pallas_call(
        functools.partial(_out_kernel, alpha=alpha),
        grid=(bsz, s // tm),
        in_specs=[row(d), pl.BlockSpec(ada.shape, lambda b, i: (0, 0), **const),
                  pl.BlockSpec((None, N_PAIRS, tm, PAIR_WIDTH), lambda b, i: (b, 0, i, 0)),
                  row(GROUP_WIDTH), row(d)]
                 + wg_specs
                 + [pl.BlockSpec((1, D_IN), lambda b, i: (0, 0), **const),
                    pl.BlockSpec((GROUP_WIDTH, d), lambda b, i: (0, 0), **const),
                    pl.BlockSpec((d, d), lambda b, i: (0, 0), **const),
                    vec, vec, vec],
        out_specs=row(d),
        out_shape=jax.ShapeDtypeStruct((bsz, s, d), F32),
        scratch_shapes=[pltpu.VMEM((tm, d), BF16),
                        pltpu.VMEM((d, 2 * d), BF16),
                        pltpu.VMEM((GROUP_WIDTH, d), BF16),
                        pltpu.VMEM((d, d), BF16)],
        compiler_params=pltpu.CompilerParams(
            dimension_semantics=("arbitrary", "arbitrary"), vmem_limit_bytes=VMEM_LIMIT),
        name="out",
    )(x, ada, o_attn, za, yc, *([w_in] * n_wg), b_in, w_pa, w_out, b_out, ln_g, ln_b)


def kernel(x, c, w_ada, b_ada, w_in, b_in, conv_w, w_proj_attn, w_proj_conv, w_out, b_out, ln_g, ln_b):
    bsz, s, d = x.shape
    depth = w_in.shape[0]
    assert d == D_MODEL and s % OUT_ROW_TILE == 0 and w_in.shape[1:] == (D_MODEL, D_IN)
    assert s // DILATED_GROUPS[-1][1] == SUB_BLOCK
    alpha = (2.0 * depth) ** 0.25
    for layer in range(depth):
        ada = _ada(c, w_ada[layer], b_ada[layer])
        b_in2 = b_in[layer].reshape(1, D_IN)
        za, yc, o_attn = _proj_attn(x, ada, w_in[layer], b_in2,
                                    conv_w[layer], w_proj_conv[layer].astype(BF16))
        x = _out(x, ada, o_attn, za, yc, w_in[layer], b_in2, w_proj_attn[layer], w_out[layer],
                 b_out[layer].reshape(1, d), ln_g[layer].reshape(1, d), ln_b[layer].reshape(1, d), alpha)
    return x
```

```python
import functools
import itertools

import numpy as np
import jax
import jax.numpy as jnp
from jax import lax
from jax.experimental import pallas as pl
from jax.experimental.pallas import tpu as pltpu

D_MODEL = 1024
HEAD_DIM = 64
HEADS_PER_GROUP = 4
DILATED_GROUPS = ((128, 1), (512, 4), (2048, 16))
N_GROUPS = len(DILATED_GROUPS)
N_ATTN_HEADS = N_GROUPS * HEADS_PER_GROUP
ATTN_WIDTH = N_ATTN_HEADS * HEAD_DIM
GROUP_WIDTH = HEADS_PER_GROUP * HEAD_DIM
QKV_WIDTH = 3 * ATTN_WIDTH
CONV_WIDTH = D_MODEL
CONV_K = 3
SUB_BLOCK = 128
ALIBI_MAX_EXP = 8.0
LN_EPS = 1e-5

OFF_Z_ATTN = QKV_WIDTH
OFF_UX = OFF_Z_ATTN + GROUP_WIDTH
OFF_GB = OFF_UX + CONV_WIDTH
OFF_GC = OFF_GB + CONV_WIDTH
OFF_ZC = OFF_GC + CONV_WIDTH
OFF_GA = OFF_ZC + CONV_WIDTH
D_IN = OFF_GA + 2 * D_MODEL

ROW_TILE = 512
OUT_ROW_TILE = 1024
OUT_CHUNK = 256
OUT_COL_CHUNK = 256
OUT_WG_BLOCK = 512
W_CHUNK = 32
W_SLOTS = 4
CONV_CHUNK = 256
LANES = 128
PAIR_WIDTH = 2 * HEAD_DIM
N_PAIRS = HEADS_PER_GROUP // 2
REGROUP_STRIDE = 4
RES_STRIDE = REGROUP_STRIDE
N_STAGE_SLABS = 6
HALO = 8
VMEM_LIMIT = 56 * 1024 * 1024
PROJ_VMEM_LIMIT = 60 * 1024 * 1024

F32 = jnp.float32
BF16 = jnp.bfloat16

CONV_COLUMN_MOVES = [(off + c0, OFF_UX + 4 * c0 + j * CONV_CHUNK)
                     for c0 in range(0, CONV_WIDTH, CONV_CHUNK)
                     for j, off in enumerate((OFF_GC, OFF_UX, OFF_GB, OFF_ZC))]

assert PAIR_WIDTH == LANES and ROW_TILE // DILATED_GROUPS[1][1] == SUB_BLOCK
assert DILATED_GROUPS[1][1] == RES_STRIDE and DILATED_GROUPS[2][1] == RES_STRIDE * RES_STRIDE
assert 2 * (CONV_WIDTH // CONV_CHUNK) == 2 * (ROW_TILE // SUB_BLOCK)
assert OFF_GA % OUT_WG_BLOCK == 0 and D_MODEL % W_CHUNK == 0 and D_MODEL // W_CHUNK >= W_SLOTS


def _sigmoid(v):
    return 0.5 * jnp.tanh(0.5 * v) + 0.5


def _silu(v):
    return v * _sigmoid(v)


def _ada_kernel(c_ref, w_ref, b_ref, o_ref):
    o_ref[...] = jnp.dot(_silu(c_ref[...]).astype(BF16), w_ref[...].astype(BF16),
                         preferred_element_type=F32) + b_ref[...]


def _ada(c, w_ada, b_ada):
    bsz, d = c.shape
    n_out = w_ada.shape[1]
    return pl.pallas_call(
        _ada_kernel,
        grid=(n_out // d,),
        in_specs=[pl.BlockSpec((bsz, d), lambda j: (0, 0)),
                  pl.BlockSpec((d, d), lambda j: (0, j)),
                  pl.BlockSpec((1, d), lambda j: (0, j))],
        out_specs=pl.BlockSpec((bsz, d), lambda j: (0, j)),
        out_shape=jax.ShapeDtypeStruct((bsz, n_out), F32),
        name="ada",
    )(c, w_ada, b_ada.reshape(1, n_out))


def _attn_bias_table():
    L = SUB_BLOCK
    tabs = []
    for group, (window, dilation) in enumerate(DILATED_GROUPS):
        span = window // dilation
        heads = np.arange(group * HEADS_PER_GROUP, (group + 1) * HEADS_PER_GROUP, dtype=np.float64)
        slopes = 2.0 ** (-ALIBI_MAX_EXP * (heads + 1.0) / N_ATTN_HEADS)
        delta = (np.arange(L)[:, None] + L - np.arange(2 * L)[None, :]).astype(np.float64)
        valid = (delta >= 0) & (delta <= span)
        bias = np.where(valid[None], -slopes[:, None, None] * (delta * dilation)[None], -np.inf)
        tabs.append(bias.reshape(HEADS_PER_GROUP // 2, 2 * L, 2 * L))
    return np.stack(tabs).astype(np.float32)


def _proj_attn_kernel(x_ref, ada_ref, w_hbm, b_ref, cw_ref, wpc_ref, bias_ref,
                      za_ref, yc_ref, o_ref, u_scr, t_scr, stage_scr, qkv_scr, res, tmp_scr, kt_scr,
                      w_ref, w_stage, w_sem):
    tm = x_ref.shape[0]
    S = qkv_scr.shape[0]
    L = SUB_BLOCK
    tile = pl.program_id(1)
    first_tile = tile == 0
    base = pl.multiple_of(tile * tm, tm)
    prev_base = pl.multiple_of(jnp.maximum(tile - 1, 0) * tm, tm)

    def weight_chunk(k):
        return pltpu.make_async_copy(w_hbm.at[pl.ds(k * W_CHUNK, W_CHUNK), pl.ds(0, OFF_GA)],
                                     w_stage.at[k % W_SLOTS], w_sem.at[k % W_SLOTS])

    @pl.when(jnp.logical_and(pl.program_id(0) == 0, first_tile))
    def _():
        n_chunks = D_MODEL // W_CHUNK
        ahead = W_SLOTS - 1
        for k in range(ahead):
            weight_chunk(k).start()
        for k in range(n_chunks):
            if k + ahead < n_chunks:
                weight_chunk(k + ahead).start()
            weight_chunk(k).wait()
            rows = slice(k * W_CHUNK, (k + 1) * W_CHUNK)
            w_ref[rows, 0:OFF_UX] = w_stage[k % W_SLOTS, :, 0:OFF_UX].astype(BF16)
            for src_off, dst_off in CONV_COLUMN_MOVES:
                w_ref[rows, dst_off:dst_off + CONV_CHUNK] = (
                    w_stage[k % W_SLOTS, :, src_off:src_off + CONV_CHUNK].astype(BF16))

    @pl.when(first_tile)
    def _():
        u_scr[0:HALO, :] = jnp.zeros((HALO, CONV_WIDTH), F32)

    @pl.when(jnp.logical_not(first_tile))
    def _():
        u_scr[0:HALO, :] = u_scr[tm:tm + HALO, :]

    ada = ada_ref[pl.ds(pl.program_id(0), 1), :]
    shift = ada[:, 0:D_MODEL]
    scale = ada[:, D_MODEL:2 * D_MODEL]
    h = (x_ref[...] * (1.0 + scale) + shift).astype(BF16)

    def proj(lo, width):
        return (jnp.dot(h, w_ref[:, lo:lo + width], preferred_element_type=F32)
                + b_ref[:, lo:lo + width])

    slabs = itertools.cycle(range(N_STAGE_SLABS))

    def project_qkv(g):
        d = DILATED_GROUPS[g][1]
        for which in range(3):
            c0 = which * ATTN_WIDTH + g * GROUP_WIDTH
            blk = proj(c0, GROUP_WIDTH)
            if d == 1:
                qkv_scr[pl.ds(base, tm), c0:c0 + GROUP_WIDTH] = blk.astype(BF16)
                continue
            n_t = tm // d
            for lo in range(0, GROUP_WIDTH, LANES):
                cols = slice(c0 + lo, c0 + lo + LANES)
                slab = next(slabs)
                stage_scr[slab] = blk[:, lo:lo + LANES]
                if d == REGROUP_STRIDE:
                    for r in range(d):
                        qkv_scr[pl.ds(base + r * n_t, n_t), cols] = (
                            stage_scr[slab, pl.ds(r, n_t, stride=d), :].astype(BF16))
                else:
                    assert d == REGROUP_STRIDE * REGROUP_STRIDE
                    n_1 = tm // REGROUP_STRIDE
                    slab2 = next(slabs)
                    for r1 in range(REGROUP_STRIDE):
                        stage_scr[slab2, r1 * n_1:(r1 + 1) * n_1, :] = (
                            stage_scr[slab, pl.ds(r1, n_1, stride=REGROUP_STRIDE), :])
                    for r1 in range(REGROUP_STRIDE):
                        for r2 in range(REGROUP_STRIDE):
                            r = r2 * REGROUP_STRIDE + r1
                            qkv_scr[pl.ds(base + r * n_t, n_t), cols] = stage_scr[
                                slab2, pl.ds(r1 * n_1 + r2, n_t, stride=REGROUP_STRIDE), :].astype(BF16)

    project_qkv(1)
    project_qkv(0)

    low_half = lax.broadcasted_iota(jnp.int32, (L, PAIR_WIDTH), 1) < HEAD_DIM
    no_prev = jnp.logical_and(lax.broadcasted_iota(jnp.int32, (2 * L, 2 * L), 1) < L, first_tile)

    def col(g, which, pair):
        return which * ATTN_WIDTH + g * GROUP_WIDTH + pair * PAIR_WIDTH

    kt_slots = itertools.cycle(range(kt_scr.shape[0]))

    def probabilities(g, pair, q, k, maybe_no_prev):
        q = q * 0.125
        zero = jnp.zeros_like(q)
        q2 = jnp.concatenate([jnp.where(low_half, q, zero), jnp.where(low_half, zero, q)], axis=0)
        slot = next(kt_slots)
        n_keys = k.shape[0]
        kt_scr[slot, :, :n_keys] = k.T
        sc = jnp.dot(q2, kt_scr[slot, :, :n_keys], preferred_element_type=F32)
        sc = sc + bias_ref[g, pair, :, 2 * L - k.shape[0]:]
        if maybe_no_prev:
            sc = jnp.where(no_prev, -jnp.inf, sc)
        m = jnp.max(sc, axis=-1, keepdims=True)
        return jnp.exp(sc - m).astype(BF16), m

    def weighted_values(p, m, v):
        v1 = jnp.concatenate([v, jnp.ones(v.shape, BF16)], axis=1)
        pv = jnp.dot(p, v1, preferred_element_type=F32)
        mb = jnp.broadcast_to(m, (2 * L, PAIR_WIDTH))
        return (jnp.where(low_half, pv[:L, :PAIR_WIDTH], pv[L:, :PAIR_WIDTH]),
                jnp.where(low_half, pv[:L, PAIR_WIDTH:], pv[L:, PAIR_WIDTH:]),
                jnp.where(low_half, mb[:L], mb[L:]))

    def merged(pair, rows, acc, den, m):
        m_old = res[2, pair, rows, :]
        m_new = jnp.maximum(m_old, m)
        a = jnp.exp(m_old - m_new)
        b = jnp.exp(m - m_new)
        return a * res[0, pair, rows, :] + b * acc, a * res[1, pair, rows, :] + b * den, m_new

    g1_d = DILATED_GROUPS[1][1]

    def g1_scores(r):
        out = []
        for pair in range(N_PAIRS):
            cur = pl.ds(base + r * L, L)
            prev = pl.ds(prev_base + r * L, L)
            k = jnp.concatenate([qkv_scr[prev, pl.ds(col(1, 1, pair), PAIR_WIDTH)],
                                 qkv_scr[cur, pl.ds(col(1, 1, pair), PAIR_WIDTH)]], axis=0)
            out.append(probabilities(1, pair, qkv_scr[cur, pl.ds(col(1, 0, pair), PAIR_WIDTH)], k, True))
        return out

    def g1_finish(r, parts):
        for pair, (p, m) in enumerate(parts):
            cur = pl.ds(base + r * L, L)
            prev = pl.ds(prev_base + r * L, L)
            v = jnp.concatenate([qkv_scr[prev, pl.ds(col(1, 2, pair), PAIR_WIDTH)],
                                 qkv_scr[cur, pl.ds(col(1, 2, pair), PAIR_WIDTH)]], axis=0)
            acc, den, mm = weighted_values(p, m, v)
            rows = pl.ds(base + r * L, L)
            res[0, pair, rows, :] = acc
            res[1, pair, rows, :] = den
            res[2, pair, rows, :] = mm

    def g0_rows(c):
        q0 = base + c * L
        k0 = jnp.maximum(q0 - L, 0) if c == 0 else q0 - L
        return pl.ds(pl.multiple_of(q0, L), L), pl.ds(pl.multiple_of(k0, L), L)

    def g0_scores(c):
        cur, prev = g0_rows(c)
        out = []
        for pair in range(N_PAIRS):
            k = jnp.concatenate([qkv_scr[prev, pl.ds(col(0, 1, pair), PAIR_WIDTH)],
                                 qkv_scr[cur, pl.ds(col(0, 1, pair), PAIR_WIDTH)]], axis=0)
            out.append(probabilities(0, pair, qkv_scr[cur, pl.ds(col(0, 0, pair), PAIR_WIDTH)], k, c == 0))
        return out

    def g0_finish(c, parts):
        cur, prev = g0_rows(c)
        for pair, (p, m) in enumerate(parts):
            v = jnp.concatenate([qkv_scr[prev, pl.ds(col(0, 2, pair), PAIR_WIDTH)],
                                 qkv_scr[cur, pl.ds(col(0, 2, pair), PAIR_WIDTH)]], axis=0)
            slab = c * N_PAIRS + pair
            for k, val in enumerate(weighted_values(p, m, v)):
                tmp_scr[slab, k] = val
            n_r = L // RES_STRIDE
            for r in range(RES_STRIDE):
                rows = pl.ds(pl.multiple_of(base + r * (tm // RES_STRIDE) + c * n_r, n_r), n_r)
                acc, den, mm = merged(pair, rows, *[tmp_scr[slab, k, pl.ds(r, n_r, stride=RES_STRIDE), :]
                                                    for k in range(3)])
                res[0, pair, rows, :] = acc
                res[1, pair, rows, :] = den
                res[2, pair, rows, :] = mm

    items = ([(g1_scores, g1_finish, (j,)) for j in range(tm // L)]
             + [(g0_scores, g0_finish, (j,)) for j in range(tm // L)])

    def start(item):
        scores_fn, finish_fn, args = item
        return finish_fn, [(a, scores_fn(a)) for a in args]

    def finish(pending):
        finish_fn, parts = pending
        for a, part in parts:
            finish_fn(a, part)

    in_flight = []
    pending_items = list(items)

    def rotate():
        in_flight.append(start(pending_items.pop(0)))
        if len(in_flight) > 2:
            finish(in_flight.pop(0))

    for c0 in range(0, CONV_WIDTH, CONV_CHUNK):
        cs = slice(c0, c0 + CONV_CHUNK)
        lo = OFF_UX + 4 * c0
        gu = (jnp.dot(h, w_ref[:, lo:lo + 2 * CONV_CHUNK], preferred_element_type=F32)
              + jnp.concatenate([b_ref[:, OFF_GC + c0:OFF_GC + c0 + CONV_CHUNK],
                                 b_ref[:, OFF_UX + c0:OFF_UX + c0 + CONV_CHUNK]], axis=1))
        u = gu[:, :CONV_CHUNK] * gu[:, CONV_CHUNK:]
        u_scr[HALO:HALO + tm, cs] = u
        conv = cw_ref[0:1, cs] * u_scr[HALO - 2:HALO - 2 + tm, cs]
        conv = conv + cw_ref[1:2, cs] * u_scr[HALO - 1:HALO - 1 + tm, cs]
        conv = conv + cw_ref[2:3, cs] * u
        rotate()
        gz = (jnp.dot(h, w_ref[:, lo + 2 * CONV_CHUNK:lo + 4 * CONV_CHUNK], preferred_element_type=F32)
              + jnp.concatenate([b_ref[:, OFF_GB + c0:OFF_GB + c0 + CONV_CHUNK],
                                 b_ref[:, OFF_ZC + c0:OFF_ZC + c0 + CONV_CHUNK]], axis=1))
        t = gz[:, :CONV_CHUNK] * conv * _silu(gz[:, CONV_CHUNK:])
        t_scr[:, cs] = t.astype(BF16)
        rotate()
    finish(in_flight.pop(0))
    za_ref[...] = _silu(proj(OFF_Z_ATTN, GROUP_WIDTH)).astype(BF16)
    yc_ref[...] = jnp.dot(t_scr[...], wpc_ref[...], preferred_element_type=F32).astype(BF16)
    finish(in_flight.pop(0))
    project_qkv(2)

    g2_d = DILATED_GROUPS[2][1]
    n_t2 = tm // g2_d

    def g2_piece(which, r, pair):
        pieces = [qkv_scr[pl.ds(pl.multiple_of(t * tm + r * n_t2, n_t2), n_t2),
                          pl.ds(col(2, which, pair), PAIR_WIDTH)] for t in range(S // tm)]
        return jnp.concatenate(pieces, axis=0)

    @pl.when(tile == S // tm - 1)
    def _():
        def body(step, carry):
            work = [(u, pair) for u in range(RES_STRIDE) for pair in range(N_PAIRS)]
            parts = [probabilities(2, pair, g2_piece(0, step * RES_STRIDE + u, pair),
                                   g2_piece(1, step * RES_STRIDE + u, pair), False) for u, pair in work]
            for (u, pair), (p, m) in zip(work, parts):
                r = step * RES_STRIDE + u
                new = weighted_values(p, m, g2_piece(2, r, pair))
                outs = []
                for t in range(S // tm):
                    start_row = t * tm + u * (tm // RES_STRIDE) + step
                    rows = pl.ds(start_row, n_t2, stride=RES_STRIDE)
                    acc, den, _ = merged(pair, rows, *[x[t * n_t2:(t + 1) * n_t2] for x in new])
                    outs.append(acc / den)
                o_ref[pair, pl.ds(r, L, stride=g2_d), :] = jnp.concatenate(outs, axis=0)
            return carry
        lax.fori_loop(0, g2_d // RES_STRIDE, body, 0)


def _proj_attn(x, ada, w_in, b_in, conv_w, w_pc):
    bsz, s, d = x.shape
    tm = ROW_TILE
    bias = jnp.asarray(_attn_bias_table())
    const = dict(pipeline_mode=pl.Buffered(1))
    row = lambda width: pl.BlockSpec((None, tm, width), lambda b, i: (b, i, 0))
    return pl.pallas_call(
        _proj_attn_kernel,
        grid=(bsz, s // tm),
        in_specs=[row(d),
                  pl.BlockSpec(ada.shape, lambda b, i: (0, 0), **const),
                  pl.BlockSpec(memory_space=pl.ANY),
                  pl.BlockSpec((1, D_IN), lambda b, i: (0, 0), **const),
                  pl.BlockSpec((CONV_K, CONV_WIDTH), lambda b, i: (0, 0), **const),
                  pl.BlockSpec((CONV_WIDTH, d), lambda b, i: (0, 0), **const),
                  pl.BlockSpec(bias.shape, lambda b, i: (0, 0, 0, 0), **const)],
        out_specs=[row(GROUP_WIDTH), row(d),
                   pl.BlockSpec((None, N_PAIRS, s, PAIR_WIDTH), lambda b, i: (b, 0, 0, 0))],
        out_shape=[jax.ShapeDtypeStruct((bsz, s, GROUP_WIDTH), BF16),
                   jax.ShapeDtypeStruct((bsz, s, d), BF16),
                   jax.ShapeDtypeStruct((bsz, N_PAIRS, s, PAIR_WIDTH), F32)],
        scratch_shapes=[pltpu.VMEM((tm + HALO, CONV_WIDTH), F32),
                        pltpu.VMEM((tm, CONV_WIDTH), BF16),
                        pltpu.VMEM((N_STAGE_SLABS, tm, LANES), F32),
                        pltpu.VMEM((s, QKV_WIDTH), BF16),
                        pltpu.VMEM((3, N_PAIRS, s, PAIR_WIDTH), F32),
                        pltpu.VMEM((N_PAIRS * (tm // SUB_BLOCK), 3, SUB_BLOCK, PAIR_WIDTH), F32),
                        pltpu.VMEM((4, PAIR_WIDTH, 2 * SUB_BLOCK), BF16),
                        pltpu.VMEM((d, OFF_GA), BF16),
                        pltpu.VMEM((W_SLOTS, W_CHUNK, OFF_GA), F32),
                        pltpu.SemaphoreType.DMA((W_SLOTS,))],
        compiler_params=pltpu.CompilerParams(
            dimension_semantics=("arbitrary", "arbitrary"), vmem_limit_bytes=PROJ_VMEM_LIMIT),
        name="proj_attn",
    )(x, ada, w_in, b_in, conv_w, w_pc, bias)


def _out_kernel(x_ref, ada_ref, oa_ref, za_ref, yc_ref, *rest, alpha):
    n_wg = 2 * D_MODEL // OUT_WG_BLOCK
    wg_f32 = rest[:n_wg]
    (b_ref, wpa_f32, wout_f32, bout_ref, lng_ref, lnb_ref,
     out_ref, m_scr, wg_ref, wpa_ref, wout_ref) = rest[n_wg:]

    @pl.when(jnp.logical_and(pl.program_id(0) == 0, pl.program_id(1) == 0))
    def _():
        per_blk = OUT_WG_BLOCK // OUT_COL_CHUNK
        for gate_idx in range(2):
            for j in range(D_MODEL // OUT_COL_CHUNK):
                blk = wg_f32[gate_idx * (D_MODEL // OUT_WG_BLOCK) + j // per_blk]
                piece = blk[:, (j % per_blk) * OUT_COL_CHUNK:(j % per_blk + 1) * OUT_COL_CHUNK]
                dst = (2 * j + gate_idx) * OUT_COL_CHUNK
                wg_ref[:, dst:dst + OUT_COL_CHUNK] = piece.astype(BF16)
        wpa_ref[...] = wpa_f32[...].astype(BF16)
        wout_ref[...] = wout_f32[...].astype(BF16)

    ada = ada_ref[pl.ds(pl.program_id(0), 1), :]
    shift = ada[:, 0:D_MODEL]
    scale = ada[:, D_MODEL:2 * D_MODEL]
    gate = ada[:, 2 * D_MODEL:] * (1.0 / alpha)
    eps = LN_EPS / (alpha * alpha)
    h = (x_ref[...] * (1.0 + scale) + shift).astype(BF16)
    o_attn = jnp.concatenate([oa_ref[pair] for pair in range(N_PAIRS)], axis=1)
    ta = (o_attn * za_ref[...].astype(F32)).astype(BF16)

    for c0 in range(0, D_MODEL, OUT_COL_CHUNK):
        cs = slice(c0, c0 + OUT_COL_CHUNK)
        bias = jnp.concatenate([b_ref[:, OFF_GA + c0:OFF_GA + c0 + OUT_COL_CHUNK],
                                b_ref[:, OFF_GA + D_MODEL + c0:OFF_GA + D_MODEL + c0 + OUT_COL_CHUNK]], axis=1)
        g = _sigmoid(jnp.dot(h, wg_ref[:, 2 * c0:2 * c0 + 2 * OUT_COL_CHUNK], preferred_element_type=F32)
                     + bias).astype(BF16)
        y_attn = jnp.dot(ta, wpa_ref[:, cs], preferred_element_type=F32)
        m_scr[:, cs] = g[:, :OUT_COL_CHUNK] * y_attn.astype(BF16) + g[:, OUT_COL_CHUNK:] * yc_ref[:, cs]

    for r0 in range(0, x_ref.shape[0], OUT_CHUNK):
        rows = slice(r0, r0 + OUT_CHUNK)
        sub = gate * (jnp.dot(m_scr[rows, :], wout_ref[...], preferred_element_type=F32) + bout_ref[...])
        r = x_ref[rows, :] + sub
        mu = jnp.mean(r, axis=-1, keepdims=True)
        cen = r - mu
        var = jnp.mean(cen * cen, axis=-1, keepdims=True)
        out_ref[rows, :] = cen * lax.rsqrt(var + eps) * lng_ref[...] + lnb_ref[...]


def _out(x, ada, o_attn, za, yc, w_in, b_in, w_pa, w_out, b_out, ln_g, ln_b, alpha):
    bsz, s, d = x.shape
    tm = OUT_ROW_TILE
    const = dict(pipeline_mode=pl.Buffered(1))
    row = lambda width: pl.BlockSpec((None, tm, width), lambda b, i: (b, i, 0))
    vec = pl.BlockSpec((1, d), lambda b, i: (0, 0), **const)
    n_wg = 2 * d // OUT_WG_BLOCK
    wg_specs = [pl.BlockSpec((d, OUT_WG_BLOCK), lambda b, i, k=k: (0, OFF_GA // OUT_WG_BLOCK + k), **const)
                for k in range(n_wg)]
    return pl.pallas_call(
        functools.partial(_out_kernel, alpha=alpha),
        grid=(bsz, s // tm),
        in_specs=[row(d), pl.BlockSpec(ada.shape, lambda b, i: (0, 0), **const),
                  pl.BlockSpec((None, N_PAIRS, tm, PAIR_WIDTH), lambda b, i: (b, 0, i, 0)),
                  row(GROUP_WIDTH), row(d)]
                 + wg_specs
                 + [pl.BlockSpec((1, D_IN), lambda b, i: (0, 0), **const),
                    pl.BlockSpec((GROUP_WIDTH, d), lambda b, i: (0, 0), **const),
                    pl.BlockSpec((d, d), lambda b, i: (0, 0), **const),
                    vec, vec, vec],
        out_specs=row(d),
        out_shape=jax.ShapeDtypeStruct((bsz, s, d), F32),
        scratch_shapes=[pltpu.VMEM((tm, d), BF16),
                        pltpu.VMEM((d, 2 * d), BF16),
                        pltpu.VMEM((GROUP_WIDTH, d), BF16),
                        pltpu.VMEM((d, d), BF16)],
        compiler_params=pltpu.CompilerParams(
            dimension_semantics=("arbitrary", "arbitrary"), vmem_limit_bytes=VMEM_LIMIT),
        name="out",
    )(x, ada, o_attn, za, yc, *([w_in] * n_wg), b_in, w_pa, w_out, b_out, ln_g, ln_b)


def kernel(x, c, w_ada, b_ada, w_in, b_in, conv_w, w_proj_attn, w_proj_conv, w_out, b_out, ln_g, ln_b):
    bsz, s, d = x.shape
    depth = w_in.shape[0]
    assert d == D_MODEL and s % OUT_ROW_TILE == 0 and w_in.shape[1:] == (D_MODEL, D_IN)
    assert s // DILATED_GROUPS[-1][1] == SUB_BLOCK
    alpha = (2.0 * depth) ** 0.25
    for layer in range(depth):
        ada = _ada(c, w_ada[layer], b_ada[layer])
        b_in2 = b_in[layer].reshape(1, D_IN)
        za, yc, o_attn = _proj_attn(x, ada, w_in[layer], b_in2,
                                    conv_w[layer], w_proj_conv[layer].astype(BF16))
        x = _out(x, ada, o_attn, za, yc, w_in[layer], b_in2, w_proj_attn[layer], w_out[layer],
                 b_out[layer].reshape(1, d), ln_g[layer].reshape(1, d), ln_b[layer].reshape(1, d), alpha)
    return x
```
